```python
import math
import jax, jax.numpy as jnp
from jax import lax
import numpy as np

D_MODEL = 2048
BATCH = 4
SEQ = 4096
DEPTH = 2

GRID_W = 64
CTX_LEN = 256
N_EVEN = (DEPTH + 1) // 2
N_ODD = DEPTH // 2
EPS = 1e-6
CONV_W = 4
CONV_PAD = (CONV_W // 2, CONV_W - 1 - CONV_W // 2)

S5_W = D_MODEL // 2
S5_CH = 16
S5_G = S5_W // S5_CH
S5_N = 64
SSD_P = 64
SSD_H = D_MODEL // SSD_P
SSD_DI = SSD_H * SSD_P
SSD_N = 128
SSD_G = 4
SSD_CHUNK = 128
SSD_CONV_CH = SSD_DI + 2 * SSD_G * SSD_N
EV_IN = S5_W + SSD_DI + SSD_CONV_CH + 2 * SSD_H
EV_MIX = S5_W + SSD_DI
DN_DK = 128
DN_DV = 128
DN_H = D_MODEL // DN_DK
DN_CHUNK = 64
DN_QKV = DN_H * (2 * DN_DK + DN_DV)
RG_W = D_MODEL // 2
RG_H = 8
RG_BS = RG_W // RG_H
RG_C = 8.0
OD_IN = DN_QKV + DN_H * DN_DV + 4 * DN_H + 2 * RG_W
OD_MIX = DN_H * DN_DV + RG_W
OD_CUTS = [DN_QKV,
           DN_QKV + DN_H * DN_DV,
           DN_QKV + DN_H * DN_DV + 2 * DN_H,
           DN_QKV + DN_H * DN_DV + 4 * DN_H,
           DN_QKV + DN_H * DN_DV + 4 * DN_H + RG_W]
EV_CUTS = [S5_W, S5_W + SSD_DI, S5_W + SSD_DI + SSD_CONV_CH]
PEER_H = 8
PEER_DK = 256
PEER_HALF = PEER_DK // 2
PEER_NK = 128
PEER_E = PEER_NK * PEER_NK
PEER_TOPK = 16
PEER_BLOCK = 128

kernel_name = "hybrid_s5_ssd_deltanet_rglru_peer_dit"

F32 = jnp.float32


def rmsnorm(x, g):
    xf = x.astype(F32)
    y = xf * lax.rsqrt(jnp.mean(xf * xf, axis=-1, keepdims=True) + EPS)
    return (y * g.astype(F32)).astype(x.dtype)


def l2norm(x):
    return x * lax.rsqrt(jnp.sum(x * x, axis=-1, keepdims=True) + EPS)


def modulate(h, shift, scale):
    return h * (1 + scale) + shift


def dwconv(x, w, b):
    y = lax.conv_general_dilated(x, w[:, None, :], (1,), [CONV_PAD],
                                 dimension_numbers=('NWC', 'WIO', 'NWC'),
                                 feature_group_count=x.shape[-1])
    return y + b


def flip(t):
    return jnp.flip(t, axis=1)


def to_col_major(x, rows):
    b, l, d = x.shape
    return x.reshape(b, rows, GRID_W, d).transpose(0, 2, 1, 3).reshape(b, l, d)


def to_row_major(x, rows):
    b, l, d = x.shape
    return x.reshape(b, GRID_W, rows, d).transpose(0, 2, 1, 3).reshape(b, l, d)


def linear_scan(a, b, h0):
    b = b.at[:, 0].add(a[:, 0] * h0)

    def comb(e1, e2):
        return e1[0] * e2[0], e2[0] * e1[1] + e2[1]
    return lax.associative_scan(comb, (a, b), axis=1)[1]


def _cplx_comb(e1, e2):
    ar1, ai1, br1, bi1 = e1
    ar2, ai2, br2, bi2 = e2
    return (ar2 * ar1 - ai2 * ai1, ar2 * ai1 + ai2 * ar1,
            ar2 * br1 - ai2 * bi1 + br2, ar2 * bi1 + ai2 * br1 + bi2)


def s5_direction(u, h0r, h0i, lam_re, lam_im, log_dt, b_re, b_im, c_re, c_im):
    lam_re, lam_im, b_re, b_im, c_re, c_im = (t.astype(F32) for t in (lam_re, lam_im, b_re, b_im, c_re, c_im))
    dt = jnp.exp(log_dt.astype(F32))[:, None]
    mag = jnp.exp(lam_re * dt)
    ar, ai = mag * jnp.cos(lam_im * dt), mag * jnp.sin(lam_im * dt)
    den = lam_re * lam_re + lam_im * lam_im
    cr = ((ar - 1) * lam_re + ai * lam_im) / den
    ci = (ai * lam_re - (ar - 1) * lam_im) / den
    bb_re = cr[..., None] * b_re - ci[..., None] * b_im
    bb_im = cr[..., None] * b_im + ci[..., None] * b_re
    seq_len = u.shape[1]
    a_re = jnp.broadcast_to(ar, (seq_len,) + ar.shape)
    a_im = jnp.broadcast_to(ai, (seq_len,) + ai.shape)

    def one(args):
        ub, hr, hi = args
        xr = jnp.einsum('lgc,gnc->lgn', ub, bb_re)
        xi = jnp.einsum('lgc,gnc->lgn', ub, bb_im)
        xr = xr.at[0].add(ar * hr - ai * hi)
        xi = xi.at[0].add(ar * hi + ai * hr)
        _, _, sr, si = lax.associative_scan(_cplx_comb, (a_re, a_im, xr, xi), axis=0)
        y = jnp.einsum('lgn,gcn->lgc', sr, c_re) - jnp.einsum('lgn,gcn->lgc', si, c_im)
        return y, sr[-1], si[-1]
    return lax.map(one, (u, h0r, h0i))


def s5_stream(u, state, lam_re, lam_im, log_dt, b_re, b_im, c_re, c_im, d, glu_w, glu_b):
    bn, seq_len, _ = u.shape
    ug = u.astype(F32).reshape(bn, seq_len, S5_G, S5_CH)
    yf, fr, fi = s5_direction(ug, state[0], state[1], lam_re[0], lam_im[0], log_dt[0],
                              b_re[0], b_im[0], c_re[0], c_im[0])
    yb, br, bi = s5_direction(flip(ug), state[2], state[3], lam_re[1], lam_im[1], log_dt[1],
                              b_re[1], b_im[1], c_re[1], c_im[1])
    y = yf + flip(yb) + d.astype(F32).reshape(S5_G, S5_CH) * ug
    y = jax.nn.gelu(y.reshape(bn, seq_len, S5_W)).astype(u.dtype)
    out = y * jax.nn.sigmoid(y @ glu_w + glu_b)
    return out, (fr, fi, br, bi)


def ssd_scan(x, dt, a_neg, bm, cm, h0):
    bn, seq_len, nh, hp = x.shape
    ng, ns = bm.shape[2], bm.shape[3]
    nr = nh // ng
    q = SSD_CHUNK
    nc = seq_len // q
    xdt = (x * dt[..., None]).reshape(bn, nc, q, ng, nr, hp)
    cum = jnp.cumsum((dt * a_neg).reshape(bn, nc, q, ng, nr), axis=2)
    bm = bm.reshape(bn, nc, q, ng, ns)
    cm = cm.reshape(bn, nc, q, ng, ns)
    idx = jnp.arange(q)
    incl = idx[:, None] >= idx[None, :]
    cum_t = jnp.moveaxis(cum, 2, -1)
    seg = jnp.exp(jnp.where(incl, cum_t[..., :, None] - cum_t[..., None, :], -jnp.inf))
    cb = jnp.einsum('bclgn,bcsgn->bcgls', cm, bm)
    y_diag = jnp.einsum('bcgrls,bcsgrp->bclgrp', seg * cb[:, :, :, None], xdt)

    def step(h, inp):
        c_c, b_c, xdt_c, cum_c = inp
        y_off = jnp.einsum('blgn,bgrpn->blgrp', c_c, h) * jnp.exp(cum_c)[..., None]
        dte = jnp.exp(cum_c[:, -1:] - cum_c)
        st = jnp.einsum('blgn,blgrp->bgrpn', b_c, xdt_c * dte[..., None])
        h = h * jnp.exp(cum_c[:, -1])[..., None, None] + st
        return h, y_off
    xs = tuple(jnp.moveaxis(t, 1, 0) for t in (cm, bm, xdt, cum))
    h_last, y_off = lax.scan(step, h0.reshape(bn, ng, nr, hp, ns), xs)
    y = y_diag + jnp.moveaxis(y_off, 0, 1)
    return y.reshape(bn, seq_len, nh, hp), h_last.reshape(bn, nh, hp, ns)


def ssd_stream(z, xbc, dt_raw, state, conv_w, conv_b, dt_bias, a_log, d, norm_g):
    bn, seq_len, _ = z.shape
    xbc = jax.nn.silu(dwconv(xbc, conv_w, conv_b)).astype(F32)
    xs, bm, cm = jnp.split(xbc, [SSD_DI, SSD_DI + SSD_G * SSD_N], axis=-1)
    xs = xs.reshape(bn, seq_len, SSD_H, SSD_P)
    bm = bm.reshape(bn, seq_len, SSD_G, SSD_N)
    cm = cm.reshape(bn, seq_len, SSD_G, SSD_N)
    dt = jax.nn.softplus(dt_raw.astype(F32).reshape(bn, seq_len, 2, SSD_H) + dt_bias.astype(F32))
    a_neg = -jnp.exp(a_log.astype(F32))
    yf, hf = ssd_scan(xs, dt[:, :, 0], a_neg[0], bm, cm, state[0])
    yb, hb = ssd_scan(flip(xs), flip(dt[:, :, 1]), a_neg[1], flip(bm), flip(cm), state[1])
    y = yf + flip(yb) + d.astype(F32)[:, None] * xs
    y = y.reshape(bn, seq_len, SSD_DI) * jax.nn.silu(z.astype(F32))
    return rmsnorm(y, norm_g).astype(z.dtype), (hf, hb)


def gated_delta_rule(q, k, v, beta, g, s0):
    bn, seq_len, nh, dk = q.shape
    dv = v.shape[-1]
    qn = DN_CHUNK
    nc = seq_len // qn

    def chunks(t):
        return jnp.swapaxes(t.reshape((bn, nc, qn) + t.shape[2:]), 2, 3)
    qc, kc, vc, bc, gc = (chunks(t) for t in (q, k, v, beta, g))
    gcum = jnp.cumsum(gc, axis=-1)
    idx = jnp.arange(qn)
    incl = idx[:, None] >= idx[None, :]
    strict = idx[:, None] > idx[None, :]
    dmat = jnp.exp(jnp.where(incl, gcum[..., :, None] - gcum[..., None, :], -jnp.inf))
    kb = kc * bc[..., None]
    m = jnp.where(strict, jnp.einsum('bnhid,bnhjd->bnhij', kb, kc) * dmat, 0.0)
    rhs = jnp.concatenate([vc * bc[..., None], kb * jnp.exp(gcum)[..., None]], axis=-1)
    sol = lax.linalg.triangular_solve(m + jnp.eye(qn, dtype=m.dtype), rhs,
                                      left_side=True, lower=True, unit_diagonal=True)
    u, w = sol[..., :dv], sol[..., dv:]
    qk = jnp.einsum('bnhid,bnhjd->bnhij', qc, kc) * dmat
    q_dec = qc * jnp.exp(gcum)[..., None]
    k_tail = kc * jnp.exp(gcum[..., -1:] - gcum)[..., None]
    tot = jnp.exp(gcum[..., -1])

    def step(s, inp):
        u_c, w_c, qk_c, qd_c, kt_c, tot_c = inp
        v_new = u_c - jnp.einsum('bhqk,bhkv->bhqv', w_c, s)
        o = jnp.einsum('bhqk,bhkv->bhqv', qd_c, s) + jnp.einsum('bhij,bhjv->bhiv', qk_c, v_new)
        s = s * tot_c[..., None, None] + jnp.einsum('bhqk,bhqv->bhkv', kt_c, v_new)
        return s, o
    xs = tuple(jnp.moveaxis(t, 1, 0) for t in (u, w, qk, q_dec, k_tail, tot))
    s_last, o = lax.scan(step, s0, xs)
    o = jnp.swapaxes(jnp.moveaxis(o, 0, 1), 2, 3).reshape(bn, seq_len, nh, dv)
    return o, s_last


def deltanet_stream(qkv, z, a_raw, b_raw, state, conv_w, conv_b, dt_bias, a_log, norm_g):
    bn, seq_len, _ = z.shape
    qkv = jax.nn.silu(dwconv(qkv, conv_w, conv_b)).astype(F32)
    q, k, v = jnp.split(qkv, [DN_H * DN_DK, 2 * DN_H * DN_DK], axis=-1)
    q = l2norm(q.reshape(bn, seq_len, DN_H, DN_DK)) * (DN_DK ** -0.5)
    k = l2norm(k.reshape(bn, seq_len, DN_H, DN_DK))
    v = v.reshape(bn, seq_len, DN_H, DN_DV)
    beta = jax.nn.sigmoid(b_raw.astype(F32).reshape(bn, seq_len, 2, DN_H))
    g = -jnp.exp(a_log.astype(F32)) * jax.nn.softplus(
        a_raw.astype(F32).reshape(bn, seq_len, 2, DN_H) + dt_bias.astype(F32))
    of, sf = gated_delta_rule(q, k, v, beta[:, :, 0], g[:, :, 0], state[0])
    ob, sb = gated_delta_rule(flip(q), flip(k), flip(v), flip(beta[:, :, 1]), flip(g[:, :, 1]), state[1])
    o = rmsnorm(of + flip(ob), norm_g)
    o = o.reshape(bn, seq_len, DN_H * DN_DV) * jax.nn.silu(z.astype(F32))
    return o.astype(z.dtype), (sf, sb)


def rglru_stream(xr, gate, state, conv_w, conv_b, wa, ba, wx, bx, lam):
    bn, seq_len, _ = xr.shape
    xh = dwconv(xr, conv_w, conv_b).astype(F32).reshape(bn, seq_len, RG_H, RG_BS)

    def coeffs(dr):
        r = jax.nn.sigmoid(jnp.einsum('blhi,hij->blhj', xh, wa[dr].astype(F32)) + ba[dr].astype(F32))
        i = jax.nn.sigmoid(jnp.einsum('blhi,hij->blhj', xh, wx[dr].astype(F32)) + bx[dr].astype(F32))
        log_a = -RG_C * r * jax.nn.softplus(-lam[dr].astype(F32))
        return jnp.exp(log_a), jnp.sqrt(-jnp.expm1(2.0 * log_a)) * (i * xh)
    af, bf = coeffs(0)
    hf = linear_scan(af, bf, state[0])
    ab, bb = coeffs(1)
    hb = linear_scan(flip(ab), flip(bb), state[1])
    y = (hf + flip(hb)).reshape(bn, seq_len, RG_W) * jax.nn.gelu(gate.astype(F32))
    return y.astype(xr.dtype), (hf[:, -1], hb[:, -1])


def even_mixer(hx, hc, w_in, w_out, s5_params, ssd_params, need_ctx):
    bn = hx.shape[0]

    def stream(h, st_s5, st_ssd):
        p = h @ w_in
        u, z, xbc, dt_raw = jnp.split(p, EV_CUTS, axis=-1)
        ya, st_a = s5_stream(u, st_s5, *s5_params)
        yb, st_b = ssd_stream(z, xbc, dt_raw, st_ssd, *ssd_params)
        return jnp.concatenate([ya, yb], axis=-1), st_a, st_b
    zs5 = jnp.zeros((bn, S5_G, S5_N), F32)
    zssd = jnp.zeros((bn, SSD_H, SSD_P, SSD_N), F32)
    yc, st_a, st_b = stream(hc, (zs5, zs5, zs5, zs5), (zssd, zssd))
    yx, _, _ = stream(hx, st_a, st_b)
    return yx @ w_out, (yc @ w_out if need_ctx else None)


def odd_mixer(hx, hc, w_in, w_out, dn_params, rg_params, need_ctx):
    bn = hx.shape[0]

    def stream(h, st_dn, st_rg):
        p = h @ w_in
        qkv, z, a_raw, b_raw, xr, gate = jnp.split(p, OD_CUTS, axis=-1)
        yd, st_d = deltanet_stream(qkv, z, a_raw, b_raw, st_dn, *dn_params)
        yr, st_r = rglru_stream(xr, gate, st_rg, *rg_params)
        return jnp.concatenate([yd, yr], axis=-1), st_d, st_r
    zdn = jnp.zeros((bn, DN_H, DN_DK, DN_DV), F32)
    zrg = jnp.zeros((bn, RG_H, RG_BS), F32)
    yc, st_d, st_r = stream(hc, (zdn, zdn), (zrg, zrg))
    yx, _, _ = stream(hx, st_d, st_r)
    return yx @ w_out, (yc @ w_out if need_ctx else None)


def peer(h, w_q, k1, k2, u_tab, v_tab):
    bn, seq_len, dm = h.shape
    q = (h @ w_q).reshape(bn, seq_len, PEER_H, PEER_DK)
    s1 = jnp.einsum('blhd,hkd->blhk', q[..., :PEER_HALF], k1).astype(F32)
    s2 = jnp.einsum('blhd,hkd->blhk', q[..., PEER_HALF:], k2).astype(F32)
    t1, i1 = lax.top_k(s1, PEER_TOPK)
    t2, i2 = lax.top_k(s2, PEER_TOPK)
    cand = (t1[..., :, None] + t2[..., None, :]).reshape(bn, seq_len, PEER_H, PEER_TOPK * PEER_TOPK)
    top, ci = lax.top_k(cand, PEER_TOPK)
    idx = (jnp.take_along_axis(i1, ci // PEER_TOPK, axis=-1) * PEER_NK
           + jnp.take_along_axis(i2, ci % PEER_TOPK, axis=-1))
    gate = jax.nn.softmax(top, axis=-1).astype(h.dtype)
    nb = bn * seq_len // PEER_BLOCK
    hb = h.reshape(nb, PEER_BLOCK, dm)
    ib = idx.reshape(nb, PEER_BLOCK, PEER_H * PEER_TOPK)
    gb = gate.reshape(nb, PEER_BLOCK, PEER_H * PEER_TOPK)

    def block(args):
        hh, ii, gg = args
        act = jax.nn.gelu(jnp.einsum('td,tkd->tk', hh, u_tab[ii]))
        return jnp.einsum('tk,tkd->td', gg * act, v_tab[ii])
    return lax.map(block, (hb, ib, gb)).reshape(bn, seq_len, dm)


def setup_inputs(seed: int = 0) -> dict:
    key = jax.random.key(seed)
    keys = jax.random.split(key, 64)
    counter = [0]

    def nk():
        k = keys[counter[0]]
        counter[0] += 1
        return k

    def nrm(shape, std=1.0):
        return std * jax.random.normal(nk(), shape, F32)

    def unif(shape, lo, hi):
        return jax.random.uniform(nk(), shape, F32, lo, hi)

    def gain(shape):
        return 1.0 + nrm(shape, 0.02)

    def dt_bias(shape):
        dt = jnp.exp(unif(shape, math.log(1e-3), math.log(1e-1)))
        return dt + jnp.log(-jnp.expm1(-dt))

    n_idx = jnp.arange(S5_N, dtype=F32)
    rg_p = unif((N_ODD, 2, RG_H, RG_BS), 0.9, 0.999) ** (1.0 / RG_C)
    return {
        'x': nrm((BATCH, SEQ, D_MODEL)),
        'c': nrm((BATCH, D_MODEL)),
        'ctx': nrm((BATCH, CTX_LEN, D_MODEL)),
        'c_ctx': nrm((D_MODEL,)),
        'ada_w': nrm((DEPTH, D_MODEL, 6 * D_MODEL), 0.5 * D_MODEL ** -0.5),
        'ada_b': nrm((DEPTH, 6 * D_MODEL), 0.02),
        'norm1_g': gain((DEPTH, D_MODEL)),
        'norm2_g': gain((DEPTH, D_MODEL)),
        'final_g': gain((D_MODEL,)),
        'ev_w_in': nrm((N_EVEN, D_MODEL, EV_IN), D_MODEL ** -0.5),
        'ev_w_out': nrm((N_EVEN, EV_MIX, D_MODEL), EV_MIX ** -0.5),
        's5_lam_re': -0.5 + nrm((N_EVEN, 2, S5_G, S5_N), 0.01),
        's5_lam_im': jnp.pi * n_idx + nrm((N_EVEN, 2, S5_G, S5_N), 0.01),
        's5_log_dt': unif((N_EVEN, 2, S5_G), math.log(1e-3), math.log(1e-1)),
        's5_b_re': nrm((N_EVEN, 2, S5_G, S5_N, S5_CH), (2 * S5_CH) ** -0.5),
        's5_b_im': nrm((N_EVEN, 2, S5_G, S5_N, S5_CH), (2 * S5_CH) ** -0.5),
        's5_c_re': nrm((N_EVEN, 2, S5_G, S5_CH, S5_N), (2 * S5_N) ** -0.5),
        's5_c_im': nrm((N_EVEN, 2, S5_G, S5_CH, S5_N), (2 * S5_N) ** -0.5),
        's5_d': nrm((N_EVEN, S5_W)),
        's5_glu_w': nrm((N_EVEN, S5_W, S5_W), S5_W ** -0.5),
        's5_glu_b': nrm((N_EVEN, S5_W), 0.02),
        'ssd_conv_w': nrm((N_EVEN, CONV_W, SSD_CONV_CH), CONV_W ** -0.5),
        'ssd_conv_b': nrm((N_EVEN, SSD_CONV_CH), 0.02),
        'ssd_dt_bias': dt_bias((N_EVEN, 2, SSD_H)),
        'ssd_a_log': jnp.log(unif((N_EVEN, 2, SSD_H), 1.0, 16.0)),
        'ssd_d': gain((N_EVEN, SSD_H)),
        'ssd_norm_g': gain((N_EVEN, SSD_DI)),
        'od_w_in': nrm((N_ODD, D_MODEL, OD_IN), D_MODEL ** -0.5),
        'od_w_out': nrm((N_ODD, OD_MIX, D_MODEL), OD_MIX ** -0.5),
        'dn_conv_w': nrm((N_ODD, CONV_W, DN_QKV), CONV_W ** -0.5),
        'dn_conv_b': nrm((N_ODD, DN_QKV), 0.02),
        'dn_dt_bias': dt_bias((N_ODD, 2, DN_H)),
        'dn_a_log': jnp.log(unif((N_ODD, 2, DN_H), 1.0, 16.0)),
        'dn_norm_g': gain((N_ODD, DN_DV)),
        'rg_conv_w': nrm((N_ODD, CONV_W, RG_W), CONV_W ** -0.5),
        'rg_conv_b': nrm((N_ODD, RG_W), 0.02),
        'rg_wa': nrm((N_ODD, 2, RG_H, RG_BS, RG_BS), RG_BS ** -0.5),
        'rg_ba': nrm((N_ODD, 2, RG_H, RG_BS), 0.02),
        'rg_wx': nrm((N_ODD, 2, RG_H, RG_BS, RG_BS), RG_BS ** -0.5),
        'rg_bx': nrm((N_ODD, 2, RG_H, RG_BS), 0.02),
        'rg_lam': jnp.log(rg_p) - jnp.log1p(-rg_p),
        'peer_wq': nrm((DEPTH, D_MODEL, PEER_H * PEER_DK), D_MODEL ** -0.5),
        'peer_k1': nrm((DEPTH, PEER_H, PEER_NK, PEER_HALF), PEER_HALF ** -0.5),
        'peer_k2': nrm((DEPTH, PEER_H, PEER_NK, PEER_HALF), PEER_HALF ** -0.5),
        'peer_u': nrm((DEPTH, PEER_E, D_MODEL), D_MODEL ** -0.5),
        'peer_v': nrm((DEPTH, PEER_E, D_MODEL), PEER_H ** -0.5),
    }


def reference(x, c, ctx, c_ctx, ada_w, ada_b, norm1_g, norm2_g, final_g,
              ev_w_in, ev_w_out, s5_lam_re, s5_lam_im, s5_log_dt, s5_b_re, s5_b_im,
              s5_c_re, s5_c_im, s5_d, s5_glu_w, s5_glu_b,
              ssd_conv_w, ssd_conv_b, ssd_dt_bias, ssd_a_log, ssd_d, ssd_norm_g,
              od_w_in, od_w_out, dn_conv_w, dn_conv_b, dn_dt_bias, dn_a_log, dn_norm_g,
              rg_conv_w, rg_conv_b, rg_wa, rg_ba, rg_wx, rg_bx, rg_lam,
              peer_wq, peer_k1, peer_k2, peer_u, peer_v):
    bn, seq_len, _ = x.shape
    rows = seq_len // GRID_W
    sc = jax.nn.silu(c)
    scc = jax.nn.silu(c_ctx)
    for layer in range(DEPTH):
        j = layer // 2
        need_ctx = layer < DEPTH - 1
        mx = (sc @ ada_w[layer] + ada_b[layer]).reshape(bn, 6, 1, D_MODEL)
        mc = (scc @ ada_w[layer] + ada_b[layer]).reshape(6, D_MODEL)
        hx = modulate(rmsnorm(x, norm1_g[layer]), mx[:, 0], mx[:, 1])
        hc = modulate(rmsnorm(ctx, norm1_g[layer]), mc[0], mc[1])
        if layer % 2 == 0:
            s5_params = (s5_lam_re[j], s5_lam_im[j], s5_log_dt[j], s5_b_re[j], s5_b_im[j],
                         s5_c_re[j], s5_c_im[j], s5_d[j], s5_glu_w[j], s5_glu_b[j])
            ssd_params = (ssd_conv_w[j], ssd_conv_b[j], ssd_dt_bias[j], ssd_a_log[j],
                          ssd_d[j], ssd_norm_g[j])
            ox, oc = even_mixer(hx, hc, ev_w_in[j], ev_w_out[j], s5_params, ssd_params, need_ctx)
        else:
            dn_params = (dn_conv_w[j], dn_conv_b[j], dn_dt_bias[j], dn_a_log[j], dn_norm_g[j])
            rg_params = (rg_conv_w[j], rg_conv_b[j], rg_wa[j], rg_ba[j], rg_wx[j], rg_bx[j], rg_lam[j])
            ox, oc = odd_mixer(to_col_major(hx, rows), hc, od_w_in[j], od_w_out[j],
                               dn_params, rg_params, need_ctx)
            ox = to_row_major(ox, rows)
        x = x + mx[:, 2] * ox
        hx = modulate(rmsnorm(x, norm2_g[layer]), mx[:, 3], mx[:, 4])
        x = x + mx[:, 5] * peer(hx, peer_wq[layer], peer_k1[layer], peer_k2[layer],
                                peer_u[layer], peer_v[layer])
        if need_ctx:
            ctx = ctx + mc[2] * oc
            hc = modulate(rmsnorm(ctx, norm2_g[layer]), mc[3], mc[4])
            ctx = ctx + mc[5] * peer(hc, peer_wq[layer], peer_k1[layer], peer_k2[layer],
                                     peer_u[layer], peer_v[layer])
    return rmsnorm(x, final_g)
```

```python
import functools
import math

import jax
import jax.numpy as jnp
from jax import lax
from jax.experimental import pallas as pl
from jax.experimental.pallas import tpu as pltpu

D_MODEL = 2048
DEPTH = 2
GRID_W = 64
EPS = 1e-6
CONV_W = 4
CONV_PAD = (CONV_W // 2, CONV_W - 1 - CONV_W // 2)

S5_W = D_MODEL // 2
S5_CH = 16
S5_G = S5_W // S5_CH
S5_N = 64
SSD_P = 64
SSD_H = D_MODEL // SSD_P
SSD_DI = SSD_H * SSD_P
SSD_N = 128
SSD_G = 4
SSD_CHUNK = 128
SSD_CONV_CH = SSD_DI + 2 * SSD_G * SSD_N
EV_IN = S5_W + SSD_DI + SSD_CONV_CH + 2 * SSD_H
EV_MIX = S5_W + SSD_DI
DN_DK = 128
DN_DV = 128
DN_H = D_MODEL // DN_DK
DN_CHUNK = 64
DN_QKV = DN_H * (2 * DN_DK + DN_DV)
RG_W = D_MODEL // 2
RG_H = 8
RG_BS = RG_W // RG_H
RG_C = 8.0
OD_IN = DN_QKV + DN_H * DN_DV + 4 * DN_H + 2 * RG_W
OD_MIX = DN_H * DN_DV + RG_W
OD_CUTS = [DN_QKV,
           DN_QKV + DN_H * DN_DV,
           DN_QKV + DN_H * DN_DV + 2 * DN_H,
           DN_QKV + DN_H * DN_DV + 4 * DN_H,
           DN_QKV + DN_H * DN_DV + 4 * DN_H + RG_W]
EV_CUTS = [S5_W, S5_W + SSD_DI, S5_W + SSD_DI + SSD_CONV_CH]
PEER_H = 8
PEER_DK = 256
PEER_HALF = PEER_DK // 2
PEER_NK = 128
PEER_E = PEER_NK * PEER_NK
PEER_TOPK = 16
PEER_BLOCK = 128

F32 = jnp.float32
BF16 = jnp.bfloat16

V7X_LANES = 128
V7X_VMEM_LIMIT = 56 * 1024 * 1024


def _mm_body(x_ref, w_ref, o_ref):
    o_ref[...] = jnp.dot(x_ref[...], w_ref[...], preferred_element_type=F32).astype(o_ref.dtype)


def _pick(n, cands):
    for c in cands:
        if n % c == 0:
            return c
    raise ValueError(f"no tile for {n}")


def mm(x, w, out_dtype=F32):
    m, k = x.shape
    n = w.shape[1]
    n_pad = -n % V7X_LANES
    if n_pad:
        w = jnp.pad(w, ((0, 0), (0, n_pad)))
    np_ = n + n_pad
    tm = _pick(m, (1024, 512, 256, 128, 64, 32, 16, 8))
    tn = _pick(np_, (512, 256, 128))
    out = pl.pallas_call(
        _mm_body,
        grid=(m // tm, np_ // tn),
        in_specs=[pl.BlockSpec((tm, k), lambda i, j: (i, 0)),
                  pl.BlockSpec((k, tn), lambda i, j: (0, j))],
        out_specs=pl.BlockSpec((tm, tn), lambda i, j: (i, j)),
        out_shape=jax.ShapeDtypeStruct((m, np_), out_dtype),
        compiler_params=pltpu.CompilerParams(
            dimension_semantics=("parallel", "arbitrary"),
            vmem_limit_bytes=V7X_VMEM_LIMIT),
        name="mm",
    )(x.astype(BF16), w.astype(BF16))
    return out[:, :n] if n_pad else out


def mm3(h, w):
    b, l, d = h.shape
    return mm(h.reshape(b * l, d), w).reshape(b, l, w.shape[1])


def rmsnorm(x, g):
    xf = x.astype(F32)
    y = xf * lax.rsqrt(jnp.mean(xf * xf, axis=-1, keepdims=True) + EPS)
    return (y * g.astype(F32)).astype(x.dtype)


def l2norm(x):
    return x * lax.rsqrt(jnp.sum(x * x, axis=-1, keepdims=True) + EPS)


def modulate(h, shift, scale):
    return h * (1 + scale) + shift


def dwconv(x, w, b):
    y = lax.conv_general_dilated(x, w[:, None, :], (1,), [CONV_PAD],
                                 dimension_numbers=('NWC', 'WIO', 'NWC'),
                                 feature_group_count=x.shape[-1])
    return y + b


def flip(t):
    return jnp.flip(t, axis=1)


def to_col_major(x, rows):
    b, l, d = x.shape
    return x.reshape(b, rows, GRID_W, d).transpose(0, 2, 1, 3).reshape(b, l, d)


def to_row_major(x, rows):
    b, l, d = x.shape
    return x.reshape(b, GRID_W, rows, d).transpose(0, 2, 1, 3).reshape(b, l, d)


def linear_scan(a, b, h0):
    b = b.at[:, 0].add(a[:, 0] * h0)

    def comb(e1, e2):
        return e1[0] * e2[0], e2[0] * e1[1] + e2[1]
    return lax.associative_scan(comb, (a, b), axis=1)[1]


def _cplx_comb(e1, e2):
    ar1, ai1, br1, bi1 = e1
    ar2, ai2, br2, bi2 = e2
    return (ar2 * ar1 - ai2 * ai1, ar2 * ai1 + ai2 * ar1,
            ar2 * br1 - ai2 * bi1 + br2, ar2 * bi1 + ai2 * br1 + bi2)


def s5_direction(u, h0r, h0i, lam_re, lam_im, log_dt, b_re, b_im, c_re, c_im):
    dt = jnp.exp(log_dt)[:, None]
    mag = jnp.exp(lam_re * dt)
    ar, ai = mag * jnp.cos(lam_im * dt), mag * jnp.sin(lam_im * dt)
    den = lam_re * lam_re + lam_im * lam_im
    cr = ((ar - 1) * lam_re + ai * lam_im) / den
    ci = (ai * lam_re - (ar - 1) * lam_im) / den
    bb_re = cr[..., None] * b_re - ci[..., None] * b_im
    bb_im = cr[..., None] * b_im + ci[..., None] * b_re
    seq_len = u.shape[1]
    a_re = jnp.broadcast_to(ar, (seq_len,) + ar.shape)
    a_im = jnp.broadcast_to(ai, (seq_len,) + ai.shape)

    def one(args):
        ub, hr, hi = args
        xr = jnp.einsum('lgc,gnc->lgn', ub, bb_re)
        xi = jnp.einsum('lgc,gnc->lgn', ub, bb_im)
        xr = xr.at[0].add(ar * hr - ai * hi)
        xi = xi.at[0].add(ar * hi + ai * hr)
        _, _, sr, si = lax.associative_scan(_cplx_comb, (a_re, a_im, xr, xi), axis=0)
        y = jnp.einsum('lgn,gcn->lgc', sr, c_re) - jnp.einsum('lgn,gcn->lgc', si, c_im)
        return y, sr[-1], si[-1]
    return lax.map(one, (u, h0r, h0i))


def s5_stream(u, state, lam_re, lam_im, log_dt, b_re, b_im, c_re, c_im, d, glu_w, glu_b):
    bn, seq_len, _ = u.shape
    ug = u.reshape(bn, seq_len, S5_G, S5_CH)
    yf, fr, fi = s5_direction(ug, state[0], state[1], lam_re[0], lam_im[0], log_dt[0],
                              b_re[0], b_im[0], c_re[0], c_im[0])
    yb, br, bi = s5_direction(flip(ug), state[2], state[3], lam_re[1], lam_im[1], log_dt[1],
                              b_re[1], b_im[1], c_re[1], c_im[1])
    y = yf + flip(yb) + d.reshape(S5_G, S5_CH) * ug
    y = jax.nn.gelu(y.reshape(bn, seq_len, S5_W))
    out = y * jax.nn.sigmoid(mm3(y, glu_w) + glu_b)
    return out, (fr, fi, br, bi)


def ssd_scan(x, dt, a_neg, bm, cm, h0):
    bn, seq_len, nh, hp = x.shape
    ng, ns = bm.shape[2], bm.shape[3]
    nr = nh // ng
    q = SSD_CHUNK
    nc = seq_len // q
    xdt = (x * dt[..., None]).reshape(bn, nc, q, ng, nr, hp)
    cum = jnp.cumsum((dt * a_neg).reshape(bn, nc, q, ng, nr), axis=2)
    bm = bm.reshape(bn, nc, q, ng, ns)
    cm = cm.reshape(bn, nc, q, ng, ns)
    idx = jnp.arange(q)
    incl = idx[:, None] >= idx[None, :]
    cum_t = jnp.moveaxis(cum, 2, -1)
    seg = jnp.exp(jnp.where(incl, cum_t[..., :, None] - cum_t[..., None, :], -jnp.inf))
    cb = jnp.einsum('bclgn,bcsgn->bcgls', cm, bm)
    y_diag = jnp.einsum('bcgrls,bcsgrp->bclgrp', seg * cb[:, :, :, None], xdt)

    def step(h, inp):
        c_c, b_c, xdt_c, cum_c = inp
        y_off = jnp.einsum('blgn,bgrpn->blgrp', c_c, h) * jnp.exp(cum_c)[..., None]
        dte = jnp.exp(cum_c[:, -1:] - cum_c)
        st = jnp.einsum('blgn,blgrp->bgrpn', b_c, xdt_c * dte[..., None])
        h = h * jnp.exp(cum_c[:, -1])[..., None, None] + st
        return h, y_off
    xs = tuple(jnp.moveaxis(t, 1, 0) for t in (cm, bm, xdt, cum))
    h_last, y_off = lax.scan(step, h0.reshape(bn, ng, nr, hp, ns), xs)
    y = y_diag + jnp.moveaxis(y_off, 0, 1)
    return y.reshape(bn, seq_len, nh, hp), h_last.reshape(bn, nh, hp, ns)


def ssd_stream(z, xbc, dt_raw, state, conv_w, conv_b, dt_bias, a_log, d, norm_g):
    bn, seq_len, _ = z.shape
    xbc = jax.nn.silu(dwconv(xbc, conv_w, conv_b))
    xs, bm, cm = jnp.split(xbc, [SSD_DI, SSD_DI + SSD_G * SSD_N], axis=-1)
    xs = xs.reshape(bn, seq_len, SSD_H, SSD_P)
    bm = bm.reshape(bn, seq_len, SSD_G, SSD_N)
    cm = cm.reshape(bn, seq_len, SSD_G, SSD_N)
    dt = jax.nn.softplus(dt_raw.reshape(bn, seq_len, 2, SSD_H) + dt_bias)
    a_neg = -jnp.exp(a_log)
    yf, hf = ssd_scan(xs, dt[:, :, 0], a_neg[0], bm, cm, state[0])
    yb, hb = ssd_scan(flip(xs), flip(dt[:, :, 1]), a_neg[1], flip(bm), flip(cm), state[1])
    y = yf + flip(yb) + d[:, None] * xs
    y = y.reshape(bn, seq_len, SSD_DI) * jax.nn.silu(z)
    return rmsnorm(y, norm_g), (hf, hb)


def gated_delta_rule(q, k, v, beta, g, s0):
    bn, seq_len, nh, dk = q.shape
    dv = v.shape[-1]
    qn = DN_CHUNK
    nc = seq_len // qn

    def chunks(t):
        return jnp.swapaxes(t.reshape((bn, nc, qn) + t.shape[2:]), 2, 3)
    qc, kc, vc, bc, gc = (chunks(t) for t in (q, k, v, beta, g))
    gcum = jnp.cumsum(gc, axis=-1)
    idx = jnp.arange(qn)
    incl = idx[:, None] >= idx[None, :]
    strict = idx[:, None] > idx[None, :]
    dmat = jnp.exp(jnp.where(incl, gcum[..., :, None] - gcum[..., None, :], -jnp.inf))
    kb = kc * bc[..., None]
    m = jnp.where(strict, jnp.einsum('bnhid,bnhjd->bnhij', kb, kc) * dmat, 0.0)
    rhs = jnp.concatenate([vc * bc[..., None], kb * jnp.exp(gcum)[..., None]], axis=-1)
    sol = lax.linalg.triangular_solve(m + jnp.eye(qn, dtype=m.dtype), rhs,
                                      left_side=True, lower=True, unit_diagonal=True)
    u, w = sol[..., :dv], sol[..., dv:]
    qk = jnp.einsum('bnhid,bnhjd->bnhij', qc, kc) * dmat
    q_dec = qc * jnp.exp(gcum)[..., None]
    k_tail = kc * jnp.exp(gcum[..., -1:] - gcum)[..., None]
    tot = jnp.exp(gcum[..., -1])

    def step(s, inp):
        u_c, w_c, qk_c, qd_c, kt_c, tot_c = inp
        v_new = u_c - jnp.einsum('bhqk,bhkv->bhqv', w_c, s)
        o = jnp.einsum('bhqk,bhkv->bhqv', qd_c, s) + jnp.einsum('bhij,bhjv->bhiv', qk_c, v_new)
        s = s * tot_c[..., None, None] + jnp.einsum('bhqk,bhqv->bhkv', kt_c, v_new)
        return s, o
    xs = tuple(jnp.moveaxis(t, 1, 0) for t in (u, w, qk, q_dec, k_tail, tot))
    s_last, o = lax.scan(step, s0, xs)
    o = jnp.swapaxes(jnp.moveaxis(o, 0, 1), 2, 3).reshape(bn, seq_len, nh, dv)
    return o, s_last


def deltanet_stream(qkv, z, a_raw, b_raw, state, conv_w, conv_b, dt_bias, a_log, norm_g):
    bn, seq_len, _ = z.shape
    qkv = jax.nn.silu(dwconv(qkv, conv_w, conv_b))
    q, k, v = jnp.split(qkv, [DN_H * DN_DK, 2 * DN_H * DN_DK], axis=-1)
    q = l2norm(q.reshape(bn, seq_len, DN_H, DN_DK)) * (DN_DK ** -0.5)
    k = l2norm(k.reshape(bn, seq_len, DN_H, DN_DK))
    v = v.reshape(bn, seq_len, DN_H, DN_DV)
    beta = jax.nn.sigmoid(b_raw.reshape(bn, seq_len, 2, DN_H))
    g = -jnp.exp(a_log) * jax.nn.softplus(a_raw.reshape(bn, seq_len, 2, DN_H) + dt_bias)
    of, sf = gated_delta_rule(q, k, v, beta[:, :, 0], g[:, :, 0], state[0])
    ob, sb = gated_delta_rule(flip(q), flip(k), flip(v), flip(beta[:, :, 1]), flip(g[:, :, 1]), state[1])
    o = rmsnorm(of + flip(ob), norm_g)
    o = o.reshape(bn, seq_len, DN_H * DN_DV) * jax.nn.silu(z)
    return o, (sf, sb)


def rglru_stream(xr, gate, state, conv_w, conv_b, wa, ba, wx, bx, lam):
    bn, seq_len, _ = xr.shape
    xh = dwconv(xr, conv_w, conv_b).reshape(bn, seq_len, RG_H, RG_BS)

    def coeffs(dr):
        r = jax.nn.sigmoid(jnp.einsum('blhi,hij->blhj', xh, wa[dr]) + ba[dr])
        i = jax.nn.sigmoid(jnp.einsum('blhi,hij->blhj', xh, wx[dr]) + bx[dr])
        log_a = -RG_C * r * jax.nn.softplus(-lam[dr])
        return jnp.exp(log_a), jnp.sqrt(-jnp.expm1(2.0 * log_a)) * (i * xh)
    af, bf = coeffs(0)
    hf = linear_scan(af, bf, state[0])
    ab, bb = coeffs(1)
    hb = linear_scan(flip(ab), flip(bb), state[1])
    y = (hf + flip(hb)).reshape(bn, seq_len, RG_W) * jax.nn.gelu(gate)
    return y, (hf[:, -1], hb[:, -1])


def even_mixer(hx, hc, w_in, w_out, s5_params, ssd_params, need_ctx):
    bn = hx.shape[0]

    def stream(h, st_s5, st_ssd):
        p = mm3(h, w_in)
        u, z, xbc, dt_raw = jnp.split(p, EV_CUTS, axis=-1)
        ya, st_a = s5_stream(u, st_s5, *s5_params)
        yb, st_b = ssd_stream(z, xbc, dt_raw, st_ssd, *ssd_params)
        return jnp.concatenate([ya, yb], axis=-1), st_a, st_b
    zs5 = jnp.zeros((bn, S5_G, S5_N), F32)
    zssd = jnp.zeros((bn, SSD_H, SSD_P, SSD_N), F32)
    yc, st_a, st_b = stream(hc, (zs5, zs5, zs5, zs5), (zssd, zssd))
    yx, _, _ = stream(hx, st_a, st_b)
    return mm3(yx, w_out), (mm3(yc, w_out) if need_ctx else None)


def odd_mixer(hx, hc, w_in, w_out, dn_params, rg_params, need_ctx):
    bn = hx.shape[0]

    def stream(h, st_dn, st_rg):
        p = mm3(h, w_in)
        qkv, z, a_raw, b_raw, xr, gate = jnp.split(p, OD_CUTS, axis=-1)
        yd, st_d = deltanet_stream(qkv, z, a_raw, b_raw, st_dn, *dn_params)
        yr, st_r = rglru_stream(xr, gate, st_rg, *rg_params)
        return jnp.concatenate([yd, yr], axis=-1), st_d, st_r
    zdn = jnp.zeros((bn, DN_H, DN_DK, DN_DV), F32)
    zrg = jnp.zeros((bn, RG_H, RG_BS), F32)
    yc, st_d, st_r = stream(hc, (zdn, zdn), (zrg, zrg))
    yx, _, _ = stream(hx, st_d, st_r)
    return mm3(yx, w_out), (mm3(yc, w_out) if need_ctx else None)


def peer(h, w_q, k1, k2, u_tab, v_tab):
    bn, seq_len, dm = h.shape
    q = mm3(h, w_q).reshape(bn, seq_len, PEER_H, PEER_DK)
    s1 = jnp.einsum('blhd,hkd->blhk', q[..., :PEER_HALF], k1)
    s2 = jnp.einsum('blhd,hkd->blhk', q[..., PEER_HALF:], k2)
    t1, i1 = lax.top_k(s1, PEER_TOPK)
    t2, i2 = lax.top_k(s2, PEER_TOPK)
    cand = (t1[..., :, None] + t2[..., None, :]).reshape(bn, seq_len, PEER_H, PEER_TOPK * PEER_TOPK)
    top, ci = lax.top_k(cand, PEER_TOPK)
    idx = (jnp.take_along_axis(i1, ci // PEER_TOPK, axis=-1) * PEER_NK
           + jnp.take_along_axis(i2, ci % PEER_TOPK, axis=-1))
    gate = jax.nn.softmax(top, axis=-1)
    nb = bn * seq_len // PEER_BLOCK
    hb = h.reshape(nb, PEER_BLOCK, dm)
    ib = idx.reshape(nb, PEER_BLOCK, PEER_H * PEER_TOPK)
    gb = gate.reshape(nb, PEER_BLOCK, PEER_H * PEER_TOPK)

    def block(args):
        hh, ii, gg = args
        act = jax.nn.gelu(jnp.einsum('td,tkd->tk', hh, u_tab[ii]))
        return jnp.einsum('tk,tkd->td', gg * act, v_tab[ii])
    return lax.map(block, (hb, ib, gb)).reshape(bn, seq_len, dm)


def kernel(x, c, ctx, c_ctx, ada_w, ada_b, norm1_g, norm2_g, final_g, ev_w_in, ev_w_out, s5_lam_re, s5_lam_im, s5_log_dt, s5_b_re, s5_b_im, s5_c_re, s5_c_im, s5_d, s5_glu_w, s5_glu_b, ssd_conv_w, ssd_conv_b, ssd_dt_bias, ssd_a_log, ssd_d, ssd_norm_g, od_w_in, od_w_out, dn_conv_w, dn_conv_b, dn_dt_bias, dn_a_log, dn_norm_g, rg_conv_w, rg_conv_b, rg_wa, rg_ba, rg_wx, rg_bx, rg_lam, peer_wq, peer_k1, peer_k2, peer_u, peer_v):
    bn, seq_len, _ = x.shape
    rows = seq_len // GRID_W
    sc = jax.nn.silu(c)
    scc = jax.nn.silu(c_ctx)
    for layer in range(DEPTH):
        j = layer // 2
        need_ctx = layer < DEPTH - 1
        mx = (sc @ ada_w[layer] + ada_b[layer]).reshape(bn, 6, 1, D_MODEL)
        mc = (scc @ ada_w[layer] + ada_b[layer]).reshape(6, D_MODEL)
        hx = modulate(rmsnorm(x, norm1_g[layer]), mx[:, 0], mx[:, 1])
        hc = modulate(rmsnorm(ctx, norm1_g[layer]), mc[0], mc[1])
        if layer % 2 == 0:
            s5_params = (s5_lam_re[j], s5_lam_im[j], s5_log_dt[j], s5_b_re[j], s5_b_im[j],
                         s5_c_re[j], s5_c_im[j], s5_d[j], s5_glu_w[j], s5_glu_b[j])
            ssd_params = (ssd_conv_w[j], ssd_conv_b[j], ssd_dt_bias[j], ssd_a_log[j],
                          ssd_d[j], ssd_norm_g[j])
            ox, oc = even_mixer(hx, hc, ev_w_in[j], ev_w_out[j], s5_params, ssd_params, need_ctx)
        else:
            dn_params = (dn_conv_w[j], dn_conv_b[j], dn_dt_bias[j], dn_a_log[j], dn_norm_g[j])
            rg_params = (rg_conv_w[j], rg_conv_b[j], rg_wa[j], rg_ba[j], rg_wx[j], rg_bx[j], rg_lam[j])
            ox, oc = odd_mixer(to_col_major(hx, rows), hc, od_w_in[j], od_w_out[j],
                               dn_params, rg_params, need_ctx)
            ox = to_row_major(ox, rows)
        x = x + mx[:, 2] * ox
        hx = modulate(rmsnorm(x, norm2_g[layer]), mx[:, 3], mx[:, 4])
        x = x + mx[:, 5] * peer(hx, peer_wq[layer], peer_k1[layer], peer_k2[layer],
                                peer_u[layer], peer_v[layer])
        if need_ctx:
            ctx = ctx + mc[2] * oc
            hc = modulate(rmsnorm(ctx, norm2_g[layer]), mc[3], mc[4])
            ctx = ctx + mc[5] * peer(hc, peer_wq[layer], peer_k1[layer], peer_k2[layer],
                                     peer_u[layer], peer_v[layer])
    return rmsnorm(x, final_g)
```

```python
import functools
import math

import jax
import jax.numpy as jnp
from jax import lax
from jax.experimental import pallas as pl
from jax.experimental.pallas import tpu as pltpu

D_MODEL = 2048
DEPTH = 2
GRID_W = 64
EPS = 1e-6
CONV_W = 4
CONV_PAD = (CONV_W // 2, CONV_W - 1 - CONV_W // 2)

S5_W = D_MODEL // 2
S5_CH = 16
S5_G = S5_W // S5_CH
S5_N = 64
SSD_P = 64
SSD_H = D_MODEL // SSD_P
SSD_DI = SSD_H * SSD_P
SSD_N = 128
SSD_G = 4
SSD_CHUNK = 128
SSD_CONV_CH = SSD_DI + 2 * SSD_G * SSD_N
EV_IN = S5_W + SSD_DI + SSD_CONV_CH + 2 * SSD_H
EV_MIX = S5_W + SSD_DI
DN_DK = 128
DN_DV = 128
DN_H = D_MODEL // DN_DK
DN_CHUNK = 64
DN_QKV = DN_H * (2 * DN_DK + DN_DV)
RG_W = D_MODEL // 2
RG_H = 8
RG_BS = RG_W // RG_H
RG_C = 8.0
OD_IN = DN_QKV + DN_H * DN_DV + 4 * DN_H + 2 * RG_W
OD_MIX = DN_H * DN_DV + RG_W
OD_CUTS = [DN_QKV,
           DN_QKV + DN_H * DN_DV,
           DN_QKV + DN_H * DN_DV + 2 * DN_H,
           DN_QKV + DN_H * DN_DV + 4 * DN_H,
           DN_QKV + DN_H * DN_DV + 4 * DN_H + RG_W]
EV_CUTS = [S5_W, S5_W + SSD_DI, S5_W + SSD_DI + SSD_CONV_CH]
PEER_H = 8
PEER_DK = 256
PEER_HALF = PEER_DK // 2
PEER_NK = 128
PEER_E = PEER_NK * PEER_NK
PEER_TOPK = 16
PEER_BLOCK = 128

F32 = jnp.float32
BF16 = jnp.bfloat16

V7X_LANES = 128
V7X_VMEM_LIMIT = 56 * 1024 * 1024


def _mm_body(x_ref, w_ref, o_ref):
    o_ref[...] = jnp.dot(x_ref[...], w_ref[...], preferred_element_type=F32).astype(o_ref.dtype)


def _pick(n, cands):
    for c in cands:
        if n % c == 0:
            return c
    raise ValueError(f"no tile for {n}")


def mm(x, w, out_dtype=F32):
    m, k = x.shape
    n = w.shape[1]
    n_pad = -n % V7X_LANES
    if n_pad:
        w = jnp.pad(w, ((0, 0), (0, n_pad)))
    np_ = n + n_pad
    tm = _pick(m, (1024, 512, 256, 128, 64, 32, 16, 8))
    tn = _pick(np_, (512, 256, 128))
    out = pl.pallas_call(
        _mm_body,
        grid=(m // tm, np_ // tn),
        in_specs=[pl.BlockSpec((tm, k), lambda i, j: (i, 0)),
                  pl.BlockSpec((k, tn), lambda i, j: (0, j))],
        out_specs=pl.BlockSpec((tm, tn), lambda i, j: (i, j)),
        out_shape=jax.ShapeDtypeStruct((m, np_), out_dtype),
        compiler_params=pltpu.CompilerParams(
            dimension_semantics=("parallel", "arbitrary"),
            vmem_limit_bytes=V7X_VMEM_LIMIT),
        name="mm",
    )(x.astype(BF16), w.astype(BF16))
    return out[:, :n] if n_pad else out


def mm3(h, w):
    b, l, d = h.shape
    return mm(h.reshape(b * l, d), w).reshape(b, l, w.shape[1])


def rmsnorm(x, g):
    xf = x.astype(F32)
    y = xf * lax.rsqrt(jnp.mean(xf * xf, axis=-1, keepdims=True) + EPS)
    return (y * g.astype(F32)).astype(x.dtype)


def l2norm(x):
    return x * lax.rsqrt(jnp.sum(x * x, axis=-1, keepdims=True) + EPS)


def modulate(h, shift, scale):
    return h * (1 + scale) + shift


def dwconv(x, w, b):
    y = lax.conv_general_dilated(x, w[:, None, :], (1,), [CONV_PAD],
                                 dimension_numbers=('NWC', 'WIO', 'NWC'),
                                 feature_group_count=x.shape[-1])
    return y + b


def flip(t):
    return jnp.flip(t, axis=1)


def to_col_major(x, rows):
    b, l, d = x.shape
    return x.reshape(b, rows, GRID_W, d).transpose(0, 2, 1, 3).reshape(b, l, d)


def to_row_major(x, rows):
    b, l, d = x.shape
    return x.reshape(b, GRID_W, rows, d).transpose(0, 2, 1, 3).reshape(b, l, d)


def linear_scan(a, b, h0):
    b = b.at[:, 0].add(a[:, 0] * h0)

    def comb(e1, e2):
        return e1[0] * e2[0], e2[0] * e1[1] + e2[1]
    return lax.associative_scan(comb, (a, b), axis=1)[1]


S5_LANES = S5_G * S5_N
S5_TILE = 8
S5_SHIFTS = (1, 2, 4)
S5_BG = 8
S5_CG = 16
S5_CHUNK = 1024


def _s5_scan_body(u_ref, h0r_ref, h0i_ref, bdr_ref, bdi_ref, cdr_ref, cdi_ref, coef_ref,
                  y_ref, hfr_ref, hfi_ref, xr_s, xi_s, cr_s, ci_s, *, rev, tt):
    @pl.when(pl.program_id(1) == 0)
    def _():
        cr_s[...] = h0r_ref[0]
        ci_s[...] = h0i_ref[0]

    ub = u_ref[0].astype(BF16)
    kin = S5_BG * S5_CH
    kout = S5_BG * S5_N
    for j in range(S5_G // S5_BG):
        uj = ub[:, kin * j:kin * (j + 1)]
        xr_s[:, kout * j:kout * (j + 1)] = jnp.dot(uj, bdr_ref[j], preferred_element_type=F32)
        xi_s[:, kout * j:kout * (j + 1)] = jnp.dot(uj, bdi_ref[j], preferred_element_type=F32)

    nt = tt // S5_TILE
    carry_row = 0 if rev else S5_TILE - 1

    def tile(kk, carry):
        k = (nt - 1 - kk) if rev else kk
        r0 = pl.multiple_of(k * S5_TILE, S5_TILE)
        for c in range(S5_LANES // S5_CHUNK):
            sl = slice(S5_CHUNK * c, S5_CHUNK * (c + 1))
            xr = xr_s[pl.ds(r0, S5_TILE), sl]
            xi = xi_s[pl.ds(r0, S5_TILE), sl]
            for idx, sh in enumerate(S5_SHIFTS):
                pr = coef_ref[2 * idx, :, sl]
                pim = coef_ref[2 * idx + 1, :, sl]
                rs = (S5_TILE - sh) if rev else sh
                sr = pltpu.roll(xr, rs, 0)
                si = pltpu.roll(xi, rs, 0)
                xr, xi = xr + (pr * sr - pim * si), xi + (pr * si + pim * sr)
            pr = coef_ref[2 * len(S5_SHIFTS), :, sl]
            pim = coef_ref[2 * len(S5_SHIFTS) + 1, :, sl]
            hr = cr_s[:, sl]
            hi = ci_s[:, sl]
            xr, xi = xr + (pr * hr - pim * hi), xi + (pr * hi + pim * hr)
            xr_s[pl.ds(r0, S5_TILE), sl] = xr
            xi_s[pl.ds(r0, S5_TILE), sl] = xi
            cr_s[:, sl] = xr[carry_row:carry_row + 1, :]
            ci_s[:, sl] = xi[carry_row:carry_row + 1, :]
        return carry

    lax.fori_loop(0, nt, tile, 0)

    kc = S5_CG * S5_N
    nc = S5_CG * S5_CH
    for j in range(S5_G // S5_CG):
        sr = xr_s[:, kc * j:kc * (j + 1)].astype(BF16)
        si = xi_s[:, kc * j:kc * (j + 1)].astype(BF16)
        y_ref[0, :, nc * j:nc * (j + 1)] = (
            jnp.dot(sr, cdr_ref[j], preferred_element_type=F32)
            - jnp.dot(si, cdi_ref[j], preferred_element_type=F32))
    hfr_ref[0] = cr_s[...]
    hfi_ref[0] = ci_s[...]


def _s5_direction_params(lam_re, lam_im, log_dt, b_re, b_im, c_re, c_im, rev):
    dt = jnp.exp(log_dt)[:, None]
    mag = jnp.exp(lam_re * dt)
    ar, ai = mag * jnp.cos(lam_im * dt), mag * jnp.sin(lam_im * dt)
    den = lam_re * lam_re + lam_im * lam_im
    cr = ((ar - 1) * lam_re + ai * lam_im) / den
    ci = (ai * lam_re - (ar - 1) * lam_im) / den
    bb_re = cr[..., None] * b_re - ci[..., None] * b_im
    bb_im = cr[..., None] * b_im + ci[..., None] * b_re

    def bdiag_in(bb):
        t = bb.reshape(S5_G // S5_BG, S5_BG, S5_N, S5_CH)
        eye = jnp.eye(S5_BG, dtype=F32)
        return jnp.einsum('jgnc,gh->jgchn', t, eye).reshape(
            S5_G // S5_BG, S5_BG * S5_CH, S5_BG * S5_N).astype(BF16)

    def bdiag_out(cc):
        t = cc.reshape(S5_G // S5_CG, S5_CG, S5_CH, S5_N)
        eye = jnp.eye(S5_CG, dtype=F32)
        return jnp.einsum('jgcn,gh->jgnhc', t, eye).reshape(
            S5_G // S5_CG, S5_CG * S5_N, S5_CG * S5_CH).astype(BF16)

    a_r, a_i = ar.reshape(-1), ai.reshape(-1)
    pw = [(a_r, a_i)]
    for _ in range(S5_TILE - 1):
        pr, pim = pw[-1]
        pw.append((pr * a_r - pim * a_i, pr * a_i + pim * a_r))
    rows = jnp.arange(S5_TILE)[:, None]
    coef = []
    for sh in S5_SHIFTS:
        valid = (rows <= S5_TILE - 1 - sh) if rev else (rows >= sh)
        coef.append(jnp.where(valid, pw[sh - 1][0][None, :], 0.0))
        coef.append(jnp.where(valid, pw[sh - 1][1][None, :], 0.0))
    order = list(range(S5_TILE - 1, -1, -1)) if rev else list(range(S5_TILE))
    coef.append(jnp.stack([pw[k][0] for k in order]))
    coef.append(jnp.stack([pw[k][1] for k in order]))
    return (bdiag_in(bb_re), bdiag_in(bb_im), bdiag_out(c_re), bdiag_out(c_im),
            jnp.stack(coef).astype(F32))


def s5_scan(u, h0r, h0i, dparams, rev):
    bn, seq_len, _ = u.shape
    bdr, bdi, cdr, cdi, coef = dparams
    tt = min(256, seq_len)
    nblk = seq_len // tt
    tmap = (lambda b, i: (b, nblk - 1 - i, 0)) if rev else (lambda b, i: (b, i, 0))
    const3 = lambda b, i: (0, 0, 0)
    y, hfr, hfi = pl.pallas_call(
        functools.partial(_s5_scan_body, rev=rev, tt=tt),
        grid=(bn, nblk),
        in_specs=[pl.BlockSpec((1, tt, S5_W), tmap),
                  pl.BlockSpec((1, 1, S5_LANES), lambda b, i: (b, 0, 0)),
                  pl.BlockSpec((1, 1, S5_LANES), lambda b, i: (b, 0, 0)),
                  pl.BlockSpec(bdr.shape, const3),
                  pl.BlockSpec(bdi.shape, const3),
                  pl.BlockSpec(cdr.shape, const3),
                  pl.BlockSpec(cdi.shape, const3),
                  pl.BlockSpec(coef.shape, const3)],
        out_specs=[pl.BlockSpec((1, tt, S5_W), tmap),
                   pl.BlockSpec((1, 1, S5_LANES), lambda b, i: (b, 0, 0)),
                   pl.BlockSpec((1, 1, S5_LANES), lambda b, i: (b, 0, 0))],
        out_shape=[jax.ShapeDtypeStruct((bn, seq_len, S5_W), F32),
                   jax.ShapeDtypeStruct((bn, 1, S5_LANES), F32),
                   jax.ShapeDtypeStruct((bn, 1, S5_LANES), F32)],
        scratch_shapes=[pltpu.VMEM((tt, S5_LANES), F32), pltpu.VMEM((tt, S5_LANES), F32),
                        pltpu.VMEM((1, S5_LANES), F32), pltpu.VMEM((1, S5_LANES), F32)],
        compiler_params=pltpu.CompilerParams(
            dimension_semantics=("parallel", "arbitrary"),
            vmem_limit_bytes=V7X_VMEM_LIMIT),
        name="s5_scan_bwd" if rev else "s5_scan_fwd",
    )(u, h0r.reshape(bn, 1, S5_LANES), h0i.reshape(bn, 1, S5_LANES), bdr, bdi, cdr, cdi, coef)
    return y, hfr.reshape(bn, S5_LANES), hfi.reshape(bn, S5_LANES)


def _s5_glu_body(yf_ref, yb_ref, u_ref, d_ref, w_ref, b_ref, o_ref):
    y = yf_ref[...] + yb_ref[...] + d_ref[...] * u_ref[...]
    g = jax.nn.gelu(y)
    z = jnp.dot(g.astype(BF16), w_ref[...], preferred_element_type=F32) + b_ref[...]
    o_ref[...] = (g * jax.nn.sigmoid(z)).astype(o_ref.dtype)


def s5_glu(yf, yb, u, d, glu_w, glu_b):
    m = yf.shape[0]
    tm = _pick(m, (512, 256, 128, 64, 32, 16, 8))
    row = pl.BlockSpec((tm, S5_W), lambda i: (i, 0))
    vec = pl.BlockSpec((1, S5_W), lambda i: (0, 0))
    return pl.pallas_call(
        _s5_glu_body,
        grid=(m // tm,),
        in_specs=[row, row, row, vec, pl.BlockSpec((S5_W, S5_W), lambda i: (0, 0)), vec],
        out_specs=row,
        out_shape=jax.ShapeDtypeStruct((m, S5_W), F32),
        compiler_params=pltpu.CompilerParams(
            dimension_semantics=("parallel",), vmem_limit_bytes=V7X_VMEM_LIMIT),
        name="s5_glu",
    )(yf, yb, u, d.reshape(1, S5_W), glu_w.astype(BF16), glu_b.reshape(1, S5_W))


def s5_stream(u, state, lam_re, lam_im, log_dt, b_re, b_im, c_re, c_im, d, glu_w, glu_b):
    bn, seq_len, _ = u.shape
    pf = _s5_direction_params(lam_re[0], lam_im[0], log_dt[0], b_re[0], b_im[0], c_re[0], c_im[0], False)
    pb = _s5_direction_params(lam_re[1], lam_im[1], log_dt[1], b_re[1], b_im[1], c_re[1], c_im[1], True)
    yf, fr, fi = s5_scan(u, state[0], state[1], pf, False)
    yb, br, bi = s5_scan(u, state[2], state[3], pb, True)
    m = bn * seq_len
    out = s5_glu(yf.reshape(m, S5_W), yb.reshape(m, S5_W), u.reshape(m, S5_W), d, glu_w, glu_b)
    return out.reshape(bn, seq_len, S5_W), (fr, fi, br, bi)


def ssd_scan(x, dt, a_neg, bm, cm, h0):
    bn, seq_len, nh, hp = x.shape
    ng, ns = bm.shape[2], bm.shape[3]
    nr = nh // ng
    q = SSD_CHUNK
    nc = seq_len // q
    xdt = (x * dt[..., None]).reshape(bn, nc, q, ng, nr, hp)
    cum = jnp.cumsum((dt * a_neg).reshape(bn, nc, q, ng, nr), axis=2)
    bm = bm.reshape(bn, nc, q, ng, ns)
    cm = cm.reshape(bn, nc, q, ng, ns)
    idx = jnp.arange(q)
    incl = idx[:, None] >= idx[None, :]
    cum_t = jnp.moveaxis(cum, 2, -1)
    seg = jnp.exp(jnp.where(incl, cum_t[..., :, None] - cum_t[..., None, :], -jnp.inf))
    cb = jnp.einsum('bclgn,bcsgn->bcgls', cm, bm)
    y_diag = jnp.einsum('bcgrls,bcsgrp->bclgrp', seg * cb[:, :, :, None], xdt)

    def step(h, inp):
        c_c, b_c, xdt_c, cum_c = inp
        y_off = jnp.einsum('blgn,bgrpn->blgrp', c_c, h) * jnp.exp(cum_c)[..., None]
        dte = jnp.exp(cum_c[:, -1:] - cum_c)
        st = jnp.einsum('blgn,blgrp->bgrpn', b_c, xdt_c * dte[..., None])
        h = h * jnp.exp(cum_c[:, -1])[..., None, None] + st
        return h, y_off
    xs = tuple(jnp.moveaxis(t, 1, 0) for t in (cm, bm, xdt, cum))
    h_last, y_off = lax.scan(step, h0.reshape(bn, ng, nr, hp, ns), xs)
    y = y_diag + jnp.moveaxis(y_off, 0, 1)
    return y.reshape(bn, seq_len, nh, hp), h_last.reshape(bn, nh, hp, ns)


def ssd_stream(z, xbc, dt_raw, state, conv_w, conv_b, dt_bias, a_log, d, norm_g):
    bn, seq_len, _ = z.shape
    xbc = jax.nn.silu(dwconv(xbc, conv_w, conv_b))
    xs, bm, cm = jnp.split(xbc, [SSD_DI, SSD_DI + SSD_G * SSD_N], axis=-1)
    xs = xs.reshape(bn, seq_len, SSD_H, SSD_P)
    bm = bm.reshape(bn, seq_len, SSD_G, SSD_N)
    cm = cm.reshape(bn, seq_len, SSD_G, SSD_N)
    dt = jax.nn.softplus(dt_raw.reshape(bn, seq_len, 2, SSD_H) + dt_bias)
    a_neg = -jnp.exp(a_log)
    yf, hf = ssd_scan(xs, dt[:, :, 0], a_neg[0], bm, cm, state[0])
    yb, hb = ssd_scan(flip(xs), flip(dt[:, :, 1]), a_neg[1], flip(bm), flip(cm), state[1])
    y = yf + flip(yb) + d[:, None] * xs
    y = y.reshape(bn, seq_len, SSD_DI) * jax.nn.silu(z)
    return rmsnorm(y, norm_g), (hf, hb)


def gated_delta_rule(q, k, v, beta, g, s0):
    bn, seq_len, nh, dk = q.shape
    dv = v.shape[-1]
    qn = DN_CHUNK
    nc = seq_len // qn

    def chunks(t):
        return jnp.swapaxes(t.reshape((bn, nc, qn) + t.shape[2:]), 2, 3)
    qc, kc, vc, bc, gc = (chunks(t) for t in (q, k, v, beta, g))
    gcum = jnp.cumsum(gc, axis=-1)
    idx = jnp.arange(qn)
    incl = idx[:, None] >= idx[None, :]
    strict = idx[:, None] > idx[None, :]
    dmat = jnp.exp(jnp.where(incl, gcum[..., :, None] - gcum[..., None, :], -jnp.inf))
    kb = kc * bc[..., None]
    m = jnp.where(strict, jnp.einsum('bnhid,bnhjd->bnhij', kb, kc) * dmat, 0.0)
    rhs = jnp.concatenate([vc * bc[..., None], kb * jnp.exp(gcum)[..., None]], axis=-1)
    sol = lax.linalg.triangular_solve(m + jnp.eye(qn, dtype=m.dtype), rhs,
                                      left_side=True, lower=True, unit_diagonal=True)
    u, w = sol[..., :dv], sol[..., dv:]
    qk = jnp.einsum('bnhid,bnhjd->bnhij', qc, kc) * dmat
    q_dec = qc * jnp.exp(gcum)[..., None]
    k_tail = kc * jnp.exp(gcum[..., -1:] - gcum)[..., None]
    tot = jnp.exp(gcum[..., -1])

    def step(s, inp):
        u_c, w_c, qk_c, qd_c, kt_c, tot_c = inp
        v_new = u_c - jnp.einsum('bhqk,bhkv->bhqv', w_c, s)
        o = jnp.einsum('bhqk,bhkv->bhqv', qd_c, s) + jnp.einsum('bhij,bhjv->bhiv', qk_c, v_new)
        s = s * tot_c[..., None, None] + jnp.einsum('bhqk,bhqv->bhkv', kt_c, v_new)
        return s, o
    xs = tuple(jnp.moveaxis(t, 1, 0) for t in (u, w, qk, q_dec, k_tail, tot))
    s_last, o = lax.scan(step, s0, xs)
    o = jnp.swapaxes(jnp.moveaxis(o, 0, 1), 2, 3).reshape(bn, seq_len, nh, dv)
    return o, s_last


def deltanet_stream(qkv, z, a_raw, b_raw, state, conv_w, conv_b, dt_bias, a_log, norm_g):
    bn, seq_len, _ = z.shape
    qkv = jax.nn.silu(dwconv(qkv, conv_w, conv_b))
    q, k, v = jnp.split(qkv, [DN_H * DN_DK, 2 * DN_H * DN_DK], axis=-1)
    q = l2norm(q.reshape(bn, seq_len, DN_H, DN_DK)) * (DN_DK ** -0.5)
    k = l2norm(k.reshape(bn, seq_len, DN_H, DN_DK))
    v = v.reshape(bn, seq_len, DN_H, DN_DV)
    beta = jax.nn.sigmoid(b_raw.reshape(bn, seq_len, 2, DN_H))
    g = -jnp.exp(a_log) * jax.nn.softplus(a_raw.reshape(bn, seq_len, 2, DN_H) + dt_bias)
    of, sf = gated_delta_rule(q, k, v, beta[:, :, 0], g[:, :, 0], state[0])
    ob, sb = gated_delta_rule(flip(q), flip(k), flip(v), flip(beta[:, :, 1]), flip(g[:, :, 1]), state[1])
    o = rmsnorm(of + flip(ob), norm_g)
    o = o.reshape(bn, seq_len, DN_H * DN_DV) * jax.nn.silu(z)
    return o, (sf, sb)


def rglru_stream(xr, gate, state, conv_w, conv_b, wa, ba, wx, bx, lam):
    bn, seq_len, _ = xr.shape
    xh = dwconv(xr, conv_w, conv_b).reshape(bn, seq_len, RG_H, RG_BS)

    def coeffs(dr):
        r = jax.nn.sigmoid(jnp.einsum('blhi,hij->blhj', xh, wa[dr]) + ba[dr])
        i = jax.nn.sigmoid(jnp.einsum('blhi,hij->blhj', xh, wx[dr]) + bx[dr])
        log_a = -RG_C * r * jax.nn.softplus(-lam[dr])
        return jnp.exp(log_a), jnp.sqrt(-jnp.expm1(2.0 * log_a)) * (i * xh)
    af, bf = coeffs(0)
    hf = linear_scan(af, bf, state[0])
    ab, bb = coeffs(1)
    hb = linear_scan(flip(ab), flip(bb), state[1])
    y = (hf + flip(hb)).reshape(bn, seq_len, RG_W) * jax.nn.gelu(gate)
    return y, (hf[:, -1], hb[:, -1])


def even_mixer(hx, hc, w_in, w_out, s5_params, ssd_params, need_ctx):
    bn = hx.shape[0]

    def stream(h, st_s5, st_ssd):
        p = mm3(h, w_in)
        u, z, xbc, dt_raw = jnp.split(p, EV_CUTS, axis=-1)
        ya, st_a = s5_stream(u, st_s5, *s5_params)
        yb, st_b = ssd_stream(z, xbc, dt_raw, st_ssd, *ssd_params)
        return jnp.concatenate([ya, yb], axis=-1), st_a, st_b
    zs5 = jnp.zeros((bn, S5_LANES), F32)
    zssd = jnp.zeros((bn, SSD_H, SSD_P, SSD_N), F32)
    yc, st_a, st_b = stream(hc, (zs5, zs5, zs5, zs5), (zssd, zssd))
    yx, _, _ = stream(hx, st_a, st_b)
    return mm3(yx, w_out), (mm3(yc, w_out) if need_ctx else None)


def odd_mixer(hx, hc, w_in, w_out, dn_params, rg_params, need_ctx):
    bn = hx.shape[0]

    def stream(h, st_dn, st_rg):
        p = mm3(h, w_in)
        qkv, z, a_raw, b_raw, xr, gate = jnp.split(p, OD_CUTS, axis=-1)
        yd, st_d = deltanet_stream(qkv, z, a_raw, b_raw, st_dn, *dn_params)
        yr, st_r = rglru_stream(xr, gate, st_rg, *rg_params)
        return jnp.concatenate([yd, yr], axis=-1), st_d, st_r
    zdn = jnp.zeros((bn, DN_H, DN_DK, DN_DV), F32)
    zrg = jnp.zeros((bn, RG_H, RG_BS), F32)
    yc, st_d, st_r = stream(hc, (zdn, zdn), (zrg, zrg))
    yx, _, _ = stream(hx, st_d, st_r)
    return mm3(yx, w_out), (mm3(yc, w_out) if need_ctx else None)


def peer(h, w_q, k1, k2, u_tab, v_tab):
    bn, seq_len, dm = h.shape
    q = mm3(h, w_q).reshape(bn, seq_len, PEER_H, PEER_DK)
    s1 = jnp.einsum('blhd,hkd->blhk', q[..., :PEER_HALF], k1)
    s2 = jnp.einsum('blhd,hkd->blhk', q[..., PEER_HALF:], k2)
    t1, i1 = lax.top_k(s1, PEER_TOPK)
    t2, i2 = lax.top_k(s2, PEER_TOPK)
    cand = (t1[..., :, None] + t2[..., None, :]).reshape(bn, seq_len, PEER_H, PEER_TOPK * PEER_TOPK)
    top, ci = lax.top_k(cand, PEER_TOPK)
    idx = (jnp.take_along_axis(i1, ci // PEER_TOPK, axis=-1) * PEER_NK
           + jnp.take_along_axis(i2, ci % PEER_TOPK, axis=-1))
    gate = jax.nn.softmax(top, axis=-1)
    nb = bn * seq_len // PEER_BLOCK
    hb = h.reshape(nb, PEER_BLOCK, dm)
    ib = idx.reshape(nb, PEER_BLOCK, PEER_H * PEER_TOPK)
    gb = gate.reshape(nb, PEER_BLOCK, PEER_H * PEER_TOPK)

    def block(args):
        hh, ii, gg = args
        act = jax.nn.gelu(jnp.einsum('td,tkd->tk', hh, u_tab[ii]))
        return jnp.einsum('tk,tkd->td', gg * act, v_tab[ii])
    return lax.map(block, (hb, ib, gb)).reshape(bn, seq_len, dm)


def kernel(x, c, ctx, c_ctx, ada_w, ada_b, norm1_g, norm2_g, final_g, ev_w_in, ev_w_out, s5_lam_re, s5_lam_im, s5_log_dt, s5_b_re, s5_b_im, s5_c_re, s5_c_im, s5_d, s5_glu_w, s5_glu_b, ssd_conv_w, ssd_conv_b, ssd_dt_bias, ssd_a_log, ssd_d, ssd_norm_g, od_w_in, od_w_out, dn_conv_w, dn_conv_b, dn_dt_bias, dn_a_log, dn_norm_g, rg_conv_w, rg_conv_b, rg_wa, rg_ba, rg_wx, rg_bx, rg_lam, peer_wq, peer_k1, peer_k2, peer_u, peer_v):
    bn, seq_len, _ = x.shape
    rows = seq_len // GRID_W
    sc = jax.nn.silu(c)
    scc = jax.nn.silu(c_ctx)
    for layer in range(DEPTH):
        j = layer // 2
        need_ctx = layer < DEPTH - 1
        mx = (sc @ ada_w[layer] + ada_b[layer]).reshape(bn, 6, 1, D_MODEL)
        mc = (scc @ ada_w[layer] + ada_b[layer]).reshape(6, D_MODEL)
        hx = modulate(rmsnorm(x, norm1_g[layer]), mx[:, 0], mx[:, 1])
        hc = modulate(rmsnorm(ctx, norm1_g[layer]), mc[0], mc[1])
        if layer % 2 == 0:
            s5_params = (s5_lam_re[j], s5_lam_im[j], s5_log_dt[j], s5_b_re[j], s5_b_im[j],
                         s5_c_re[j], s5_c_im[j], s5_d[j], s5_glu_w[j], s5_glu_b[j])
            ssd_params = (ssd_conv_w[j], ssd_conv_b[j], ssd_dt_bias[j], ssd_a_log[j],
                          ssd_d[j], ssd_norm_g[j])
            ox, oc = even_mixer(hx, hc, ev_w_in[j], ev_w_out[j], s5_params, ssd_params, need_ctx)
        else:
            dn_params = (dn_conv_w[j], dn_conv_b[j], dn_dt_bias[j], dn_a_log[j], dn_norm_g[j])
            rg_params = (rg_conv_w[j], rg_conv_b[j], rg_wa[j], rg_ba[j], rg_wx[j], rg_bx[j], rg_lam[j])
            ox, oc = odd_mixer(to_col_major(hx, rows), hc, od_w_in[j], od_w_out[j],
                               dn_params, rg_params, need_ctx)
            ox = to_row_major(ox, rows)
        x = x + mx[:, 2] * ox
        hx = modulate(rmsnorm(x, norm2_g[layer]), mx[:, 3], mx[:, 4])
        x = x + mx[:, 5] * peer(hx, peer_wq[layer], peer_k1[layer], peer_k2[layer],
                                peer_u[layer], peer_v[layer])
        if need_ctx:
            ctx = ctx + mc[2] * oc
            hc = modulate(rmsnorm(ctx, norm2_g[layer]), mc[3], mc[4])
            ctx = ctx + mc[5] * peer(hc, peer_wq[layer], peer_k1[layer], peer_k2[layer],
                                     peer_u[layer], peer_v[layer])
    return rmsnorm(x, final_g)
```

```python
import functools
import math

import jax
import jax.numpy as jnp
from jax import lax
from jax.experimental import pallas as pl
from jax.experimental.pallas import tpu as pltpu

D_MODEL = 2048
DEPTH = 2
GRID_W = 64
EPS = 1e-6
CONV_W = 4
CONV_PAD = (CONV_W // 2, CONV_W - 1 - CONV_W // 2)

S5_W = D_MODEL // 2
S5_CH = 16
S5_G = S5_W // S5_CH
S5_N = 64
SSD_P = 64
SSD_H = D_MODEL // SSD_P
SSD_DI = SSD_H * SSD_P
SSD_N = 128
SSD_G = 4
SSD_CHUNK = 128
SSD_CONV_CH = SSD_DI + 2 * SSD_G * SSD_N
EV_IN = S5_W + SSD_DI + SSD_CONV_CH + 2 * SSD_H
EV_MIX = S5_W + SSD_DI
DN_DK = 128
DN_DV = 128
DN_H = D_MODEL // DN_DK
DN_CHUNK = 64
DN_QKV = DN_H * (2 * DN_DK + DN_DV)
RG_W = D_MODEL // 2
RG_H = 8
RG_BS = RG_W // RG_H
RG_C = 8.0
OD_IN = DN_QKV + DN_H * DN_DV + 4 * DN_H + 2 * RG_W
OD_MIX = DN_H * DN_DV + RG_W
OD_CUTS = [DN_QKV,
           DN_QKV + DN_H * DN_DV,
           DN_QKV + DN_H * DN_DV + 2 * DN_H,
           DN_QKV + DN_H * DN_DV + 4 * DN_H,
           DN_QKV + DN_H * DN_DV + 4 * DN_H + RG_W]
EV_CUTS = [S5_W, S5_W + SSD_DI, S5_W + SSD_DI + SSD_CONV_CH]
PEER_H = 8
PEER_DK = 256
PEER_HALF = PEER_DK // 2
PEER_NK = 128
PEER_E = PEER_NK * PEER_NK
PEER_TOPK = 16
PEER_BLOCK = 128

F32 = jnp.float32
BF16 = jnp.bfloat16

V7X_LANES = 128
V7X_SUBLANES = 8
V7X_VMEM_LIMIT = 56 * 1024 * 1024


def _mm_body(x_ref, w_ref, o_ref):
    o_ref[...] = jnp.dot(x_ref[...], w_ref[...], preferred_element_type=F32).astype(o_ref.dtype)


def _pick(n, cands):
    for c in cands:
        if n % c == 0:
            return c
    raise ValueError(f"no tile for {n}")


def mm(x, w, out_dtype=F32):
    m, k = x.shape
    n = w.shape[1]
    n_pad = -n % V7X_LANES
    if n_pad:
        w = jnp.pad(w, ((0, 0), (0, n_pad)))
    np_ = n + n_pad
    tm = _pick(m, (1024, 512, 256, 128, 64, 32, 16, 8))
    tn = _pick(np_, (512, 256, 128))
    out = pl.pallas_call(
        _mm_body,
        grid=(m // tm, np_ // tn),
        in_specs=[pl.BlockSpec((tm, k), lambda i, j: (i, 0)),
                  pl.BlockSpec((k, tn), lambda i, j: (0, j))],
        out_specs=pl.BlockSpec((tm, tn), lambda i, j: (i, j)),
        out_shape=jax.ShapeDtypeStruct((m, np_), out_dtype),
        compiler_params=pltpu.CompilerParams(
            dimension_semantics=("parallel", "arbitrary"),
            vmem_limit_bytes=V7X_VMEM_LIMIT),
        name="mm",
    )(x.astype(BF16), w.astype(BF16))
    return out[:, :n] if n_pad else out


def mm3(h, w):
    b, l, d = h.shape
    return mm(h.reshape(b * l, d), w).reshape(b, l, w.shape[1])


def rmsnorm(x, g):
    xf = x.astype(F32)
    y = xf * lax.rsqrt(jnp.mean(xf * xf, axis=-1, keepdims=True) + EPS)
    return (y * g.astype(F32)).astype(x.dtype)


def l2norm(x):
    return x * lax.rsqrt(jnp.sum(x * x, axis=-1, keepdims=True) + EPS)


def modulate(h, shift, scale):
    return h * (1 + scale) + shift


def dwconv(x, w, b):
    y = lax.conv_general_dilated(x, w[:, None, :], (1,), [CONV_PAD],
                                 dimension_numbers=('NWC', 'WIO', 'NWC'),
                                 feature_group_count=x.shape[-1])
    return y + b


def flip(t):
    return jnp.flip(t, axis=1)


def to_col_major(x, rows):
    b, l, d = x.shape
    return x.reshape(b, rows, GRID_W, d).transpose(0, 2, 1, 3).reshape(b, l, d)


def to_row_major(x, rows):
    b, l, d = x.shape
    return x.reshape(b, GRID_W, rows, d).transpose(0, 2, 1, 3).reshape(b, l, d)


def linear_scan(a, b, h0):
    b = b.at[:, 0].add(a[:, 0] * h0)

    def comb(e1, e2):
        return e1[0] * e2[0], e2[0] * e1[1] + e2[1]
    return lax.associative_scan(comb, (a, b), axis=1)[1]


S5_LANES = S5_G * S5_N
S5_TILE = 8
S5_SHIFTS = (1, 2, 4)
S5_BG = 8
S5_CG = 16
S5_CHUNK = 1024


def _s5_scan_body(u_ref, h0r_ref, h0i_ref, bdr_ref, bdi_ref, cdr_ref, cdi_ref, coef_ref,
                  y_ref, hfr_ref, hfi_ref, xr_s, xi_s, cr_s, ci_s, *, rev, tt):
    @pl.when(pl.program_id(1) == 0)
    def _():
        cr_s[...] = h0r_ref[0]
        ci_s[...] = h0i_ref[0]

    ub = u_ref[0].astype(BF16)
    kin = S5_BG * S5_CH
    kout = S5_BG * S5_N
    for j in range(S5_G // S5_BG):
        uj = ub[:, kin * j:kin * (j + 1)]
        xr_s[:, kout * j:kout * (j + 1)] = jnp.dot(uj, bdr_ref[j], preferred_element_type=F32)
        xi_s[:, kout * j:kout * (j + 1)] = jnp.dot(uj, bdi_ref[j], preferred_element_type=F32)

    nt = tt // S5_TILE
    carry_row = 0 if rev else S5_TILE - 1

    def tile(kk, carry):
        k = (nt - 1 - kk) if rev else kk
        r0 = pl.multiple_of(k * S5_TILE, S5_TILE)
        for c in range(S5_LANES // S5_CHUNK):
            sl = slice(S5_CHUNK * c, S5_CHUNK * (c + 1))
            xr = xr_s[pl.ds(r0, S5_TILE), sl]
            xi = xi_s[pl.ds(r0, S5_TILE), sl]
            for idx, sh in enumerate(S5_SHIFTS):
                pr = coef_ref[2 * idx, :, sl]
                pim = coef_ref[2 * idx + 1, :, sl]
                rs = (S5_TILE - sh) if rev else sh
                sr = pltpu.roll(xr, rs, 0)
                si = pltpu.roll(xi, rs, 0)
                xr, xi = xr + (pr * sr - pim * si), xi + (pr * si + pim * sr)
            pr = coef_ref[2 * len(S5_SHIFTS), :, sl]
            pim = coef_ref[2 * len(S5_SHIFTS) + 1, :, sl]
            hr = cr_s[:, sl]
            hi = ci_s[:, sl]
            xr, xi = xr + (pr * hr - pim * hi), xi + (pr * hi + pim * hr)
            xr_s[pl.ds(r0, S5_TILE), sl] = xr
            xi_s[pl.ds(r0, S5_TILE), sl] = xi
            cr_s[:, sl] = xr[carry_row:carry_row + 1, :]
            ci_s[:, sl] = xi[carry_row:carry_row + 1, :]
        return carry

    lax.fori_loop(0, nt, tile, 0)

    kc = S5_CG * S5_N
    nc = S5_CG * S5_CH
    for j in range(S5_G // S5_CG):
        sr = xr_s[:, kc * j:kc * (j + 1)].astype(BF16)
        si = xi_s[:, kc * j:kc * (j + 1)].astype(BF16)
        y_ref[0, :, nc * j:nc * (j + 1)] = (
            jnp.dot(sr, cdr_ref[j], preferred_element_type=F32)
            - jnp.dot(si, cdi_ref[j], preferred_element_type=F32))
    hfr_ref[0] = cr_s[...]
    hfi_ref[0] = ci_s[...]


def _s5_direction_params(lam_re, lam_im, log_dt, b_re, b_im, c_re, c_im, rev):
    dt = jnp.exp(log_dt)[:, None]
    mag = jnp.exp(lam_re * dt)
    ar, ai = mag * jnp.cos(lam_im * dt), mag * jnp.sin(lam_im * dt)
    den = lam_re * lam_re + lam_im * lam_im
    cr = ((ar - 1) * lam_re + ai * lam_im) / den
    ci = (ai * lam_re - (ar - 1) * lam_im) / den
    bb_re = cr[..., None] * b_re - ci[..., None] * b_im
    bb_im = cr[..., None] * b_im + ci[..., None] * b_re

    def bdiag_in(bb):
        t = bb.reshape(S5_G // S5_BG, S5_BG, S5_N, S5_CH)
        eye = jnp.eye(S5_BG, dtype=F32)
        return jnp.einsum('jgnc,gh->jgchn', t, eye).reshape(
            S5_G // S5_BG, S5_BG * S5_CH, S5_BG * S5_N).astype(BF16)

    def bdiag_out(cc):
        t = cc.reshape(S5_G // S5_CG, S5_CG, S5_CH, S5_N)
        eye = jnp.eye(S5_CG, dtype=F32)
        return jnp.einsum('jgcn,gh->jgnhc', t, eye).reshape(
            S5_G // S5_CG, S5_CG * S5_N, S5_CG * S5_CH).astype(BF16)

    a_r, a_i = ar.reshape(-1), ai.reshape(-1)
    pw = [(a_r, a_i)]
    for _ in range(S5_TILE - 1):
        pr, pim = pw[-1]
        pw.append((pr * a_r - pim * a_i, pr * a_i + pim * a_r))
    rows = jnp.arange(S5_TILE)[:, None]
    coef = []
    for sh in S5_SHIFTS:
        valid = (rows <= S5_TILE - 1 - sh) if rev else (rows >= sh)
        coef.append(jnp.where(valid, pw[sh - 1][0][None, :], 0.0))
        coef.append(jnp.where(valid, pw[sh - 1][1][None, :], 0.0))
    order = list(range(S5_TILE - 1, -1, -1)) if rev else list(range(S5_TILE))
    coef.append(jnp.stack([pw[k][0] for k in order]))
    coef.append(jnp.stack([pw[k][1] for k in order]))
    return (bdiag_in(bb_re), bdiag_in(bb_im), bdiag_out(c_re), bdiag_out(c_im),
            jnp.stack(coef).astype(F32))


def s5_scan(u, h0r, h0i, dparams, rev):
    bn, seq_len, _ = u.shape
    bdr, bdi, cdr, cdi, coef = dparams
    tt = min(256, seq_len)
    nblk = seq_len // tt
    tmap = (lambda b, i: (b, nblk - 1 - i, 0)) if rev else (lambda b, i: (b, i, 0))
    const3 = lambda b, i: (0, 0, 0)
    y, hfr, hfi = pl.pallas_call(
        functools.partial(_s5_scan_body, rev=rev, tt=tt),
        grid=(bn, nblk),
        in_specs=[pl.BlockSpec((1, tt, S5_W), tmap),
                  pl.BlockSpec((1, 1, S5_LANES), lambda b, i: (b, 0, 0)),
                  pl.BlockSpec((1, 1, S5_LANES), lambda b, i: (b, 0, 0)),
                  pl.BlockSpec(bdr.shape, const3),
                  pl.BlockSpec(bdi.shape, const3),
                  pl.BlockSpec(cdr.shape, const3),
                  pl.BlockSpec(cdi.shape, const3),
                  pl.BlockSpec(coef.shape, const3)],
        out_specs=[pl.BlockSpec((1, tt, S5_W), tmap),
                   pl.BlockSpec((1, 1, S5_LANES), lambda b, i: (b, 0, 0)),
                   pl.BlockSpec((1, 1, S5_LANES), lambda b, i: (b, 0, 0))],
        out_shape=[jax.ShapeDtypeStruct((bn, seq_len, S5_W), F32),
                   jax.ShapeDtypeStruct((bn, 1, S5_LANES), F32),
                   jax.ShapeDtypeStruct((bn, 1, S5_LANES), F32)],
        scratch_shapes=[pltpu.VMEM((tt, S5_LANES), F32), pltpu.VMEM((tt, S5_LANES), F32),
                        pltpu.VMEM((1, S5_LANES), F32), pltpu.VMEM((1, S5_LANES), F32)],
        compiler_params=pltpu.CompilerParams(
            dimension_semantics=("parallel", "arbitrary"),
            vmem_limit_bytes=V7X_VMEM_LIMIT),
        name="s5_scan_bwd" if rev else "s5_scan_fwd",
    )(u, h0r.reshape(bn, 1, S5_LANES), h0i.reshape(bn, 1, S5_LANES), bdr, bdi, cdr, cdi, coef)
    return y, hfr.reshape(bn, S5_LANES), hfi.reshape(bn, S5_LANES)


def _s5_glu_body(yf_ref, yb_ref, u_ref, d_ref, w_ref, b_ref, o_ref):
    y = yf_ref[...] + yb_ref[...] + d_ref[...] * u_ref[...]
    g = jax.nn.gelu(y)
    z = jnp.dot(g.astype(BF16), w_ref[...], preferred_element_type=F32) + b_ref[...]
    o_ref[...] = (g * jax.nn.sigmoid(z)).astype(o_ref.dtype)


def s5_glu(yf, yb, u, d, glu_w, glu_b):
    m = yf.shape[0]
    tm = _pick(m, (512, 256, 128, 64, 32, 16, 8))
    row = pl.BlockSpec((tm, S5_W), lambda i: (i, 0))
    vec = pl.BlockSpec((1, S5_W), lambda i: (0, 0))
    return pl.pallas_call(
        _s5_glu_body,
        grid=(m // tm,),
        in_specs=[row, row, row, vec, pl.BlockSpec((S5_W, S5_W), lambda i: (0, 0)), vec],
        out_specs=row,
        out_shape=jax.ShapeDtypeStruct((m, S5_W), F32),
        compiler_params=pltpu.CompilerParams(
            dimension_semantics=("parallel",), vmem_limit_bytes=V7X_VMEM_LIMIT),
        name="s5_glu",
    )(yf, yb, u, d.reshape(1, S5_W), glu_w.astype(BF16), glu_b.reshape(1, S5_W))


def s5_stream(u, state, lam_re, lam_im, log_dt, b_re, b_im, c_re, c_im, d, glu_w, glu_b):
    bn, seq_len, _ = u.shape
    pf = _s5_direction_params(lam_re[0], lam_im[0], log_dt[0], b_re[0], b_im[0], c_re[0], c_im[0], False)
    pb = _s5_direction_params(lam_re[1], lam_im[1], log_dt[1], b_re[1], b_im[1], c_re[1], c_im[1], True)
    yf, fr, fi = s5_scan(u, state[0], state[1], pf, False)
    yb, br, bi = s5_scan(u, state[2], state[3], pb, True)
    m = bn * seq_len
    out = s5_glu(yf.reshape(m, S5_W), yb.reshape(m, S5_W), u.reshape(m, S5_W), d, glu_w, glu_b)
    return out.reshape(bn, seq_len, S5_W), (fr, fi, br, bi)


def ssd_scan(x, dt, a_neg, bm, cm, h0):
    bn, seq_len, nh, hp = x.shape
    ng, ns = bm.shape[2], bm.shape[3]
    nr = nh // ng
    q = SSD_CHUNK
    nc = seq_len // q
    xdt = (x * dt[..., None]).reshape(bn, nc, q, ng, nr, hp)
    cum = jnp.cumsum((dt * a_neg).reshape(bn, nc, q, ng, nr), axis=2)
    bm = bm.reshape(bn, nc, q, ng, ns)
    cm = cm.reshape(bn, nc, q, ng, ns)
    idx = jnp.arange(q)
    incl = idx[:, None] >= idx[None, :]
    cum_t = jnp.moveaxis(cum, 2, -1)
    seg = jnp.exp(jnp.where(incl, cum_t[..., :, None] - cum_t[..., None, :], -jnp.inf))
    cb = jnp.einsum('bclgn,bcsgn->bcgls', cm, bm)
    y_diag = jnp.einsum('bcgrls,bcsgrp->bclgrp', seg * cb[:, :, :, None], xdt)

    def step(h, inp):
        c_c, b_c, xdt_c, cum_c = inp
        y_off = jnp.einsum('blgn,bgrpn->blgrp', c_c, h) * jnp.exp(cum_c)[..., None]
        dte = jnp.exp(cum_c[:, -1:] - cum_c)
        st = jnp.einsum('blgn,blgrp->bgrpn', b_c, xdt_c * dte[..., None])
        h = h * jnp.exp(cum_c[:, -1])[..., None, None] + st
        return h, y_off
    xs = tuple(jnp.moveaxis(t, 1, 0) for t in (cm, bm, xdt, cum))
    h_last, y_off = lax.scan(step, h0.reshape(bn, ng, nr, hp, ns), xs)
    y = y_diag + jnp.moveaxis(y_off, 0, 1)
    return y.reshape(bn, seq_len, nh, hp), h_last.reshape(bn, nh, hp, ns)


def ssd_stream(z, xbc, dt_raw, state, conv_w, conv_b, dt_bias, a_log, d, norm_g):
    bn, seq_len, _ = z.shape
    xbc = jax.nn.silu(dwconv(xbc, conv_w, conv_b))
    xs, bm, cm = jnp.split(xbc, [SSD_DI, SSD_DI + SSD_G * SSD_N], axis=-1)
    xs = xs.reshape(bn, seq_len, SSD_H, SSD_P)
    bm = bm.reshape(bn, seq_len, SSD_G, SSD_N)
    cm = cm.reshape(bn, seq_len, SSD_G, SSD_N)
    dt = jax.nn.softplus(dt_raw.reshape(bn, seq_len, 2, SSD_H) + dt_bias)
    a_neg = -jnp.exp(a_log)
    yf, hf = ssd_scan(xs, dt[:, :, 0], a_neg[0], bm, cm, state[0])
    yb, hb = ssd_scan(flip(xs), flip(dt[:, :, 1]), a_neg[1], flip(bm), flip(cm), state[1])
    y = yf + flip(yb) + d[:, None] * xs
    y = y.reshape(bn, seq_len, SSD_DI) * jax.nn.silu(z)
    return rmsnorm(y, norm_g), (hf, hb)


def gated_delta_rule(q, k, v, beta, g, s0):
    bn, seq_len, nh, dk = q.shape
    dv = v.shape[-1]
    qn = DN_CHUNK
    nc = seq_len // qn

    def chunks(t):
        return jnp.swapaxes(t.reshape((bn, nc, qn) + t.shape[2:]), 2, 3)
    qc, kc, vc, bc, gc = (chunks(t) for t in (q, k, v, beta, g))
    gcum = jnp.cumsum(gc, axis=-1)
    idx = jnp.arange(qn)
    incl = idx[:, None] >= idx[None, :]
    strict = idx[:, None] > idx[None, :]
    dmat = jnp.exp(jnp.where(incl, gcum[..., :, None] - gcum[..., None, :], -jnp.inf))
    kb = kc * bc[..., None]
    m = jnp.where(strict, jnp.einsum('bnhid,bnhjd->bnhij', kb, kc) * dmat, 0.0)
    rhs = jnp.concatenate([vc * bc[..., None], kb * jnp.exp(gcum)[..., None]], axis=-1)
    sol = lax.linalg.triangular_solve(m + jnp.eye(qn, dtype=m.dtype), rhs,
                                      left_side=True, lower=True, unit_diagonal=True)
    u, w = sol[..., :dv], sol[..., dv:]
    qk = jnp.einsum('bnhid,bnhjd->bnhij', qc, kc) * dmat
    q_dec = qc * jnp.exp(gcum)[..., None]
    k_tail = kc * jnp.exp(gcum[..., -1:] - gcum)[..., None]
    tot = jnp.exp(gcum[..., -1])

    def step(s, inp):
        u_c, w_c, qk_c, qd_c, kt_c, tot_c = inp
        v_new = u_c - jnp.einsum('bhqk,bhkv->bhqv', w_c, s)
        o = jnp.einsum('bhqk,bhkv->bhqv', qd_c, s) + jnp.einsum('bhij,bhjv->bhiv', qk_c, v_new)
        s = s * tot_c[..., None, None] + jnp.einsum('bhqk,bhqv->bhkv', kt_c, v_new)
        return s, o
    xs = tuple(jnp.moveaxis(t, 1, 0) for t in (u, w, qk, q_dec, k_tail, tot))
    s_last, o = lax.scan(step, s0, xs)
    o = jnp.swapaxes(jnp.moveaxis(o, 0, 1), 2, 3).reshape(bn, seq_len, nh, dv)
    return o, s_last


def deltanet_stream(qkv, z, a_raw, b_raw, state, conv_w, conv_b, dt_bias, a_log, norm_g):
    bn, seq_len, _ = z.shape
    qkv = jax.nn.silu(dwconv(qkv, conv_w, conv_b))
    q, k, v = jnp.split(qkv, [DN_H * DN_DK, 2 * DN_H * DN_DK], axis=-1)
    q = l2norm(q.reshape(bn, seq_len, DN_H, DN_DK)) * (DN_DK ** -0.5)
    k = l2norm(k.reshape(bn, seq_len, DN_H, DN_DK))
    v = v.reshape(bn, seq_len, DN_H, DN_DV)
    beta = jax.nn.sigmoid(b_raw.reshape(bn, seq_len, 2, DN_H))
    g = -jnp.exp(a_log) * jax.nn.softplus(a_raw.reshape(bn, seq_len, 2, DN_H) + dt_bias)
    of, sf = gated_delta_rule(q, k, v, beta[:, :, 0], g[:, :, 0], state[0])
    ob, sb = gated_delta_rule(flip(q), flip(k), flip(v), flip(beta[:, :, 1]), flip(g[:, :, 1]), state[1])
    o = rmsnorm(of + flip(ob), norm_g)
    o = o.reshape(bn, seq_len, DN_H * DN_DV) * jax.nn.silu(z)
    return o, (sf, sb)


def rglru_stream(xr, gate, state, conv_w, conv_b, wa, ba, wx, bx, lam):
    bn, seq_len, _ = xr.shape
    xh = dwconv(xr, conv_w, conv_b).reshape(bn, seq_len, RG_H, RG_BS)

    def coeffs(dr):
        r = jax.nn.sigmoid(jnp.einsum('blhi,hij->blhj', xh, wa[dr]) + ba[dr])
        i = jax.nn.sigmoid(jnp.einsum('blhi,hij->blhj', xh, wx[dr]) + bx[dr])
        log_a = -RG_C * r * jax.nn.softplus(-lam[dr])
        return jnp.exp(log_a), jnp.sqrt(-jnp.expm1(2.0 * log_a)) * (i * xh)
    af, bf = coeffs(0)
    hf = linear_scan(af, bf, state[0])
    ab, bb = coeffs(1)
    hb = linear_scan(flip(ab), flip(bb), state[1])
    y = (hf + flip(hb)).reshape(bn, seq_len, RG_W) * jax.nn.gelu(gate)
    return y, (hf[:, -1], hb[:, -1])


def even_mixer(hx, hc, w_in, w_out, s5_params, ssd_params, need_ctx):
    bn = hx.shape[0]

    def stream(h, st_s5, st_ssd):
        p = mm3(h, w_in)
        u, z, xbc, dt_raw = jnp.split(p, EV_CUTS, axis=-1)
        ya, st_a = s5_stream(u, st_s5, *s5_params)
        yb, st_b = ssd_stream(z, xbc, dt_raw, st_ssd, *ssd_params)
        return jnp.concatenate([ya, yb], axis=-1), st_a, st_b
    zs5 = jnp.zeros((bn, S5_LANES), F32)
    zssd = jnp.zeros((bn, SSD_H, SSD_P, SSD_N), F32)
    yc, st_a, st_b = stream(hc, (zs5, zs5, zs5, zs5), (zssd, zssd))
    yx, _, _ = stream(hx, st_a, st_b)
    return mm3(yx, w_out), (mm3(yc, w_out) if need_ctx else None)


def odd_mixer(hx, hc, w_in, w_out, dn_params, rg_params, need_ctx):
    bn = hx.shape[0]

    def stream(h, st_dn, st_rg):
        p = mm3(h, w_in)
        qkv, z, a_raw, b_raw, xr, gate = jnp.split(p, OD_CUTS, axis=-1)
        yd, st_d = deltanet_stream(qkv, z, a_raw, b_raw, st_dn, *dn_params)
        yr, st_r = rglru_stream(xr, gate, st_rg, *rg_params)
        return jnp.concatenate([yd, yr], axis=-1), st_d, st_r
    zdn = jnp.zeros((bn, DN_H, DN_DK, DN_DV), F32)
    zrg = jnp.zeros((bn, RG_H, RG_BS), F32)
    yc, st_d, st_r = stream(hc, (zdn, zdn), (zrg, zrg))
    yx, _, _ = stream(hx, st_d, st_r)
    return mm3(yx, w_out), (mm3(yc, w_out) if need_ctx else None)


PEER_TM = 512
PEER_EB = 512
PEER_RT = 16
PEER_SEL = PEER_H * PEER_TOPK


def _peer_body(flags_ref, ht_ref, u_ref, vt_ref, s1_ref, s2_ref, e1_ref, e2_ref, tau_ref,
               pidx_ref, ridx_ref, gate_ref, o_ref, st_s, g_s, *, tm, eb):
    i = pl.program_id(0)
    j = pl.program_id(1)
    npk = eb // PEER_NK
    st_s[...] = jnp.dot(u_ref[...], ht_ref[...], preferred_element_type=F32)

    @pl.when(flags_ref[i] == 0)
    def _():
        for lg in range(tm // V7X_LANES):
            ls = slice(V7X_LANES * lg, V7X_LANES * (lg + 1))
            t_rows = [tau_ref[pl.ds(h, 1), ls] for h in range(PEER_H)]
            for pk in range(npk):
                p = j * npk + pk
                a_tile = s1_ref[p, :, ls]
                z_tile = e1_ref[p, :, ls]
                a_rows = [a_tile[h:h + 1, :] for h in range(PEER_H)]
                z_rows = [z_tile[h:h + 1, :] for h in range(PEER_H)]

                def rbody(rt, carry, ls=ls, pk=pk, a_rows=a_rows, z_rows=z_rows, t_rows=t_rows):
                    r0 = pl.multiple_of(rt * PEER_RT, PEER_RT)
                    acc = jnp.zeros((PEER_RT, V7X_LANES), F32)
                    for h in range(PEER_H):
                        s2 = s2_ref[h, pl.ds(r0, PEER_RT), ls]
                        e2 = e2_ref[h, pl.ds(r0, PEER_RT), ls]
                        acc = acc + jnp.where(a_rows[h] + s2 >= t_rows[h], z_rows[h] * e2, 0.0)
                    row0 = pl.multiple_of(pk * PEER_NK + r0, PEER_RT)
                    act = jax.nn.gelu(st_s[pl.ds(row0, PEER_RT), ls])
                    g_s[pl.ds(row0, PEER_RT), ls] = (acc * act).astype(BF16)
                    return carry

                lax.fori_loop(0, PEER_NK // PEER_RT, rbody, 0)

    @pl.when(flags_ref[i] != 0)
    def _():
        riota = lax.broadcasted_iota(jnp.int32, (PEER_NK, V7X_LANES), 0)
        for lg in range(tm // V7X_LANES):
            ls = slice(V7X_LANES * lg, V7X_LANES * (lg + 1))
            for pk in range(npk):
                p = j * npk + pk

                def kbody(k8, w, ls=ls, p=p):
                    k0 = pl.multiple_of(k8 * V7X_SUBLANES, V7X_SUBLANES)
                    ptile = pidx_ref[pl.ds(k0, V7X_SUBLANES), ls]
                    rtile = ridx_ref[pl.ds(k0, V7X_SUBLANES), ls]
                    gtile = gate_ref[pl.ds(k0, V7X_SUBLANES), ls]
                    ctile = jnp.where(ptile == p, gtile, 0.0)
                    for k in range(V7X_SUBLANES):
                        w = w + jnp.where(rtile[k:k + 1, :] == riota, ctile[k:k + 1, :], 0.0)
                    return w

                w = lax.fori_loop(0, PEER_SEL // V7X_SUBLANES, kbody,
                                  jnp.zeros((PEER_NK, V7X_LANES), F32))
                rows = slice(pk * PEER_NK, (pk + 1) * PEER_NK)
                g_s[rows, ls] = (w * jax.nn.gelu(st_s[rows, ls])).astype(BF16)

    contrib = jnp.dot(vt_ref[...], g_s[...], preferred_element_type=F32)

    @pl.when(j == 0)
    def _():
        o_ref[...] = contrib

    @pl.when(j != 0)
    def _():
        o_ref[...] += contrib


def peer_experts(flags, ht, u_bf, vt_bf, s1t, s2t, e1t, e2t, taut, pidx, ridx, gate, tm):
    dm, t = ht.shape
    eb = PEER_EB
    head3 = pl.BlockSpec((PEER_H, PEER_NK, tm), lambda i, j, f: (0, 0, i))
    key3 = pl.BlockSpec((PEER_NK, PEER_H, tm), lambda i, j, f: (0, 0, i))
    sel2 = pl.BlockSpec((PEER_SEL, tm), lambda i, j, f: (0, i))
    grid_spec = pltpu.PrefetchScalarGridSpec(
        num_scalar_prefetch=1,
        grid=(t // tm, PEER_E // eb),
        in_specs=[pl.BlockSpec((dm, tm), lambda i, j, f: (0, i)),
                  pl.BlockSpec((eb, dm), lambda i, j, f: (j, 0)),
                  pl.BlockSpec((dm, eb), lambda i, j, f: (0, j)),
                  key3, head3, key3, head3,
                  pl.BlockSpec((PEER_H, tm), lambda i, j, f: (0, i)),
                  sel2, sel2, sel2],
        out_specs=pl.BlockSpec((dm, tm), lambda i, j, f: (0, i)),
        scratch_shapes=[pltpu.VMEM((eb, tm), F32), pltpu.VMEM((eb, tm), BF16)])
    return pl.pallas_call(
        functools.partial(_peer_body, tm=tm, eb=eb),
        grid_spec=grid_spec,
        out_shape=jax.ShapeDtypeStruct((dm, t), F32),
        compiler_params=pltpu.CompilerParams(
            dimension_semantics=("parallel", "arbitrary"),
            vmem_limit_bytes=V7X_VMEM_LIMIT),
        name="peer_experts",
    )(flags, ht, u_bf, vt_bf, s1t, s2t, e1t, e2t, taut, pidx, ridx, gate)


def peer(h, w_q, k1, k2, u_bf, vt_bf, tm=PEER_TM):
    bn, seq_len, dm = h.shape
    t = bn * seq_len
    q = mm3(h, w_q).reshape(t, PEER_H, PEER_DK)
    s1 = jnp.einsum('thd,hkd->thk', q[..., :PEER_HALF], k1)
    s2 = jnp.einsum('thd,hkd->thk', q[..., PEER_HALF:], k2)
    t1x, i1x = lax.top_k(s1, PEER_TOPK + 1)
    t2x, i2x = lax.top_k(s2, PEER_TOPK + 1)
    t1, i1, t2, i2 = t1x[..., :PEER_TOPK], i1x[..., :PEER_TOPK], t2x[..., :PEER_TOPK], i2x[..., :PEER_TOPK]
    cand = (t1[..., :, None] + t2[..., None, :]).reshape(t, PEER_H, PEER_TOPK * PEER_TOPK)
    top, ci = lax.top_k(cand, PEER_TOPK)
    pidx = jnp.take_along_axis(i1, ci // PEER_TOPK, axis=-1)
    ridx = jnp.take_along_axis(i2, ci % PEER_TOPK, axis=-1)
    gate = jax.nn.softmax(top, axis=-1)
    tau = top[..., PEER_TOPK - 1]
    zsum = jnp.sum(jnp.exp(top - top[..., :1]), axis=-1)
    tie = ((jnp.sum(cand >= tau[..., None], axis=-1) != PEER_TOPK)
           | (t1x[..., PEER_TOPK] + t2[..., 0] >= tau)
           | (t1[..., 0] + t2x[..., PEER_TOPK] >= tau))
    flags = jnp.any(tie.reshape(t // tm, tm * PEER_H), axis=-1).astype(jnp.int32)
    e1 = jnp.exp(s1 - t1[..., :1]) / zsum[..., None]
    e2 = jnp.exp(s2 - t2[..., :1])

    def tr3(a):
        return jnp.transpose(a, (1, 2, 0))
    outt = peer_experts(
        flags, h.reshape(t, dm).T.astype(BF16), u_bf, vt_bf,
        jnp.transpose(s1, (2, 1, 0)), tr3(s2), jnp.transpose(e1, (2, 1, 0)), tr3(e2), tau.T,
        pidx.reshape(t, PEER_SEL).T, ridx.reshape(t, PEER_SEL).T, gate.reshape(t, PEER_SEL).T, tm)
    return outt.T.reshape(bn, seq_len, dm)


def kernel(x, c, ctx, c_ctx, ada_w, ada_b, norm1_g, norm2_g, final_g, ev_w_in, ev_w_out, s5_lam_re, s5_lam_im, s5_log_dt, s5_b_re, s5_b_im, s5_c_re, s5_c_im, s5_d, s5_glu_w, s5_glu_b, ssd_conv_w, ssd_conv_b, ssd_dt_bias, ssd_a_log, ssd_d, ssd_norm_g, od_w_in, od_w_out, dn_conv_w, dn_conv_b, dn_dt_bias, dn_a_log, dn_norm_g, rg_conv_w, rg_conv_b, rg_wa, rg_ba, rg_wx, rg_bx, rg_lam, peer_wq, peer_k1, peer_k2, peer_u, peer_v):
    bn, seq_len, _ = x.shape
    rows = seq_len // GRID_W
    sc = jax.nn.silu(c)
    scc = jax.nn.silu(c_ctx)
    for layer in range(DEPTH):
        j = layer // 2
        need_ctx = layer < DEPTH - 1
        mx = (sc @ ada_w[layer] + ada_b[layer]).reshape(bn, 6, 1, D_MODEL)
        mc = (scc @ ada_w[layer] + ada_b[layer]).reshape(6, D_MODEL)
        hx = modulate(rmsnorm(x, norm1_g[layer]), mx[:, 0], mx[:, 1])
        hc = modulate(rmsnorm(ctx, norm1_g[layer]), mc[0], mc[1])
        if layer % 2 == 0:
            s5_params = (s5_lam_re[j], s5_lam_im[j], s5_log_dt[j], s5_b_re[j], s5_b_im[j],
                         s5_c_re[j], s5_c_im[j], s5_d[j], s5_glu_w[j], s5_glu_b[j])
            ssd_params = (ssd_conv_w[j], ssd_conv_b[j], ssd_dt_bias[j], ssd_a_log[j],
                          ssd_d[j], ssd_norm_g[j])
            ox, oc = even_mixer(hx, hc, ev_w_in[j], ev_w_out[j], s5_params, ssd_params, need_ctx)
        else:
            dn_params = (dn_conv_w[j], dn_conv_b[j], dn_dt_bias[j], dn_a_log[j], dn_norm_g[j])
            rg_params = (rg_conv_w[j], rg_conv_b[j], rg_wa[j], rg_ba[j], rg_wx[j], rg_bx[j], rg_lam[j])
            ox, oc = odd_mixer(to_col_major(hx, rows), hc, od_w_in[j], od_w_out[j],
                               dn_params, rg_params, need_ctx)
            ox = to_row_major(ox, rows)
        x = x + mx[:, 2] * ox
        hx = modulate(rmsnorm(x, norm2_g[layer]), mx[:, 3], mx[:, 4])
        u_bf = peer_u[layer].astype(BF16)
        vt_bf = peer_v[layer].T.astype(BF16)
        x = x + mx[:, 5] * peer(hx, peer_wq[layer], peer_k1[layer], peer_k2[layer], u_bf, vt_bf)
        if need_ctx:
            ctx = ctx + mc[2] * oc
            hc = modulate(rmsnorm(ctx, norm2_g[layer]), mc[3], mc[4])
            ctx = ctx + mc[5] * peer(hc, peer_wq[layer], peer_k1[layer], peer_k2[layer], u_bf, vt_bf)
    return rmsnorm(x, final_g)
```

```python
import functools
import math

import jax
import jax.numpy as jnp
from jax import lax
from jax.experimental import pallas as pl
from jax.experimental.pallas import tpu as pltpu

D_MODEL = 2048
DEPTH = 2
GRID_W = 64
EPS = 1e-6
CONV_W = 4
CONV_PAD = (CONV_W // 2, CONV_W - 1 - CONV_W // 2)

S5_W = D_MODEL // 2
S5_CH = 16
S5_G = S5_W // S5_CH
S5_N = 64
SSD_P = 64
SSD_H = D_MODEL // SSD_P
SSD_DI = SSD_H * SSD_P
SSD_N = 128
SSD_G = 4
SSD_CHUNK = 128
SSD_CONV_CH = SSD_DI + 2 * SSD_G * SSD_N
EV_IN = S5_W + SSD_DI + SSD_CONV_CH + 2 * SSD_H
EV_MIX = S5_W + SSD_DI
DN_DK = 128
DN_DV = 128
DN_H = D_MODEL // DN_DK
DN_CHUNK = 64
DN_QKV = DN_H * (2 * DN_DK + DN_DV)
RG_W = D_MODEL // 2
RG_H = 8
RG_BS = RG_W // RG_H
RG_C = 8.0
OD_IN = DN_QKV + DN_H * DN_DV + 4 * DN_H + 2 * RG_W
OD_MIX = DN_H * DN_DV + RG_W
OD_CUTS = [DN_QKV,
           DN_QKV + DN_H * DN_DV,
           DN_QKV + DN_H * DN_DV + 2 * DN_H,
           DN_QKV + DN_H * DN_DV + 4 * DN_H,
           DN_QKV + DN_H * DN_DV + 4 * DN_H + RG_W]
EV_CUTS = [S5_W, S5_W + SSD_DI, S5_W + SSD_DI + SSD_CONV_CH]
PEER_H = 8
PEER_DK = 256
PEER_HALF = PEER_DK // 2
PEER_NK = 128
PEER_E = PEER_NK * PEER_NK
PEER_TOPK = 16
PEER_BLOCK = 128

F32 = jnp.float32
BF16 = jnp.bfloat16

V7X_LANES = 128
V7X_SUBLANES = 8
V7X_VMEM_LIMIT = 56 * 1024 * 1024


def _mm_body(x_ref, w_ref, o_ref):
    o_ref[...] = jnp.dot(x_ref[...], w_ref[...], preferred_element_type=F32).astype(o_ref.dtype)


def _pick(n, cands):
    for c in cands:
        if n % c == 0:
            return c
    raise ValueError(f"no tile for {n}")


def mm(x, w, out_dtype=F32):
    m, k = x.shape
    n = w.shape[1]
    n_pad = -n % V7X_LANES
    if n_pad:
        w = jnp.pad(w, ((0, 0), (0, n_pad)))
    np_ = n + n_pad
    tm = _pick(m, (1024, 512, 256, 128, 64, 32, 16, 8))
    tn = _pick(np_, (512, 256, 128))
    out = pl.pallas_call(
        _mm_body,
        grid=(m // tm, np_ // tn),
        in_specs=[pl.BlockSpec((tm, k), lambda i, j: (i, 0)),
                  pl.BlockSpec((k, tn), lambda i, j: (0, j))],
        out_specs=pl.BlockSpec((tm, tn), lambda i, j: (i, j)),
        out_shape=jax.ShapeDtypeStruct((m, np_), out_dtype),
        compiler_params=pltpu.CompilerParams(
            dimension_semantics=("parallel", "arbitrary"),
            vmem_limit_bytes=V7X_VMEM_LIMIT),
        name="mm",
    )(x.astype(BF16), w.astype(BF16))
    return out[:, :n] if n_pad else out


def mm3(h, w):
    b, l, d = h.shape
    return mm(h.reshape(b * l, d), w).reshape(b, l, w.shape[1])


def rmsnorm(x, g):
    xf = x.astype(F32)
    y = xf * lax.rsqrt(jnp.mean(xf * xf, axis=-1, keepdims=True) + EPS)
    return (y * g.astype(F32)).astype(x.dtype)


def l2norm(x):
    return x * lax.rsqrt(jnp.sum(x * x, axis=-1, keepdims=True) + EPS)


def modulate(h, shift, scale):
    return h * (1 + scale) + shift


def dwconv(x, w, b):
    y = lax.conv_general_dilated(x, w[:, None, :], (1,), [CONV_PAD],
                                 dimension_numbers=('NWC', 'WIO', 'NWC'),
                                 feature_group_count=x.shape[-1])
    return y + b


def flip(t):
    return jnp.flip(t, axis=1)


def to_col_major(x, rows):
    b, l, d = x.shape
    return x.reshape(b, rows, GRID_W, d).transpose(0, 2, 1, 3).reshape(b, l, d)


def to_row_major(x, rows):
    b, l, d = x.shape
    return x.reshape(b, GRID_W, rows, d).transpose(0, 2, 1, 3).reshape(b, l, d)


def linear_scan(a, b, h0):
    b = b.at[:, 0].add(a[:, 0] * h0)

    def comb(e1, e2):
        return e1[0] * e2[0], e2[0] * e1[1] + e2[1]
    return lax.associative_scan(comb, (a, b), axis=1)[1]


S5_LANES = S5_G * S5_N
S5_TILE = 8
S5_SHIFTS = (1, 2, 4)
S5_BG = 8
S5_CG = 16
S5_CHUNK = 1024


def _s5_scan_body(u_ref, h0r_ref, h0i_ref, bdr_ref, bdi_ref, cdr_ref, cdi_ref, coef_ref,
                  y_ref, hfr_ref, hfi_ref, xr_s, xi_s, cr_s, ci_s, *, rev, tt):
    @pl.when(pl.program_id(1) == 0)
    def _():
        cr_s[...] = h0r_ref[0]
        ci_s[...] = h0i_ref[0]

    ub = u_ref[0].astype(BF16)
    kin = S5_BG * S5_CH
    kout = S5_BG * S5_N
    for j in range(S5_G // S5_BG):
        uj = ub[:, kin * j:kin * (j + 1)]
        xr_s[:, kout * j:kout * (j + 1)] = jnp.dot(uj, bdr_ref[j], preferred_element_type=F32)
        xi_s[:, kout * j:kout * (j + 1)] = jnp.dot(uj, bdi_ref[j], preferred_element_type=F32)

    nt = tt // S5_TILE
    carry_row = 0 if rev else S5_TILE - 1

    def tile(kk, carry):
        k = (nt - 1 - kk) if rev else kk
        r0 = pl.multiple_of(k * S5_TILE, S5_TILE)
        for c in range(S5_LANES // S5_CHUNK):
            sl = slice(S5_CHUNK * c, S5_CHUNK * (c + 1))
            xr = xr_s[pl.ds(r0, S5_TILE), sl]
            xi = xi_s[pl.ds(r0, S5_TILE), sl]
            for idx, sh in enumerate(S5_SHIFTS):
                pr = coef_ref[2 * idx, :, sl]
                pim = coef_ref[2 * idx + 1, :, sl]
                rs = (S5_TILE - sh) if rev else sh
                sr = pltpu.roll(xr, rs, 0)
                si = pltpu.roll(xi, rs, 0)
                xr, xi = xr + (pr * sr - pim * si), xi + (pr * si + pim * sr)
            pr = coef_ref[2 * len(S5_SHIFTS), :, sl]
            pim = coef_ref[2 * len(S5_SHIFTS) + 1, :, sl]
            hr = cr_s[:, sl]
            hi = ci_s[:, sl]
            xr, xi = xr + (pr * hr - pim * hi), xi + (pr * hi + pim * hr)
            xr_s[pl.ds(r0, S5_TILE), sl] = xr
            xi_s[pl.ds(r0, S5_TILE), sl] = xi
            cr_s[:, sl] = xr[carry_row:carry_row + 1, :]
            ci_s[:, sl] = xi[carry_row:carry_row + 1, :]
        return carry

    lax.fori_loop(0, nt, tile, 0)

    kc = S5_CG * S5_N
    nc = S5_CG * S5_CH
    for j in range(S5_G // S5_CG):
        sr = xr_s[:, kc * j:kc * (j + 1)].astype(BF16)
        si = xi_s[:, kc * j:kc * (j + 1)].astype(BF16)
        y_ref[0, :, nc * j:nc * (j + 1)] = (
            jnp.dot(sr, cdr_ref[j], preferred_element_type=F32)
            - jnp.dot(si, cdi_ref[j], preferred_element_type=F32))
    hfr_ref[0] = cr_s[...]
    hfi_ref[0] = ci_s[...]


def _s5_direction_params(lam_re, lam_im, log_dt, b_re, b_im, c_re, c_im, rev):
    dt = jnp.exp(log_dt)[:, None]
    mag = jnp.exp(lam_re * dt)
    ar, ai = mag * jnp.cos(lam_im * dt), mag * jnp.sin(lam_im * dt)
    den = lam_re * lam_re + lam_im * lam_im
    cr = ((ar - 1) * lam_re + ai * lam_im) / den
    ci = (ai * lam_re - (ar - 1) * lam_im) / den
    bb_re = cr[..., None] * b_re - ci[..., None] * b_im
    bb_im = cr[..., None] * b_im + ci[..., None] * b_re

    def bdiag_in(bb):
        t = bb.reshape(S5_G // S5_BG, S5_BG, S5_N, S5_CH)
        eye = jnp.eye(S5_BG, dtype=F32)
        return jnp.einsum('jgnc,gh->jgchn', t, eye).reshape(
            S5_G // S5_BG, S5_BG * S5_CH, S5_BG * S5_N).astype(BF16)

    def bdiag_out(cc):
        t = cc.reshape(S5_G // S5_CG, S5_CG, S5_CH, S5_N)
        eye = jnp.eye(S5_CG, dtype=F32)
        return jnp.einsum('jgcn,gh->jgnhc', t, eye).reshape(
            S5_G // S5_CG, S5_CG * S5_N, S5_CG * S5_CH).astype(BF16)

    a_r, a_i = ar.reshape(-1), ai.reshape(-1)
    pw = [(a_r, a_i)]
    for _ in range(S5_TILE - 1):
        pr, pim = pw[-1]
        pw.append((pr * a_r - pim * a_i, pr * a_i + pim * a_r))
    rows = jnp.arange(S5_TILE)[:, None]
    coef = []
    for sh in S5_SHIFTS:
        valid = (rows <= S5_TILE - 1 - sh) if rev else (rows >= sh)
        coef.append(jnp.where(valid, pw[sh - 1][0][None, :], 0.0))
        coef.append(jnp.where(valid, pw[sh - 1][1][None, :], 0.0))
    order = list(range(S5_TILE - 1, -1, -1)) if rev else list(range(S5_TILE))
    coef.append(jnp.stack([pw[k][0] for k in order]))
    coef.append(jnp.stack([pw[k][1] for k in order]))
    return (bdiag_in(bb_re), bdiag_in(bb_im), bdiag_out(c_re), bdiag_out(c_im),
            jnp.stack(coef).astype(F32))


def s5_scan(u, h0r, h0i, dparams, rev):
    bn, seq_len, _ = u.shape
    bdr, bdi, cdr, cdi, coef = dparams
    tt = min(256, seq_len)
    nblk = seq_len // tt
    tmap = (lambda b, i: (b, nblk - 1 - i, 0)) if rev else (lambda b, i: (b, i, 0))
    const3 = lambda b, i: (0, 0, 0)
    y, hfr, hfi = pl.pallas_call(
        functools.partial(_s5_scan_body, rev=rev, tt=tt),
        grid=(bn, nblk),
        in_specs=[pl.BlockSpec((1, tt, S5_W), tmap),
                  pl.BlockSpec((1, 1, S5_LANES), lambda b, i: (b, 0, 0)),
                  pl.BlockSpec((1, 1, S5_LANES), lambda b, i: (b, 0, 0)),
                  pl.BlockSpec(bdr.shape, const3),
                  pl.BlockSpec(bdi.shape, const3),
                  pl.BlockSpec(cdr.shape, const3),
                  pl.BlockSpec(cdi.shape, const3),
                  pl.BlockSpec(coef.shape, const3)],
        out_specs=[pl.BlockSpec((1, tt, S5_W), tmap),
                   pl.BlockSpec((1, 1, S5_LANES), lambda b, i: (b, 0, 0)),
                   pl.BlockSpec((1, 1, S5_LANES), lambda b, i: (b, 0, 0))],
        out_shape=[jax.ShapeDtypeStruct((bn, seq_len, S5_W), F32),
                   jax.ShapeDtypeStruct((bn, 1, S5_LANES), F32),
                   jax.ShapeDtypeStruct((bn, 1, S5_LANES), F32)],
        scratch_shapes=[pltpu.VMEM((tt, S5_LANES), F32), pltpu.VMEM((tt, S5_LANES), F32),
                        pltpu.VMEM((1, S5_LANES), F32), pltpu.VMEM((1, S5_LANES), F32)],
        compiler_params=pltpu.CompilerParams(
            dimension_semantics=("parallel", "arbitrary"),
            vmem_limit_bytes=V7X_VMEM_LIMIT),
        name="s5_scan_bwd" if rev else "s5_scan_fwd",
    )(u, h0r.reshape(bn, 1, S5_LANES), h0i.reshape(bn, 1, S5_LANES), bdr, bdi, cdr, cdi, coef)
    return y, hfr.reshape(bn, S5_LANES), hfi.reshape(bn, S5_LANES)


def _s5_glu_body(yf_ref, yb_ref, u_ref, d_ref, w_ref, b_ref, o_ref):
    y = yf_ref[...] + yb_ref[...] + d_ref[...] * u_ref[...]
    g = jax.nn.gelu(y)
    z = jnp.dot(g.astype(BF16), w_ref[...], preferred_element_type=F32) + b_ref[...]
    o_ref[...] = (g * jax.nn.sigmoid(z)).astype(o_ref.dtype)


def s5_glu(yf, yb, u, d, glu_w, glu_b):
    m = yf.shape[0]
    tm = _pick(m, (512, 256, 128, 64, 32, 16, 8))
    row = pl.BlockSpec((tm, S5_W), lambda i: (i, 0))
    vec = pl.BlockSpec((1, S5_W), lambda i: (0, 0))
    return pl.pallas_call(
        _s5_glu_body,
        grid=(m // tm,),
        in_specs=[row, row, row, vec, pl.BlockSpec((S5_W, S5_W), lambda i: (0, 0)), vec],
        out_specs=row,
        out_shape=jax.ShapeDtypeStruct((m, S5_W), F32),
        compiler_params=pltpu.CompilerParams(
            dimension_semantics=("parallel",), vmem_limit_bytes=V7X_VMEM_LIMIT),
        name="s5_glu",
    )(yf, yb, u, d.reshape(1, S5_W), glu_w.astype(BF16), glu_b.reshape(1, S5_W))


def s5_stream(u, state, lam_re, lam_im, log_dt, b_re, b_im, c_re, c_im, d, glu_w, glu_b):
    bn, seq_len, _ = u.shape
    pf = _s5_direction_params(lam_re[0], lam_im[0], log_dt[0], b_re[0], b_im[0], c_re[0], c_im[0], False)
    pb = _s5_direction_params(lam_re[1], lam_im[1], log_dt[1], b_re[1], b_im[1], c_re[1], c_im[1], True)
    yf, fr, fi = s5_scan(u, state[0], state[1], pf, False)
    yb, br, bi = s5_scan(u, state[2], state[3], pb, True)
    m = bn * seq_len
    out = s5_glu(yf.reshape(m, S5_W), yb.reshape(m, S5_W), u.reshape(m, S5_W), d, glu_w, glu_b)
    return out.reshape(bn, seq_len, S5_W), (fr, fi, br, bi)


def ssd_scan(x, dt, a_neg, bm, cm, h0):
    bn, seq_len, nh, hp = x.shape
    ng, ns = bm.shape[2], bm.shape[3]
    nr = nh // ng
    q = SSD_CHUNK
    nc = seq_len // q
    xdt = (x * dt[..., None]).reshape(bn, nc, q, ng, nr, hp)
    cum = jnp.cumsum((dt * a_neg).reshape(bn, nc, q, ng, nr), axis=2)
    bm = bm.reshape(bn, nc, q, ng, ns)
    cm = cm.reshape(bn, nc, q, ng, ns)
    idx = jnp.arange(q)
    incl = idx[:, None] >= idx[None, :]
    cum_t = jnp.moveaxis(cum, 2, -1)
    seg = jnp.exp(jnp.where(incl, cum_t[..., :, None] - cum_t[..., None, :], -jnp.inf))
    cb = jnp.einsum('bclgn,bcsgn->bcgls', cm, bm)
    y_diag = jnp.einsum('bcgrls,bcsgrp->bclgrp', seg * cb[:, :, :, None], xdt)

    def step(h, inp):
        c_c, b_c, xdt_c, cum_c = inp
        y_off = jnp.einsum('blgn,bgrpn->blgrp', c_c, h) * jnp.exp(cum_c)[..., None]
        dte = jnp.exp(cum_c[:, -1:] - cum_c)
        st = jnp.einsum('blgn,blgrp->bgrpn', b_c, xdt_c * dte[..., None])
        h = h * jnp.exp(cum_c[:, -1])[..., None, None] + st
        return h, y_off
    xs = tuple(jnp.moveaxis(t, 1, 0) for t in (cm, bm, xdt, cum))
    h_last, y_off = lax.scan(step, h0.reshape(bn, ng, nr, hp, ns), xs)
    y = y_diag + jnp.moveaxis(y_off, 0, 1)
    return y.reshape(bn, seq_len, nh, hp), h_last.reshape(bn, nh, hp, ns)


def ssd_stream(z, xbc, dt_raw, state, conv_w, conv_b, dt_bias, a_log, d, norm_g):
    bn, seq_len, _ = z.shape
    xbc = jax.nn.silu(dwconv(xbc, conv_w, conv_b))
    xs, bm, cm = jnp.split(xbc, [SSD_DI, SSD_DI + SSD_G * SSD_N], axis=-1)
    xs = xs.reshape(bn, seq_len, SSD_H, SSD_P)
    bm = bm.reshape(bn, seq_len, SSD_G, SSD_N)
    cm = cm.reshape(bn, seq_len, SSD_G, SSD_N)
    dt = jax.nn.softplus(dt_raw.reshape(bn, seq_len, 2, SSD_H) + dt_bias)
    a_neg = -jnp.exp(a_log)
    yf, hf = ssd_scan(xs, dt[:, :, 0], a_neg[0], bm, cm, state[0])
    yb, hb = ssd_scan(flip(xs), flip(dt[:, :, 1]), a_neg[1], flip(bm), flip(cm), state[1])
    y = yf + flip(yb) + d[:, None] * xs
    y = y.reshape(bn, seq_len, SSD_DI) * jax.nn.silu(z)
    return rmsnorm(y, norm_g), (hf, hb)


def gated_delta_rule(q, k, v, beta, g, s0):
    bn, seq_len, nh, dk = q.shape
    dv = v.shape[-1]
    qn = DN_CHUNK
    nc = seq_len // qn

    def chunks(t):
        return jnp.swapaxes(t.reshape((bn, nc, qn) + t.shape[2:]), 2, 3)
    qc, kc, vc, bc, gc = (chunks(t) for t in (q, k, v, beta, g))
    gcum = jnp.cumsum(gc, axis=-1)
    idx = jnp.arange(qn)
    incl = idx[:, None] >= idx[None, :]
    strict = idx[:, None] > idx[None, :]
    dmat = jnp.exp(jnp.where(incl, gcum[..., :, None] - gcum[..., None, :], -jnp.inf))
    kb = kc * bc[..., None]
    m = jnp.where(strict, jnp.einsum('bnhid,bnhjd->bnhij', kb, kc) * dmat, 0.0)
    rhs = jnp.concatenate([vc * bc[..., None], kb * jnp.exp(gcum)[..., None]], axis=-1)
    sol = lax.linalg.triangular_solve(m + jnp.eye(qn, dtype=m.dtype), rhs,
                                      left_side=True, lower=True, unit_diagonal=True)
    u, w = sol[..., :dv], sol[..., dv:]
    qk = jnp.einsum('bnhid,bnhjd->bnhij', qc, kc) * dmat
    q_dec = qc * jnp.exp(gcum)[..., None]
    k_tail = kc * jnp.exp(gcum[..., -1:] - gcum)[..., None]
    tot = jnp.exp(gcum[..., -1])

    def step(s, inp):
        u_c, w_c, qk_c, qd_c, kt_c, tot_c = inp
        v_new = u_c - jnp.einsum('bhqk,bhkv->bhqv', w_c, s)
        o = jnp.einsum('bhqk,bhkv->bhqv', qd_c, s) + jnp.einsum('bhij,bhjv->bhiv', qk_c, v_new)
        s = s * tot_c[..., None, None] + jnp.einsum('bhqk,bhqv->bhkv', kt_c, v_new)
        return s, o
    xs = tuple(jnp.moveaxis(t, 1, 0) for t in (u, w, qk, q_dec, k_tail, tot))
    s_last, o = lax.scan(step, s0, xs)
    o = jnp.swapaxes(jnp.moveaxis(o, 0, 1), 2, 3).reshape(bn, seq_len, nh, dv)
    return o, s_last


def deltanet_stream(qkv, z, a_raw, b_raw, state, conv_w, conv_b, dt_bias, a_log, norm_g):
    bn, seq_len, _ = z.shape
    qkv = jax.nn.silu(dwconv(qkv, conv_w, conv_b))
    q, k, v = jnp.split(qkv, [DN_H * DN_DK, 2 * DN_H * DN_DK], axis=-1)
    q = l2norm(q.reshape(bn, seq_len, DN_H, DN_DK)) * (DN_DK ** -0.5)
    k = l2norm(k.reshape(bn, seq_len, DN_H, DN_DK))
    v = v.reshape(bn, seq_len, DN_H, DN_DV)
    beta = jax.nn.sigmoid(b_raw.reshape(bn, seq_len, 2, DN_H))
    g = -jnp.exp(a_log) * jax.nn.softplus(a_raw.reshape(bn, seq_len, 2, DN_H) + dt_bias)
    of, sf = gated_delta_rule(q, k, v, beta[:, :, 0], g[:, :, 0], state[0])
    ob, sb = gated_delta_rule(flip(q), flip(k), flip(v), flip(beta[:, :, 1]), flip(g[:, :, 1]), state[1])
    o = rmsnorm(of + flip(ob), norm_g)
    o = o.reshape(bn, seq_len, DN_H * DN_DV) * jax.nn.silu(z)
    return o, (sf, sb)


def rglru_stream(xr, gate, state, conv_w, conv_b, wa, ba, wx, bx, lam):
    bn, seq_len, _ = xr.shape
    xh = dwconv(xr, conv_w, conv_b).reshape(bn, seq_len, RG_H, RG_BS)

    def coeffs(dr):
        r = jax.nn.sigmoid(jnp.einsum('blhi,hij->blhj', xh, wa[dr]) + ba[dr])
        i = jax.nn.sigmoid(jnp.einsum('blhi,hij->blhj', xh, wx[dr]) + bx[dr])
        log_a = -RG_C * r * jax.nn.softplus(-lam[dr])
        return jnp.exp(log_a), jnp.sqrt(-jnp.expm1(2.0 * log_a)) * (i * xh)
    af, bf = coeffs(0)
    hf = linear_scan(af, bf, state[0])
    ab, bb = coeffs(1)
    hb = linear_scan(flip(ab), flip(bb), state[1])
    y = (hf + flip(hb)).reshape(bn, seq_len, RG_W) * jax.nn.gelu(gate)
    return y, (hf[:, -1], hb[:, -1])


def even_mixer(hx, hc, w_in, w_out, s5_params, ssd_params, need_ctx):
    bn = hx.shape[0]

    def stream(h, st_s5, st_ssd):
        p = mm3(h, w_in)
        u, z, xbc, dt_raw = jnp.split(p, EV_CUTS, axis=-1)
        ya, st_a = s5_stream(u, st_s5, *s5_params)
        yb, st_b = ssd_stream(z, xbc, dt_raw, st_ssd, *ssd_params)
        return jnp.concatenate([ya, yb], axis=-1), st_a, st_b
    zs5 = jnp.zeros((bn, S5_LANES), F32)
    zssd = jnp.zeros((bn, SSD_H, SSD_P, SSD_N), F32)
    yc, st_a, st_b = stream(hc, (zs5, zs5, zs5, zs5), (zssd, zssd))
    yx, _, _ = stream(hx, st_a, st_b)
    return mm3(yx, w_out), (mm3(yc, w_out) if need_ctx else None)


def odd_mixer(hx, hc, w_in, w_out, dn_params, rg_params, need_ctx):
    bn = hx.shape[0]

    def stream(h, st_dn, st_rg):
        p = mm3(h, w_in)
        qkv, z, a_raw, b_raw, xr, gate = jnp.split(p, OD_CUTS, axis=-1)
        yd, st_d = deltanet_stream(qkv, z, a_raw, b_raw, st_dn, *dn_params)
        yr, st_r = rglru_stream(xr, gate, st_rg, *rg_params)
        return jnp.concatenate([yd, yr], axis=-1), st_d, st_r
    zdn = jnp.zeros((bn, DN_H, DN_DK, DN_DV), F32)
    zrg = jnp.zeros((bn, RG_H, RG_BS), F32)
    yc, st_d, st_r = stream(hc, (zdn, zdn), (zrg, zrg))
    yx, _, _ = stream(hx, st_d, st_r)
    return mm3(yx, w_out), (mm3(yc, w_out) if need_ctx else None)


PEER_TM = 512
PEER_EB = 512
PEER_RT = 16
PEER_SEL = PEER_H * PEER_TOPK


def _peer_body(flags_ref, ht_ref, u_ref, vt_ref, s1_ref, s2_ref, e1_ref, e2_ref, tau_ref,
               pidx_ref, ridx_ref, gate_ref, o_ref, st_s, g_s, *, tm, eb):
    i = pl.program_id(0)
    j = pl.program_id(1)
    npk = eb // PEER_NK
    st_s[...] = jnp.dot(u_ref[...], ht_ref[...], preferred_element_type=F32)

    @pl.when(flags_ref[i] == 0)
    def _():
        for lg in range(tm // V7X_LANES):
            ls = slice(V7X_LANES * lg, V7X_LANES * (lg + 1))
            t_rows = [tau_ref[pl.ds(h, 1), ls] for h in range(PEER_H)]
            for pk in range(npk):
                p = j * npk + pk
                a_tile = s1_ref[p, :, ls]
                z_tile = e1_ref[p, :, ls]
                a_rows = [a_tile[h:h + 1, :] for h in range(PEER_H)]
                z_rows = [z_tile[h:h + 1, :] for h in range(PEER_H)]

                def rbody(rt, carry, ls=ls, pk=pk, a_rows=a_rows, z_rows=z_rows, t_rows=t_rows):
                    r0 = pl.multiple_of(rt * PEER_RT, PEER_RT)
                    acc = jnp.zeros((PEER_RT, V7X_LANES), F32)
                    for h in range(PEER_H):
                        s2 = s2_ref[h, pl.ds(r0, PEER_RT), ls]
                        e2 = e2_ref[h, pl.ds(r0, PEER_RT), ls]
                        acc = acc + jnp.where(a_rows[h] + s2 >= t_rows[h], z_rows[h] * e2, 0.0)
                    row0 = pl.multiple_of(pk * PEER_NK + r0, PEER_RT)
                    act = jax.nn.gelu(st_s[pl.ds(row0, PEER_RT), ls])
                    g_s[pl.ds(row0, PEER_RT), ls] = (acc * act).astype(BF16)
                    return carry

                lax.fori_loop(0, PEER_NK // PEER_RT, rbody, 0)

    @pl.when(flags_ref[i] != 0)
    def _():
        riota = lax.broadcasted_iota(jnp.int32, (PEER_NK, V7X_LANES), 0)
        for lg in range(tm // V7X_LANES):
            ls = slice(V7X_LANES * lg, V7X_LANES * (lg + 1))
            for pk in range(npk):
                p = j * npk + pk

                def kbody(k8, w, ls=ls, p=p):
                    k0 = pl.multiple_of(k8 * V7X_SUBLANES, V7X_SUBLANES)
                    ptile = pidx_ref[pl.ds(k0, V7X_SUBLANES), ls]
                    rtile = ridx_ref[pl.ds(k0, V7X_SUBLANES), ls]
                    gtile = gate_ref[pl.ds(k0, V7X_SUBLANES), ls]
                    ctile = jnp.where(ptile == p, gtile, 0.0)
                    for k in range(V7X_SUBLANES):
                        w = w + jnp.where(rtile[k:k + 1, :] == riota, ctile[k:k + 1, :], 0.0)
                    return w

                w = lax.fori_loop(0, PEER_SEL // V7X_SUBLANES, kbody,
                                  jnp.zeros((PEER_NK, V7X_LANES), F32))
                rows = slice(pk * PEER_NK, (pk + 1) * PEER_NK)
                g_s[rows, ls] = (w * jax.nn.gelu(st_s[rows, ls])).astype(BF16)

    contrib = jnp.dot(vt_ref[...], g_s[...], preferred_element_type=F32)

    @pl.when(j == 0)
    def _():
        o_ref[...] = contrib

    @pl.when(j != 0)
    def _():
        o_ref[...] += contrib


def peer_experts(flags, ht, u_bf, vt_bf, s1t, s2t, e1t, e2t, taut, pidx, ridx, gate, tm):
    dm, t = ht.shape
    eb = PEER_EB
    head3 = pl.BlockSpec((PEER_H, PEER_NK, tm), lambda i, j, f: (0, 0, i))
    key3 = pl.BlockSpec((PEER_NK, PEER_H, tm), lambda i, j, f: (0, 0, i))
    sel2 = pl.BlockSpec((PEER_SEL, tm), lambda i, j, f: (0, i))
    grid_spec = pltpu.PrefetchScalarGridSpec(
        num_scalar_prefetch=1,
        grid=(t // tm, PEER_E // eb),
        in_specs=[pl.BlockSpec((dm, tm), lambda i, j, f: (0, i)),
                  pl.BlockSpec((eb, dm), lambda i, j, f: (j, 0)),
                  pl.BlockSpec((dm, eb), lambda i, j, f: (0, j)),
                  key3, head3, key3, head3,
                  pl.BlockSpec((PEER_H, tm), lambda i, j, f: (0, i)),
                  sel2, sel2, sel2],
        out_specs=pl.BlockSpec((dm, tm), lambda i, j, f: (0, i)),
        scratch_shapes=[pltpu.VMEM((eb, tm), F32), pltpu.VMEM((eb, tm), BF16)])
    return pl.pallas_call(
        functools.partial(_peer_body, tm=tm, eb=eb),
        grid_spec=grid_spec,
        out_shape=jax.ShapeDtypeStruct((dm, t), F32),
        compiler_params=pltpu.CompilerParams(
            dimension_semantics=("parallel", "arbitrary"),
            vmem_limit_bytes=V7X_VMEM_LIMIT),
        name="peer_experts",
    )(flags, ht, u_bf, vt_bf, s1t, s2t, e1t, e2t, taut, pidx, ridx, gate)


def _topk_rows(x, k):
    n = x.shape[0]
    iota = lax.broadcasted_iota(jnp.int32, x.shape, 0)
    vals, idxs = [], []
    for r in range(k):
        m = jnp.max(x, axis=0, keepdims=True)
        idx = jnp.min(jnp.where(x == m, iota, n), axis=0, keepdims=True)
        vals.append(m)
        idxs.append(idx)
        if r < k - 1:
            x = jnp.where(iota == idx, -jnp.inf, x)
    return vals, idxs


def _peer_select_body(q_ref, k1_ref, k2_ref, s1_ref, s2_ref, e1_ref, e2_ref, tau_ref, tie_ref,
                      pidx_ref, ridx_ref, gate_ref):
    qb = q_ref[...].astype(BF16)
    dn = (((1,), (1,)), ((), ()))
    s1 = lax.dot_general(k1_ref[0], qb[:, :PEER_HALF], dn, preferred_element_type=F32)
    s2 = lax.dot_general(k2_ref[0], qb[:, PEER_HALF:], dn, preferred_element_type=F32)
    t1, i1 = _topk_rows(s1, PEER_TOPK + 1)
    t2, i2 = _topk_rows(s2, PEER_TOPK + 1)
    t2blk = jnp.concatenate(t2[:PEER_TOPK], axis=0)
    cand = jnp.concatenate([t1[a] + t2blk for a in range(PEER_TOPK)], axis=0)
    top, ci = _topk_rows(cand, PEER_TOPK)
    topb = jnp.concatenate(top, axis=0)
    cib = jnp.concatenate(ci, axis=0)
    ca = cib // PEER_TOPK
    cb = cib % PEER_TOPK
    pid = jnp.zeros_like(cib)
    rid = jnp.zeros_like(cib)
    for a in range(PEER_TOPK):
        pid = pid + jnp.where(ca == a, i1[a], 0)
        rid = rid + jnp.where(cb == a, i2[a], 0)
    ex = jnp.exp(topb - top[0])
    zsum = jnp.sum(ex, axis=0, keepdims=True)
    tau = top[PEER_TOPK - 1]
    cnt = jnp.sum((cand >= tau).astype(jnp.int32), axis=0, keepdims=True)
    tie = ((cnt != PEER_TOPK) | (t1[PEER_TOPK] + t2[0] >= tau) | (t1[0] + t2[PEER_TOPK] >= tau))
    s1_ref[0] = s1
    s2_ref[0] = s2
    e1_ref[0] = jnp.exp(s1 - t1[0]) / zsum
    e2_ref[0] = jnp.exp(s2 - t2[0])
    tau_ref[0] = tau
    tie_ref[0] = tie.astype(jnp.int32)
    pidx_ref[...] = pid
    ridx_ref[...] = rid
    gate_ref[...] = ex / zsum


def peer_select(q, k1_bf, k2_bf):
    t = q.shape[0]
    tl = V7X_LANES
    head3 = pl.BlockSpec((1, PEER_NK, tl), lambda i, h: (h, 0, i))
    row3 = pl.BlockSpec((1, 1, tl), lambda i, h: (h, 0, i))
    sel2 = pl.BlockSpec((PEER_TOPK, tl), lambda i, h: (h, i))
    key_spec = pl.BlockSpec((1, PEER_NK, PEER_HALF), lambda i, h: (h, 0, 0))
    f3 = jax.ShapeDtypeStruct((PEER_H, PEER_NK, t), F32)
    return pl.pallas_call(
        _peer_select_body,
        grid=(t // tl, PEER_H),
        in_specs=[pl.BlockSpec((tl, PEER_DK), lambda i, h: (i, h)), key_spec, key_spec],
        out_specs=[head3, head3, head3, head3, row3, row3, sel2, sel2, sel2],
        out_shape=[f3, f3, f3, f3,
                   jax.ShapeDtypeStruct((PEER_H, 1, t), F32),
                   jax.ShapeDtypeStruct((PEER_H, 1, t), jnp.int32),
                   jax.ShapeDtypeStruct((PEER_SEL, t), jnp.int32),
                   jax.ShapeDtypeStruct((PEER_SEL, t), jnp.int32),
                   jax.ShapeDtypeStruct((PEER_SEL, t), F32)],
        compiler_params=pltpu.CompilerParams(
            dimension_semantics=("parallel", "arbitrary"),
            vmem_limit_bytes=V7X_VMEM_LIMIT),
        name="peer_select",
    )(q, k1_bf, k2_bf)


def peer(h, w_q, k1, k2, u_bf, vt_bf, tm=PEER_TM):
    bn, seq_len, dm = h.shape
    t = bn * seq_len
    q = mm(h.reshape(t, dm), w_q)
    s1t, s2t, e1t, e2t, taut, tiet, pidx, ridx, gate = peer_select(q, k1.astype(BF16), k2.astype(BF16))
    flags = jnp.any(tiet.reshape(PEER_H, t // tm, tm) != 0, axis=(0, 2)).astype(jnp.int32)
    outt = peer_experts(
        flags, h.reshape(t, dm).T.astype(BF16), u_bf, vt_bf,
        jnp.transpose(s1t, (1, 0, 2)), s2t, jnp.transpose(e1t, (1, 0, 2)), e2t, taut.reshape(PEER_H, t),
        pidx, ridx, gate, tm)
    return outt.T.reshape(bn, seq_len, dm)


def kernel(x, c, ctx, c_ctx, ada_w, ada_b, norm1_g, norm2_g, final_g, ev_w_in, ev_w_out, s5_lam_re, s5_lam_im, s5_log_dt, s5_b_re, s5_b_im, s5_c_re, s5_c_im, s5_d, s5_glu_w, s5_glu_b, ssd_conv_w, ssd_conv_b, ssd_dt_bias, ssd_a_log, ssd_d, ssd_norm_g, od_w_in, od_w_out, dn_conv_w, dn_conv_b, dn_dt_bias, dn_a_log, dn_norm_g, rg_conv_w, rg_conv_b, rg_wa, rg_ba, rg_wx, rg_bx, rg_lam, peer_wq, peer_k1, peer_k2, peer_u, peer_v):
    bn, seq_len, _ = x.shape
    rows = seq_len // GRID_W
    sc = jax.nn.silu(c)
    scc = jax.nn.silu(c_ctx)
    for layer in range(DEPTH):
        j = layer // 2
        need_ctx = layer < DEPTH - 1
        mx = (sc @ ada_w[layer] + ada_b[layer]).reshape(bn, 6, 1, D_MODEL)
        mc = (scc @ ada_w[layer] + ada_b[layer]).reshape(6, D_MODEL)
        hx = modulate(rmsnorm(x, norm1_g[layer]), mx[:, 0], mx[:, 1])
        hc = modulate(rmsnorm(ctx, norm1_g[layer]), mc[0], mc[1])
        if layer % 2 == 0:
            s5_params = (s5_lam_re[j], s5_lam_im[j], s5_log_dt[j], s5_b_re[j], s5_b_im[j],
                         s5_c_re[j], s5_c_im[j], s5_d[j], s5_glu_w[j], s5_glu_b[j])
            ssd_params = (ssd_conv_w[j], ssd_conv_b[j], ssd_dt_bias[j], ssd_a_log[j],
                          ssd_d[j], ssd_norm_g[j])
            ox, oc = even_mixer(hx, hc, ev_w_in[j], ev_w_out[j], s5_params, ssd_params, need_ctx)
        else:
            dn_params = (dn_conv_w[j], dn_conv_b[j], dn_dt_bias[j], dn_a_log[j], dn_norm_g[j])
            rg_params = (rg_conv_w[j], rg_conv_b[j], rg_wa[j], rg_ba[j], rg_wx[j], rg_bx[j], rg_lam[j])
            ox, oc = odd_mixer(to_col_major(hx, rows), hc, od_w_in[j], od_w_out[j],
                               dn_params, rg_params, need_ctx)
            ox = to_row_major(ox, rows)
        x = x + mx[:, 2] * ox
        hx = modulate(rmsnorm(x, norm2_g[layer]), mx[:, 3], mx[:, 4])
        u_bf = peer_u[layer].astype(BF16)
        vt_bf = peer_v[layer].T.astype(BF16)
        x = x + mx[:, 5] * peer(hx, peer_wq[layer], peer_k1[layer], peer_k2[layer], u_bf, vt_bf)
        if need_ctx:
            ctx = ctx + mc[2] * oc
            hc = modulate(rmsnorm(ctx, norm2_g[layer]), mc[3], mc[4])
            ctx = ctx + mc[5] * peer(hc, peer_wq[layer], peer_k1[layer], peer_k2[layer], u_bf, vt_bf)
    return rmsnorm(x, final_g)
```

```python
import functools
import math

import jax
import jax.numpy as jnp
from jax import lax
from jax.experimental import pallas as pl
from jax.experimental.pallas import tpu as pltpu

D_MODEL = 2048
DEPTH = 2
GRID_W = 64
EPS = 1e-6
CONV_W = 4
CONV_PAD = (CONV_W // 2, CONV_W - 1 - CONV_W // 2)

S5_W = D_MODEL // 2
S5_CH = 16
S5_G = S5_W // S5_CH
S5_N = 64
SSD_P = 64
SSD_H = D_MODEL // SSD_P
SSD_DI = SSD_H * SSD_P
SSD_N = 128
SSD_G = 4
SSD_CHUNK = 128
SSD_CONV_CH = SSD_DI + 2 * SSD_G * SSD_N
EV_IN = S5_W + SSD_DI + SSD_CONV_CH + 2 * SSD_H
EV_MIX = S5_W + SSD_DI
DN_DK = 128
DN_DV = 128
DN_H = D_MODEL // DN_DK
DN_CHUNK = 64
DN_QKV = DN_H * (2 * DN_DK + DN_DV)
RG_W = D_MODEL // 2
RG_H = 8
RG_BS = RG_W // RG_H
RG_C = 8.0
OD_IN = DN_QKV + DN_H * DN_DV + 4 * DN_H + 2 * RG_W
OD_MIX = DN_H * DN_DV + RG_W
OD_CUTS = [DN_QKV,
           DN_QKV + DN_H * DN_DV,
           DN_QKV + DN_H * DN_DV + 2 * DN_H,
           DN_QKV + DN_H * DN_DV + 4 * DN_H,
           DN_QKV + DN_H * DN_DV + 4 * DN_H + RG_W]
EV_CUTS = [S5_W, S5_W + SSD_DI, S5_W + SSD_DI + SSD_CONV_CH]
PEER_H = 8
PEER_DK = 256
PEER_HALF = PEER_DK // 2
PEER_NK = 128
PEER_E = PEER_NK * PEER_NK
PEER_TOPK = 16
PEER_BLOCK = 128

F32 = jnp.float32
BF16 = jnp.bfloat16

V7X_LANES = 128
V7X_SUBLANES = 8
V7X_VMEM_LIMIT = 56 * 1024 * 1024


def _mm_body(x_ref, w_ref, o_ref):
    o_ref[...] = jnp.dot(x_ref[...], w_ref[...], preferred_element_type=F32).astype(o_ref.dtype)


def _pick(n, cands):
    for c in cands:
        if n % c == 0:
            return c
    raise ValueError(f"no tile for {n}")


def mm(x, w, out_dtype=F32):
    m, k = x.shape
    n = w.shape[1]
    n_pad = -n % V7X_LANES
    if n_pad:
        w = jnp.pad(w, ((0, 0), (0, n_pad)))
    np_ = n + n_pad
    tm = _pick(m, (1024, 512, 256, 128, 64, 32, 16, 8))
    tn = _pick(np_, (512, 256, 128))
    out = pl.pallas_call(
        _mm_body,
        grid=(m // tm, np_ // tn),
        in_specs=[pl.BlockSpec((tm, k), lambda i, j: (i, 0)),
                  pl.BlockSpec((k, tn), lambda i, j: (0, j))],
        out_specs=pl.BlockSpec((tm, tn), lambda i, j: (i, j)),
        out_shape=jax.ShapeDtypeStruct((m, np_), out_dtype),
        compiler_params=pltpu.CompilerParams(
            dimension_semantics=("parallel", "arbitrary"),
            vmem_limit_bytes=V7X_VMEM_LIMIT),
        name="mm",
    )(x.astype(BF16), w.astype(BF16))
    return out[:, :n] if n_pad else out


def mm3(h, w):
    b, l, d = h.shape
    return mm(h.reshape(b * l, d), w).reshape(b, l, w.shape[1])


def rmsnorm(x, g):
    xf = x.astype(F32)
    y = xf * lax.rsqrt(jnp.mean(xf * xf, axis=-1, keepdims=True) + EPS)
    return (y * g.astype(F32)).astype(x.dtype)


def l2norm(x):
    return x * lax.rsqrt(jnp.sum(x * x, axis=-1, keepdims=True) + EPS)


def modulate(h, shift, scale):
    return h * (1 + scale) + shift


def dwconv(x, w, b):
    y = lax.conv_general_dilated(x, w[:, None, :], (1,), [CONV_PAD],
                                 dimension_numbers=('NWC', 'WIO', 'NWC'),
                                 feature_group_count=x.shape[-1])
    return y + b


def flip(t):
    return jnp.flip(t, axis=1)


def to_col_major(x, rows):
    b, l, d = x.shape
    return x.reshape(b, rows, GRID_W, d).transpose(0, 2, 1, 3).reshape(b, l, d)


def to_row_major(x, rows):
    b, l, d = x.shape
    return x.reshape(b, GRID_W, rows, d).transpose(0, 2, 1, 3).reshape(b, l, d)


def linear_scan(a, b, h0):
    b = b.at[:, 0].add(a[:, 0] * h0)

    def comb(e1, e2):
        return e1[0] * e2[0], e2[0] * e1[1] + e2[1]
    return lax.associative_scan(comb, (a, b), axis=1)[1]


S5_LANES = S5_G * S5_N
S5_TILE = 8
S5_SHIFTS = (1, 2, 4)
S5_BG = 8
S5_CG = 16
S5_CHUNK = 1024


def _s5_scan_body(u_ref, h0r_ref, h0i_ref, bdr_ref, bdi_ref, cdr_ref, cdi_ref, coef_ref,
                  y_ref, hfr_ref, hfi_ref, xr_s, xi_s, cr_s, ci_s, *, rev, tt):
    @pl.when(pl.program_id(1) == 0)
    def _():
        cr_s[...] = h0r_ref[0]
        ci_s[...] = h0i_ref[0]

    ub = u_ref[0].astype(BF16)
    kin = S5_BG * S5_CH
    kout = S5_BG * S5_N
    for j in range(S5_G // S5_BG):
        uj = ub[:, kin * j:kin * (j + 1)]
        xr_s[:, kout * j:kout * (j + 1)] = jnp.dot(uj, bdr_ref[j], preferred_element_type=F32)
        xi_s[:, kout * j:kout * (j + 1)] = jnp.dot(uj, bdi_ref[j], preferred_element_type=F32)

    nt = tt // S5_TILE
    carry_row = 0 if rev else S5_TILE - 1

    def tile(kk, carry):
        k = (nt - 1 - kk) if rev else kk
        r0 = pl.multiple_of(k * S5_TILE, S5_TILE)
        for c in range(S5_LANES // S5_CHUNK):
            sl = slice(S5_CHUNK * c, S5_CHUNK * (c + 1))
            xr = xr_s[pl.ds(r0, S5_TILE), sl]
            xi = xi_s[pl.ds(r0, S5_TILE), sl]
            for idx, sh in enumerate(S5_SHIFTS):
                pr = coef_ref[2 * idx, :, sl]
                pim = coef_ref[2 * idx + 1, :, sl]
                rs = (S5_TILE - sh) if rev else sh
                sr = pltpu.roll(xr, rs, 0)
                si = pltpu.roll(xi, rs, 0)
                xr, xi = xr + (pr * sr - pim * si), xi + (pr * si + pim * sr)
            pr = coef_ref[2 * len(S5_SHIFTS), :, sl]
            pim = coef_ref[2 * len(S5_SHIFTS) + 1, :, sl]
            hr = cr_s[:, sl]
            hi = ci_s[:, sl]
            xr, xi = xr + (pr * hr - pim * hi), xi + (pr * hi + pim * hr)
            xr_s[pl.ds(r0, S5_TILE), sl] = xr
            xi_s[pl.ds(r0, S5_TILE), sl] = xi
            cr_s[:, sl] = xr[carry_row:carry_row + 1, :]
            ci_s[:, sl] = xi[carry_row:carry_row + 1, :]
        return carry

    lax.fori_loop(0, nt, tile, 0)

    kc = S5_CG * S5_N
    nc = S5_CG * S5_CH
    for j in range(S5_G // S5_CG):
        sr = xr_s[:, kc * j:kc * (j + 1)].astype(BF16)
        si = xi_s[:, kc * j:kc * (j + 1)].astype(BF16)
        y_ref[0, :, nc * j:nc * (j + 1)] = (
            jnp.dot(sr, cdr_ref[j], preferred_element_type=F32)
            - jnp.dot(si, cdi_ref[j], preferred_element_type=F32))
    hfr_ref[0] = cr_s[...]
    hfi_ref[0] = ci_s[...]


def _s5_direction_params(lam_re, lam_im, log_dt, b_re, b_im, c_re, c_im, rev):
    dt = jnp.exp(log_dt)[:, None]
    mag = jnp.exp(lam_re * dt)
    ar, ai = mag * jnp.cos(lam_im * dt), mag * jnp.sin(lam_im * dt)
    den = lam_re * lam_re + lam_im * lam_im
    cr = ((ar - 1) * lam_re + ai * lam_im) / den
    ci = (ai * lam_re - (ar - 1) * lam_im) / den
    bb_re = cr[..., None] * b_re - ci[..., None] * b_im
    bb_im = cr[..., None] * b_im + ci[..., None] * b_re

    def bdiag_in(bb):
        t = bb.reshape(S5_G // S5_BG, S5_BG, S5_N, S5_CH)
        eye = jnp.eye(S5_BG, dtype=F32)
        return jnp.einsum('jgnc,gh->jgchn', t, eye).reshape(
            S5_G // S5_BG, S5_BG * S5_CH, S5_BG * S5_N).astype(BF16)

    def bdiag_out(cc):
        t = cc.reshape(S5_G // S5_CG, S5_CG, S5_CH, S5_N)
        eye = jnp.eye(S5_CG, dtype=F32)
        return jnp.einsum('jgcn,gh->jgnhc', t, eye).reshape(
            S5_G // S5_CG, S5_CG * S5_N, S5_CG * S5_CH).astype(BF16)

    a_r, a_i = ar.reshape(-1), ai.reshape(-1)
    pw = [(a_r, a_i)]
    for _ in range(S5_TILE - 1):
        pr, pim = pw[-1]
        pw.append((pr * a_r - pim * a_i, pr * a_i + pim * a_r))
    rows = jnp.arange(S5_TILE)[:, None]
    coef = []
    for sh in S5_SHIFTS:
        valid = (rows <= S5_TILE - 1 - sh) if rev else (rows >= sh)
        coef.append(jnp.where(valid, pw[sh - 1][0][None, :], 0.0))
        coef.append(jnp.where(valid, pw[sh - 1][1][None, :], 0.0))
    order = list(range(S5_TILE - 1, -1, -1)) if rev else list(range(S5_TILE))
    coef.append(jnp.stack([pw[k][0] for k in order]))
    coef.append(jnp.stack([pw[k][1] for k in order]))
    return (bdiag_in(bb_re), bdiag_in(bb_im), bdiag_out(c_re), bdiag_out(c_im),
            jnp.stack(coef).astype(F32))


def s5_scan(u, h0r, h0i, dparams, rev):
    bn, seq_len, _ = u.shape
    bdr, bdi, cdr, cdi, coef = dparams
    tt = min(256, seq_len)
    nblk = seq_len // tt
    tmap = (lambda b, i: (b, nblk - 1 - i, 0)) if rev else (lambda b, i: (b, i, 0))
    const3 = lambda b, i: (0, 0, 0)
    y, hfr, hfi = pl.pallas_call(
        functools.partial(_s5_scan_body, rev=rev, tt=tt),
        grid=(bn, nblk),
        in_specs=[pl.BlockSpec((1, tt, S5_W), tmap),
                  pl.BlockSpec((1, 1, S5_LANES), lambda b, i: (b, 0, 0)),
                  pl.BlockSpec((1, 1, S5_LANES), lambda b, i: (b, 0, 0)),
                  pl.BlockSpec(bdr.shape, const3),
                  pl.BlockSpec(bdi.shape, const3),
                  pl.BlockSpec(cdr.shape, const3),
                  pl.BlockSpec(cdi.shape, const3),
                  pl.BlockSpec(coef.shape, const3)],
        out_specs=[pl.BlockSpec((1, tt, S5_W), tmap),
                   pl.BlockSpec((1, 1, S5_LANES), lambda b, i: (b, 0, 0)),
                   pl.BlockSpec((1, 1, S5_LANES), lambda b, i: (b, 0, 0))],
        out_shape=[jax.ShapeDtypeStruct((bn, seq_len, S5_W), F32),
                   jax.ShapeDtypeStruct((bn, 1, S5_LANES), F32),
                   jax.ShapeDtypeStruct((bn, 1, S5_LANES), F32)],
        scratch_shapes=[pltpu.VMEM((tt, S5_LANES), F32), pltpu.VMEM((tt, S5_LANES), F32),
                        pltpu.VMEM((1, S5_LANES), F32), pltpu.VMEM((1, S5_LANES), F32)],
        compiler_params=pltpu.CompilerParams(
            dimension_semantics=("parallel", "arbitrary"),
            vmem_limit_bytes=V7X_VMEM_LIMIT),
        name="s5_scan_bwd" if rev else "s5_scan_fwd",
    )(u, h0r.reshape(bn, 1, S5_LANES), h0i.reshape(bn, 1, S5_LANES), bdr, bdi, cdr, cdi, coef)
    return y, hfr.reshape(bn, S5_LANES), hfi.reshape(bn, S5_LANES)


def _s5_glu_body(yf_ref, yb_ref, u_ref, d_ref, w_ref, b_ref, o_ref):
    y = yf_ref[...] + yb_ref[...] + d_ref[...] * u_ref[...]
    g = jax.nn.gelu(y)
    z = jnp.dot(g.astype(BF16), w_ref[...], preferred_element_type=F32) + b_ref[...]
    o_ref[...] = (g * jax.nn.sigmoid(z)).astype(o_ref.dtype)


def s5_glu(yf, yb, u, d, glu_w, glu_b):
    m = yf.shape[0]
    tm = _pick(m, (512, 256, 128, 64, 32, 16, 8))
    row = pl.BlockSpec((tm, S5_W), lambda i: (i, 0))
    vec = pl.BlockSpec((1, S5_W), lambda i: (0, 0))
    return pl.pallas_call(
        _s5_glu_body,
        grid=(m // tm,),
        in_specs=[row, row, row, vec, pl.BlockSpec((S5_W, S5_W), lambda i: (0, 0)), vec],
        out_specs=row,
        out_shape=jax.ShapeDtypeStruct((m, S5_W), F32),
        compiler_params=pltpu.CompilerParams(
            dimension_semantics=("parallel",), vmem_limit_bytes=V7X_VMEM_LIMIT),
        name="s5_glu",
    )(yf, yb, u, d.reshape(1, S5_W), glu_w.astype(BF16), glu_b.reshape(1, S5_W))


def s5_stream(u, state, lam_re, lam_im, log_dt, b_re, b_im, c_re, c_im, d, glu_w, glu_b):
    bn, seq_len, _ = u.shape
    pf = _s5_direction_params(lam_re[0], lam_im[0], log_dt[0], b_re[0], b_im[0], c_re[0], c_im[0], False)
    pb = _s5_direction_params(lam_re[1], lam_im[1], log_dt[1], b_re[1], b_im[1], c_re[1], c_im[1], True)
    yf, fr, fi = s5_scan(u, state[0], state[1], pf, False)
    yb, br, bi = s5_scan(u, state[2], state[3], pb, True)
    m = bn * seq_len
    out = s5_glu(yf.reshape(m, S5_W), yb.reshape(m, S5_W), u.reshape(m, S5_W), d, glu_w, glu_b)
    return out.reshape(bn, seq_len, S5_W), (fr, fi, br, bi)


def ssd_scan(x, dt, a_neg, bm, cm, h0):
    bn, seq_len, nh, hp = x.shape
    ng, ns = bm.shape[2], bm.shape[3]
    nr = nh // ng
    q = SSD_CHUNK
    nc = seq_len // q
    xdt = (x * dt[..., None]).reshape(bn, nc, q, ng, nr, hp)
    cum = jnp.cumsum((dt * a_neg).reshape(bn, nc, q, ng, nr), axis=2)
    bm = bm.reshape(bn, nc, q, ng, ns)
    cm = cm.reshape(bn, nc, q, ng, ns)
    idx = jnp.arange(q)
    incl = idx[:, None] >= idx[None, :]
    cum_t = jnp.moveaxis(cum, 2, -1)
    seg = jnp.exp(jnp.where(incl, cum_t[..., :, None] - cum_t[..., None, :], -jnp.inf))
    cb = jnp.einsum('bclgn,bcsgn->bcgls', cm, bm)
    y_diag = jnp.einsum('bcgrls,bcsgrp->bclgrp', seg * cb[:, :, :, None], xdt)

    def step(h, inp):
        c_c, b_c, xdt_c, cum_c = inp
        y_off = jnp.einsum('blgn,bgrpn->blgrp', c_c, h) * jnp.exp(cum_c)[..., None]
        dte = jnp.exp(cum_c[:, -1:] - cum_c)
        st = jnp.einsum('blgn,blgrp->bgrpn', b_c, xdt_c * dte[..., None])
        h = h * jnp.exp(cum_c[:, -1])[..., None, None] + st
        return h, y_off
    xs = tuple(jnp.moveaxis(t, 1, 0) for t in (cm, bm, xdt, cum))
    h_last, y_off = lax.scan(step, h0.reshape(bn, ng, nr, hp, ns), xs)
    y = y_diag + jnp.moveaxis(y_off, 0, 1)
    return y.reshape(bn, seq_len, nh, hp), h_last.reshape(bn, nh, hp, ns)


def ssd_stream(z, xbc, dt_raw, state, conv_w, conv_b, dt_bias, a_log, d, norm_g):
    bn, seq_len, _ = z.shape
    xbc = jax.nn.silu(dwconv(xbc, conv_w, conv_b))
    xs, bm, cm = jnp.split(xbc, [SSD_DI, SSD_DI + SSD_G * SSD_N], axis=-1)
    xs = xs.reshape(bn, seq_len, SSD_H, SSD_P)
    bm = bm.reshape(bn, seq_len, SSD_G, SSD_N)
    cm = cm.reshape(bn, seq_len, SSD_G, SSD_N)
    dt = jax.nn.softplus(dt_raw.reshape(bn, seq_len, 2, SSD_H) + dt_bias)
    a_neg = -jnp.exp(a_log)
    yf, hf = ssd_scan(xs, dt[:, :, 0], a_neg[0], bm, cm, state[0])
    yb, hb = ssd_scan(flip(xs), flip(dt[:, :, 1]), a_neg[1], flip(bm), flip(cm), state[1])
    y = yf + flip(yb) + d[:, None] * xs
    y = y.reshape(bn, seq_len, SSD_DI) * jax.nn.silu(z)
    return rmsnorm(y, norm_g), (hf, hb)


def gated_delta_rule(q, k, v, beta, g, s0):
    bn, seq_len, nh, dk = q.shape
    dv = v.shape[-1]
    qn = DN_CHUNK
    nc = seq_len // qn

    def chunks(t):
        return jnp.swapaxes(t.reshape((bn, nc, qn) + t.shape[2:]), 2, 3)
    qc, kc, vc, bc, gc = (chunks(t) for t in (q, k, v, beta, g))
    gcum = jnp.cumsum(gc, axis=-1)
    idx = jnp.arange(qn)
    incl = idx[:, None] >= idx[None, :]
    strict = idx[:, None] > idx[None, :]
    dmat = jnp.exp(jnp.where(incl, gcum[..., :, None] - gcum[..., None, :], -jnp.inf))
    kb = kc * bc[..., None]
    m = jnp.where(strict, jnp.einsum('bnhid,bnhjd->bnhij', kb, kc) * dmat, 0.0)
    rhs = jnp.concatenate([vc * bc[..., None], kb * jnp.exp(gcum)[..., None]], axis=-1)
    sol = lax.linalg.triangular_solve(m + jnp.eye(qn, dtype=m.dtype), rhs,
                                      left_side=True, lower=True, unit_diagonal=True)
    u, w = sol[..., :dv], sol[..., dv:]
    qk = jnp.einsum('bnhid,bnhjd->bnhij', qc, kc) * dmat
    q_dec = qc * jnp.exp(gcum)[..., None]
    k_tail = kc * jnp.exp(gcum[..., -1:] - gcum)[..., None]
    tot = jnp.exp(gcum[..., -1])

    def step(s, inp):
        u_c, w_c, qk_c, qd_c, kt_c, tot_c = inp
        v_new = u_c - jnp.einsum('bhqk,bhkv->bhqv', w_c, s)
        o = jnp.einsum('bhqk,bhkv->bhqv', qd_c, s) + jnp.einsum('bhij,bhjv->bhiv', qk_c, v_new)
        s = s * tot_c[..., None, None] + jnp.einsum('bhqk,bhqv->bhkv', kt_c, v_new)
        return s, o
    xs = tuple(jnp.moveaxis(t, 1, 0) for t in (u, w, qk, q_dec, k_tail, tot))
    s_last, o = lax.scan(step, s0, xs)
    o = jnp.swapaxes(jnp.moveaxis(o, 0, 1), 2, 3).reshape(bn, seq_len, nh, dv)
    return o, s_last


DN_TT = 256
DN_HB = 8
DN_W = DN_H * DN_DK
DN_NEUMANN = 5


def _dn_prep_body(xp_ref, x_ref, xn_ref, cw_ref, cb_ref, o_ref, *, tt, nblk):
    blk = pl.program_id(1)
    part = pl.program_id(2)
    x = x_ref[0]
    xp = xp_ref[0, tt - V7X_SUBLANES:, :] * (blk > 0).astype(F32)
    xn = xn_ref[0, :V7X_SUBLANES, :] * (blk < nblk - 1).astype(F32)
    xc = jnp.concatenate([xp, x, xn], axis=0)
    base = V7X_SUBLANES - CONV_PAD[0]
    y = cb_ref[...] + sum(cw_ref[j:j + 1, :] * xc[base + j:base + j + tt, :] for j in range(CONV_W))
    y = jax.nn.silu(y)
    scale = jnp.where(part == 0, DN_DK ** -0.5, 1.0)
    for h in range(DN_H):
        yh = y[:, DN_DK * h:DN_DK * (h + 1)]
        yn = yh * lax.rsqrt(jnp.sum(yh * yh, axis=-1, keepdims=True) + EPS) * scale
        o_ref[0, 0, h] = jnp.where(part == 2, yh, yn)


def dn_prep(qkv, conv_w, conv_b):
    bn, seq_len, _ = qkv.shape
    tt = min(DN_TT, seq_len)
    nblk = seq_len // tt
    cur = lambda b, i, p: (b, i, p)
    prev = lambda b, i, p: (b, jnp.maximum(i - 1, 0), p)
    nxt = lambda b, i, p: (b, jnp.minimum(i + 1, nblk - 1), p)
    return pl.pallas_call(
        functools.partial(_dn_prep_body, tt=tt, nblk=nblk),
        grid=(bn, nblk, 3),
        in_specs=[pl.BlockSpec((1, tt, DN_W), prev), pl.BlockSpec((1, tt, DN_W), cur),
                  pl.BlockSpec((1, tt, DN_W), nxt),
                  pl.BlockSpec((CONV_W, DN_W), lambda b, i, p: (0, p)),
                  pl.BlockSpec((1, DN_W), lambda b, i, p: (0, p))],
        out_specs=pl.BlockSpec((1, 1, DN_H, tt, DN_DK), lambda b, i, p: (p, b, 0, i, 0)),
        out_shape=jax.ShapeDtypeStruct((3, bn, DN_H, seq_len, DN_DK), F32),
        compiler_params=pltpu.CompilerParams(
            dimension_semantics=("parallel", "parallel", "arbitrary"),
            vmem_limit_bytes=V7X_VMEM_LIMIT),
        name="dn_prep",
    )(qkv, qkv, qkv, conv_w, conv_b.reshape(1, 3 * DN_W))


def _dn_delta_body(prm_ref, q_ref, k_ref, v_ref, ac_ref, bc_ref, ar_ref, s0_ref,
                   o_ref, sf_ref, s_s, *, rev):
    hg = pl.program_id(1)
    c = pl.program_id(2)

    @pl.when(c == 0)
    def _():
        s_s[...] = s0_ref[0]

    qn = DN_CHUNK
    ri = lax.broadcasted_iota(jnp.int32, (qn, qn), 0)
    ci = lax.broadcasted_iota(jnp.int32, (qn, qn), 1)
    incl = (ri <= ci) if rev else (ri >= ci)
    incl_t = (ri >= ci) if rev else (ri <= ci)
    strict = (ri < ci) if rev else (ri > ci)
    eye = (ri == ci).astype(F32)
    end = 0 if rev else qn - 1
    nt = (((1,), (1,)), ((), ()))
    tn = (((0,), (0,)), ((), ()))
    heads = range(DN_HB)
    qs = [q_ref[0, j] for j in heads]
    ks = [k_ref[0, j] for j in heads]
    kbfs = [k.astype(BF16) for k in ks]
    gcs, dmats, kbs, betas, ns = [], [], [], [], []
    for j in heads:
        hd = hg * DN_HB + j
        neg_a = prm_ref[0, hd]
        dtb = prm_ref[1, hd]
        g_col = neg_a * jax.nn.softplus(ac_ref[0, 0, :, j:j + 1] + dtb)
        g_row = neg_a * jax.nn.softplus(ar_ref[0, 0, 0, j:j + 1, :] + dtb)
        beta = jax.nn.sigmoid(bc_ref[0, 0, :, j:j + 1])
        gc_col = jnp.sum(jnp.where(incl, g_row, 0.0), axis=1, keepdims=True)
        gc_row = jnp.sum(jnp.where(incl_t, g_col, 0.0), axis=0, keepdims=True)
        dmat = jnp.where(incl, jnp.exp(gc_col - gc_row), 0.0)
        kb = ks[j] * beta
        kk = lax.dot_general(kb.astype(BF16), kbfs[j], nt, preferred_element_type=F32)
        gcs.append(gc_col)
        dmats.append(dmat)
        kbs.append(kb)
        betas.append(beta)
        ns.append(jnp.where(strict, -(kk * dmat), 0.0))
    tinvs = [eye + n for n in ns]
    npows = ns
    for _ in range(DN_NEUMANN):
        nbs = [n.astype(BF16) for n in npows]
        npows = [jnp.dot(nb, nb, preferred_element_type=F32) for nb in nbs]
        tinvs = [t + jnp.dot(t.astype(BF16), n.astype(BF16), preferred_element_type=F32)
                 for t, n in zip(tinvs, npows)]
    egs = [jnp.exp(gc) for gc in gcs]
    sols = [jnp.dot(tinvs[j].astype(BF16),
                    jnp.concatenate([v_ref[0, j] * betas[j], kbs[j] * egs[j]], axis=1).astype(BF16),
                    preferred_element_type=F32) for j in heads]
    qks = [lax.dot_general(qs[j].astype(BF16), kbfs[j], nt, preferred_element_type=F32) * dmats[j]
           for j in heads]
    ss = [s_s[j] for j in heads]
    sbs = [s.astype(BF16) for s in ss]
    vnbs = [(sols[j][:, :DN_DV]
             - jnp.dot(sols[j][:, DN_DV:].astype(BF16), sbs[j], preferred_element_type=F32)).astype(BF16)
            for j in heads]
    for j in heads:
        o_ref[0, j] = (jnp.dot((qs[j] * egs[j]).astype(BF16), sbs[j], preferred_element_type=F32)
                       + jnp.dot(qks[j].astype(BF16), vnbs[j], preferred_element_type=F32))
    for j in heads:
        gend = gcs[j][end:end + 1, :]
        k_tail = (ks[j] * jnp.exp(gend - gcs[j])).astype(BF16)
        s_new = ss[j] * jnp.exp(gend) + lax.dot_general(k_tail, vnbs[j], tn, preferred_element_type=F32)
        s_s[j] = s_new
        sf_ref[0, j] = s_new


def dn_delta(qkvh, a_col, b_col, a_row, prm, s0, rev):
    _, bn, _, seq_len, _ = qkvh.shape
    qn = DN_CHUNK
    nc = seq_len // qn
    cidx = (lambda c: nc - 1 - c) if rev else (lambda c: c)
    seq = lambda part: pl.BlockSpec((None, 1, DN_HB, qn, DN_DK),
                                    lambda b, g, c, part=part: (part, b, g, cidx(c), 0))
    col = pl.BlockSpec((1, 1, qn, DN_HB), lambda b, g, c: (b, g, cidx(c), 0))
    row = pl.BlockSpec((1, 1, 1, DN_HB, qn), lambda b, g, c: (b, g, cidx(c), 0, 0))
    st = pl.BlockSpec((1, DN_HB, DN_DK, DN_DV), lambda b, g, c: (b, g, 0, 0))
    return pl.pallas_call(
        functools.partial(_dn_delta_body, rev=rev),
        grid=(bn, DN_H // DN_HB, nc),
        in_specs=[pl.BlockSpec(memory_space=pltpu.SMEM), seq(0), seq(1), seq(2), col, col, row, st],
        out_specs=[pl.BlockSpec((1, DN_HB, qn, DN_DV), lambda b, g, c: (b, g, cidx(c), 0)), st],
        out_shape=[jax.ShapeDtypeStruct((bn, DN_H, seq_len, DN_DV), F32),
                   jax.ShapeDtypeStruct((bn, DN_H, DN_DK, DN_DV), F32)],
        scratch_shapes=[pltpu.VMEM((DN_HB, DN_DK, DN_DV), F32)],
        compiler_params=pltpu.CompilerParams(
            dimension_semantics=("parallel", "parallel", "arbitrary"),
            vmem_limit_bytes=V7X_VMEM_LIMIT),
        name="dn_delta_bwd" if rev else "dn_delta_fwd",
    )(prm, qkvh, qkvh, qkvh, a_col, b_col, a_row, s0)


def _dn_combine_body(of_ref, ob_ref, z_ref, g_ref, o_ref):
    for h in range(DN_H):
        o = of_ref[0, h] + ob_ref[0, h]
        y = o * lax.rsqrt(jnp.mean(o * o, axis=-1, keepdims=True) + EPS) * g_ref[...]
        o_ref[0, :, DN_DV * h:DN_DV * (h + 1)] = y * jax.nn.silu(z_ref[0, :, DN_DV * h:DN_DV * (h + 1)])


def dn_combine(of, ob, z, norm_g):
    bn, _, seq_len, _ = of.shape
    tt = min(DN_TT, seq_len)
    hm = pl.BlockSpec((1, DN_H, tt, DN_DV), lambda b, i: (b, 0, i, 0))
    tm = pl.BlockSpec((1, tt, DN_H * DN_DV), lambda b, i: (b, i, 0))
    return pl.pallas_call(
        _dn_combine_body,
        grid=(bn, seq_len // tt),
        in_specs=[hm, hm, tm, pl.BlockSpec((1, DN_DV), lambda b, i: (0, 0))],
        out_specs=tm,
        out_shape=jax.ShapeDtypeStruct((bn, seq_len, DN_H * DN_DV), F32),
        compiler_params=pltpu.CompilerParams(
            dimension_semantics=("parallel", "parallel"), vmem_limit_bytes=V7X_VMEM_LIMIT),
        name="dn_combine",
    )(of, ob, z, norm_g.reshape(1, DN_DV))


def deltanet_stream(qkv, z, a_raw, b_raw, state, conv_w, conv_b, dt_bias, a_log, norm_g):
    bn, seq_len, _ = z.shape
    qkv = jax.nn.silu(dwconv(qkv, conv_w, conv_b))
    q, k, v = jnp.split(qkv, [DN_H * DN_DK, 2 * DN_H * DN_DK], axis=-1)
    q = l2norm(q.reshape(bn, seq_len, DN_H, DN_DK)) * (DN_DK ** -0.5)
    k = l2norm(k.reshape(bn, seq_len, DN_H, DN_DK))
    v = v.reshape(bn, seq_len, DN_H, DN_DV)
    beta = jax.nn.sigmoid(b_raw.reshape(bn, seq_len, 2, DN_H))
    g = -jnp.exp(a_log) * jax.nn.softplus(a_raw.reshape(bn, seq_len, 2, DN_H) + dt_bias)
    of, sf = gated_delta_rule(q, k, v, beta[:, :, 0], g[:, :, 0], state[0])
    ob, sb = gated_delta_rule(flip(q), flip(k), flip(v), flip(beta[:, :, 1]), flip(g[:, :, 1]), state[1])
    o = rmsnorm(of + flip(ob), norm_g)
    o = o.reshape(bn, seq_len, DN_H * DN_DV) * jax.nn.silu(z)
    return o, (sf, sb)


def deltanet_stream_pallas(qkv, z, a_raw, b_raw, state, conv_w, conv_b, dt_bias, a_log, norm_g):
    bn, seq_len, _ = z.shape
    qkvh = dn_prep(qkv, conv_w, conv_b)
    ng = DN_H // DN_HB
    nc = seq_len // DN_CHUNK
    outs = []
    for dr in range(2):
        a = a_raw.reshape(bn, seq_len, 2, ng, DN_HB)[:, :, dr]
        b = b_raw.reshape(bn, seq_len, 2, ng, DN_HB)[:, :, dr]
        a_col = jnp.transpose(a, (0, 2, 1, 3))
        b_col = jnp.transpose(b, (0, 2, 1, 3))
        a_row = jnp.transpose(a.reshape(bn, nc, DN_CHUNK, ng, DN_HB), (0, 3, 1, 4, 2))
        prm = jnp.stack([-jnp.exp(a_log[dr]), dt_bias[dr]]).astype(F32)
        outs.append(dn_delta(qkvh, a_col, b_col, a_row, prm, state[dr], dr == 1))
    (of, sf), (ob, sb) = outs
    return dn_combine(of, ob, z, norm_g), (sf, sb)


RG_TT = 256


def _rglru_body(xp_ref, x_ref, xn_ref, h0_ref, cw_ref, cb_ref, wa_ref, ba_ref, wx_ref, bx_ref, c_ref,
                h_ref, hf_ref, carry_s, *, rev, tt, nblk):
    i = pl.program_id(1)
    blk = (nblk - 1 - i) if rev else i

    @pl.when(i == 0)
    def _():
        carry_s[...] = h0_ref[0]

    x = x_ref[0]
    xp = xp_ref[0, tt - V7X_SUBLANES:, :] * (blk > 0).astype(F32)
    xn = xn_ref[0, :V7X_SUBLANES, :] * (blk < nblk - 1).astype(F32)
    xc = jnp.concatenate([xp, x, xn], axis=0)
    base = V7X_SUBLANES - CONV_PAD[0]
    xh = cb_ref[...] + sum(cw_ref[j:j + 1, :] * xc[base + j:base + j + tt, :] for j in range(CONV_W))

    xb = xh.astype(BF16)
    rs, is_ = [], []
    for h in range(RG_H):
        xs = xb[:, RG_BS * h:RG_BS * (h + 1)]
        rs.append(jnp.dot(xs, wa_ref[h], preferred_element_type=F32))
        is_.append(jnp.dot(xs, wx_ref[h], preferred_element_type=F32))
    r = jax.nn.sigmoid(jnp.concatenate(rs, axis=1) + ba_ref[...])
    ig = jax.nn.sigmoid(jnp.concatenate(is_, axis=1) + bx_ref[...])
    log_a = r * c_ref[...]
    a = jnp.exp(log_a)
    b = jnp.sqrt(1.0 - jnp.exp(2.0 * log_a)) * (ig * xh)

    row = lax.broadcasted_iota(jnp.int32, (tt, RG_W), 0)
    s = 1
    while s < tt:
        valid = (row < tt - s) if rev else (row >= s)
        shift = (tt - s) if rev else s
        a_s = jnp.where(valid, pltpu.roll(a, shift, 0), 1.0)
        b_s = jnp.where(valid, pltpu.roll(b, shift, 0), 0.0)
        b = a * b_s + b
        a = a * a_s
        s *= 2
    hcur = b + a * carry_s[...]
    h_ref[0] = hcur
    last = 0 if rev else tt - 1
    carry_s[...] = hcur[last:last + 1, :]
    hf_ref[0] = hcur[last:last + 1, :]


def rglru_scan(xr, h0, conv_w, conv_b, wa, ba, wx, bx, lam, rev):
    bn, seq_len, _ = xr.shape
    tt = min(RG_TT, seq_len)
    nblk = seq_len // tt
    blk = (lambda i: nblk - 1 - i) if rev else (lambda i: i)
    cur = lambda b, i: (b, blk(i), 0)
    prev = lambda b, i: (b, jnp.maximum(blk(i) - 1, 0), 0)
    nxt = lambda b, i: (b, jnp.minimum(blk(i) + 1, nblk - 1), 0)
    vec = pl.BlockSpec((1, RG_W), lambda b, i: (0, 0))
    wspec = pl.BlockSpec((RG_H, RG_BS, RG_BS), lambda b, i: (0, 0, 0))
    st = pl.BlockSpec((1, 1, RG_W), lambda b, i: (b, 0, 0))
    c = (-RG_C * jax.nn.softplus(-lam)).reshape(1, RG_W)
    h, hf = pl.pallas_call(
        functools.partial(_rglru_body, rev=rev, tt=tt, nblk=nblk),
        grid=(bn, nblk),
        in_specs=[pl.BlockSpec((1, tt, RG_W), prev), pl.BlockSpec((1, tt, RG_W), cur),
                  pl.BlockSpec((1, tt, RG_W), nxt), st,
                  pl.BlockSpec((CONV_W, RG_W), lambda b, i: (0, 0)), vec,
                  wspec, vec, wspec, vec, vec],
        out_specs=[pl.BlockSpec((1, tt, RG_W), cur), st],
        out_shape=[jax.ShapeDtypeStruct((bn, seq_len, RG_W), F32),
                   jax.ShapeDtypeStruct((bn, 1, RG_W), F32)],
        scratch_shapes=[pltpu.VMEM((1, RG_W), F32)],
        compiler_params=pltpu.CompilerParams(
            dimension_semantics=("parallel", "arbitrary"),
            vmem_limit_bytes=V7X_VMEM_LIMIT),
        name="rglru_bwd" if rev else "rglru_fwd",
    )(xr, xr, xr, h0.reshape(bn, 1, RG_W), conv_w, conv_b.reshape(1, RG_W),
      wa.astype(BF16), ba.reshape(1, RG_W), wx.astype(BF16), bx.reshape(1, RG_W), c)
    return h, hf.reshape(bn, RG_W)


def _rg_combine_body(hf_ref, hb_ref, g_ref, o_ref):
    o_ref[...] = (hf_ref[...] + hb_ref[...]) * jax.nn.gelu(g_ref[...])


def rg_combine(hf, hb, gate):
    m = hf.shape[0]
    tm = _pick(m, (1024, 512, 256, 128, 64, 32, 16, 8))
    row = pl.BlockSpec((tm, RG_W), lambda i: (i, 0))
    return pl.pallas_call(
        _rg_combine_body, grid=(m // tm,), in_specs=[row, row, row], out_specs=row,
        out_shape=jax.ShapeDtypeStruct((m, RG_W), F32),
        compiler_params=pltpu.CompilerParams(dimension_semantics=("parallel",)),
        name="rg_combine",
    )(hf, hb, gate)


def rglru_stream(xr, gate, state, conv_w, conv_b, wa, ba, wx, bx, lam):
    bn, seq_len, _ = xr.shape
    hf, sf = rglru_scan(xr, state[0], conv_w, conv_b, wa[0], ba[0], wx[0], bx[0], lam[0], False)
    hb, sb = rglru_scan(xr, state[1], conv_w, conv_b, wa[1], ba[1], wx[1], bx[1], lam[1], True)
    m = bn * seq_len
    y = rg_combine(hf.reshape(m, RG_W), hb.reshape(m, RG_W), gate.reshape(m, RG_W))
    return y.reshape(bn, seq_len, RG_W), (sf, sb)


def even_mixer(hx, hc, w_in, w_out, s5_params, ssd_params, need_ctx):
    bn = hx.shape[0]

    def stream(h, st_s5, st_ssd):
        p = mm3(h, w_in)
        u, z, xbc, dt_raw = jnp.split(p, EV_CUTS, axis=-1)
        ya, st_a = s5_stream(u, st_s5, *s5_params)
        yb, st_b = ssd_stream(z, xbc, dt_raw, st_ssd, *ssd_params)
        return jnp.concatenate([ya, yb], axis=-1), st_a, st_b
    zs5 = jnp.zeros((bn, S5_LANES), F32)
    zssd = jnp.zeros((bn, SSD_H, SSD_P, SSD_N), F32)
    yc, st_a, st_b = stream(hc, (zs5, zs5, zs5, zs5), (zssd, zssd))
    yx, _, _ = stream(hx, st_a, st_b)
    return mm3(yx, w_out), (mm3(yc, w_out) if need_ctx else None)


def odd_mixer(hx, hc, w_in, w_out, dn_params, rg_params, need_ctx):
    bn = hx.shape[0]

    def stream(h, st_dn, st_rg):
        p = mm3(h, w_in)
        qkv, z, a_raw, b_raw, xr, gate = jnp.split(p, OD_CUTS, axis=-1)
        yd, st_d = deltanet_stream(qkv, z, a_raw, b_raw, st_dn, *dn_params)
        yr, st_r = rglru_stream(xr, gate, st_rg, *rg_params)
        return jnp.concatenate([yd, yr], axis=-1), st_d, st_r
    zdn = jnp.zeros((bn, DN_H, DN_DK, DN_DV), F32)
    zrg = jnp.zeros((bn, RG_W), F32)
    yc, st_d, st_r = stream(hc, (zdn, zdn), (zrg, zrg))
    yx, _, _ = stream(hx, st_d, st_r)
    return mm3(yx, w_out), (mm3(yc, w_out) if need_ctx else None)


PEER_TM = 512
PEER_EB = 512
PEER_RT = 16
PEER_SEL = PEER_H * PEER_TOPK


def _peer_body(flags_ref, ht_ref, u_ref, vt_ref, s1_ref, s2_ref, e1_ref, e2_ref, tau_ref,
               pidx_ref, ridx_ref, gate_ref, o_ref, st_s, g_s, *, tm, eb):
    i = pl.program_id(0)
    j = pl.program_id(1)
    npk = eb // PEER_NK
    st_s[...] = jnp.dot(u_ref[...], ht_ref[...], preferred_element_type=F32)

    @pl.when(flags_ref[i] == 0)
    def _():
        for lg in range(tm // V7X_LANES):
            ls = slice(V7X_LANES * lg, V7X_LANES * (lg + 1))
            t_rows = [tau_ref[pl.ds(h, 1), ls] for h in range(PEER_H)]
            for pk in range(npk):
                p = j * npk + pk
                a_tile = s1_ref[p, :, ls]
                z_tile = e1_ref[p, :, ls]
                a_rows = [a_tile[h:h + 1, :] for h in range(PEER_H)]
                z_rows = [z_tile[h:h + 1, :] for h in range(PEER_H)]

                def rbody(rt, carry, ls=ls, pk=pk, a_rows=a_rows, z_rows=z_rows, t_rows=t_rows):
                    r0 = pl.multiple_of(rt * PEER_RT, PEER_RT)
                    acc = jnp.zeros((PEER_RT, V7X_LANES), F32)
                    for h in range(PEER_H):
                        s2 = s2_ref[h, pl.ds(r0, PEER_RT), ls]
                        e2 = e2_ref[h, pl.ds(r0, PEER_RT), ls]
                        acc = acc + jnp.where(a_rows[h] + s2 >= t_rows[h], z_rows[h] * e2, 0.0)
                    row0 = pl.multiple_of(pk * PEER_NK + r0, PEER_RT)
                    act = jax.nn.gelu(st_s[pl.ds(row0, PEER_RT), ls])
                    g_s[pl.ds(row0, PEER_RT), ls] = (acc * act).astype(BF16)
                    return carry

                lax.fori_loop(0, PEER_NK // PEER_RT, rbody, 0)

    @pl.when(flags_ref[i] != 0)
    def _():
        riota = lax.broadcasted_iota(jnp.int32, (PEER_NK, V7X_LANES), 0)
        for lg in range(tm // V7X_LANES):
            ls = slice(V7X_LANES * lg, V7X_LANES * (lg + 1))
            for pk in range(npk):
                p = j * npk + pk

                def kbody(k8, w, ls=ls, p=p):
                    k0 = pl.multiple_of(k8 * V7X_SUBLANES, V7X_SUBLANES)
                    ptile = pidx_ref[pl.ds(k0, V7X_SUBLANES), ls]
                    rtile = ridx_ref[pl.ds(k0, V7X_SUBLANES), ls]
                    gtile = gate_ref[pl.ds(k0, V7X_SUBLANES), ls]
                    ctile = jnp.where(ptile == p, gtile, 0.0)
                    for k in range(V7X_SUBLANES):
                        w = w + jnp.where(rtile[k:k + 1, :] == riota, ctile[k:k + 1, :], 0.0)
                    return w

                w = lax.fori_loop(0, PEER_SEL // V7X_SUBLANES, kbody,
                                  jnp.zeros((PEER_NK, V7X_LANES), F32))
                rows = slice(pk * PEER_NK, (pk + 1) * PEER_NK)
                g_s[rows, ls] = (w * jax.nn.gelu(st_s[rows, ls])).astype(BF16)

    contrib = jnp.dot(vt_ref[...], g_s[...], preferred_element_type=F32)

    @pl.when(j == 0)
    def _():
        o_ref[...] = contrib

    @pl.when(j != 0)
    def _():
        o_ref[...] += contrib


def peer_experts(flags, ht, u_bf, vt_bf, s1t, s2t, e1t, e2t, taut, pidx, ridx, gate, tm):
    dm, t = ht.shape
    eb = PEER_EB
    head3 = pl.BlockSpec((PEER_H, PEER_NK, tm), lambda i, j, f: (0, 0, i))
    key3 = pl.BlockSpec((PEER_NK, PEER_H, tm), lambda i, j, f: (0, 0, i))
    sel2 = pl.BlockSpec((PEER_SEL, tm), lambda i, j, f: (0, i))
    grid_spec = pltpu.PrefetchScalarGridSpec(
        num_scalar_prefetch=1,
        grid=(t // tm, PEER_E // eb),
        in_specs=[pl.BlockSpec((dm, tm), lambda i, j, f: (0, i)),
                  pl.BlockSpec((eb, dm), lambda i, j, f: (j, 0)),
                  pl.BlockSpec((dm, eb), lambda i, j, f: (0, j)),
                  key3, head3, key3, head3,
                  pl.BlockSpec((PEER_H, tm), lambda i, j, f: (0, i)),
                  sel2, sel2, sel2],
        out_specs=pl.BlockSpec((dm, tm), lambda i, j, f: (0, i)),
        scratch_shapes=[pltpu.VMEM((eb, tm), F32), pltpu.VMEM((eb, tm), BF16)])
    return pl.pallas_call(
        functools.partial(_peer_body, tm=tm, eb=eb),
        grid_spec=grid_spec,
        out_shape=jax.ShapeDtypeStruct((dm, t), F32),
        compiler_params=pltpu.CompilerParams(
            dimension_semantics=("parallel", "arbitrary"),
            vmem_limit_bytes=V7X_VMEM_LIMIT),
        name="peer_experts",
    )(flags, ht, u_bf, vt_bf, s1t, s2t, e1t, e2t, taut, pidx, ridx, gate)


def _topk_rows(x, k):
    n = x.shape[0]
    iota = lax.broadcasted_iota(jnp.int32, x.shape, 0)
    vals, idxs = [], []
    for r in range(k):
        m = jnp.max(x, axis=0, keepdims=True)
        idx = jnp.min(jnp.where(x == m, iota, n), axis=0, keepdims=True)
        vals.append(m)
        idxs.append(idx)
        if r < k - 1:
            x = jnp.where(iota == idx, -jnp.inf, x)
    return vals, idxs


def _peer_select_body(q_ref, k1_ref, k2_ref, s1_ref, s2_ref, e1_ref, e2_ref, tau_ref, tie_ref,
                      pidx_ref, ridx_ref, gate_ref):
    qb = q_ref[...].astype(BF16)
    dn = (((1,), (1,)), ((), ()))
    s1 = lax.dot_general(k1_ref[0], qb[:, :PEER_HALF], dn, preferred_element_type=F32)
    s2 = lax.dot_general(k2_ref[0], qb[:, PEER_HALF:], dn, preferred_element_type=F32)
    t1, i1 = _topk_rows(s1, PEER_TOPK + 1)
    t2, i2 = _topk_rows(s2, PEER_TOPK + 1)
    t2blk = jnp.concatenate(t2[:PEER_TOPK], axis=0)
    cand = jnp.concatenate([t1[a] + t2blk for a in range(PEER_TOPK)], axis=0)
    top, ci = _topk_rows(cand, PEER_TOPK)
    topb = jnp.concatenate(top, axis=0)
    cib = jnp.concatenate(ci, axis=0)
    ca = cib // PEER_TOPK
    cb = cib % PEER_TOPK
    pid = jnp.zeros_like(cib)
    rid = jnp.zeros_like(cib)
    for a in range(PEER_TOPK):
        pid = pid + jnp.where(ca == a, i1[a], 0)
        rid = rid + jnp.where(cb == a, i2[a], 0)
    ex = jnp.exp(topb - top[0])
    zsum = jnp.sum(ex, axis=0, keepdims=True)
    tau = top[PEER_TOPK - 1]
    cnt = jnp.sum((cand >= tau).astype(jnp.int32), axis=0, keepdims=True)
    tie = ((cnt != PEER_TOPK) | (t1[PEER_TOPK] + t2[0] >= tau) | (t1[0] + t2[PEER_TOPK] >= tau))
    s1_ref[0] = s1
    s2_ref[0] = s2
    e1_ref[0] = jnp.exp(s1 - t1[0]) / zsum
    e2_ref[0] = jnp.exp(s2 - t2[0])
    tau_ref[0] = tau
    tie_ref[0] = tie.astype(jnp.int32)
    pidx_ref[...] = pid
    ridx_ref[...] = rid
    gate_ref[...] = ex / zsum


def peer_select(q, k1_bf, k2_bf):
    t = q.shape[0]
    tl = V7X_LANES
    head3 = pl.BlockSpec((1, PEER_NK, tl), lambda i, h: (h, 0, i))
    row3 = pl.BlockSpec((1, 1, tl), lambda i, h: (h, 0, i))
    sel2 = pl.BlockSpec((PEER_TOPK, tl), lambda i, h: (h, i))
    key_spec = pl.BlockSpec((1, PEER_NK, PEER_HALF), lambda i, h: (h, 0, 0))
    f3 = jax.ShapeDtypeStruct((PEER_H, PEER_NK, t), F32)
    return pl.pallas_call(
        _peer_select_body,
        grid=(t // tl, PEER_H),
        in_specs=[pl.BlockSpec((tl, PEER_DK), lambda i, h: (i, h)), key_spec, key_spec],
        out_specs=[head3, head3, head3, head3, row3, row3, sel2, sel2, sel2],
        out_shape=[f3, f3, f3, f3,
                   jax.ShapeDtypeStruct((PEER_H, 1, t), F32),
                   jax.ShapeDtypeStruct((PEER_H, 1, t), jnp.int32),
                   jax.ShapeDtypeStruct((PEER_SEL, t), jnp.int32),
                   jax.ShapeDtypeStruct((PEER_SEL, t), jnp.int32),
                   jax.ShapeDtypeStruct((PEER_SEL, t), F32)],
        compiler_params=pltpu.CompilerParams(
            dimension_semantics=("parallel", "arbitrary"),
            vmem_limit_bytes=V7X_VMEM_LIMIT),
        name="peer_select",
    )(q, k1_bf, k2_bf)


def peer(h, w_q, k1, k2, u_bf, vt_bf, tm=PEER_TM):
    bn, seq_len, dm = h.shape
    t = bn * seq_len
    q = mm(h.reshape(t, dm), w_q)
    s1t, s2t, e1t, e2t, taut, tiet, pidx, ridx, gate = peer_select(q, k1.astype(BF16), k2.astype(BF16))
    flags = jnp.any(tiet.reshape(PEER_H, t // tm, tm) != 0, axis=(0, 2)).astype(jnp.int32)
    outt = peer_experts(
        flags, h.reshape(t, dm).T.astype(BF16), u_bf, vt_bf,
        jnp.transpose(s1t, (1, 0, 2)), s2t, jnp.transpose(e1t, (1, 0, 2)), e2t, taut.reshape(PEER_H, t),
        pidx, ridx, gate, tm)
    return outt.T.reshape(bn, seq_len, dm)


def kernel(x, c, ctx, c_ctx, ada_w, ada_b, norm1_g, norm2_g, final_g, ev_w_in, ev_w_out, s5_lam_re, s5_lam_im, s5_log_dt, s5_b_re, s5_b_im, s5_c_re, s5_c_im, s5_d, s5_glu_w, s5_glu_b, ssd_conv_w, ssd_conv_b, ssd_dt_bias, ssd_a_log, ssd_d, ssd_norm_g, od_w_in, od_w_out, dn_conv_w, dn_conv_b, dn_dt_bias, dn_a_log, dn_norm_g, rg_conv_w, rg_conv_b, rg_wa, rg_ba, rg_wx, rg_bx, rg_lam, peer_wq, peer_k1, peer_k2, peer_u, peer_v):
    bn, seq_len, _ = x.shape
    rows = seq_len // GRID_W
    sc = jax.nn.silu(c)
    scc = jax.nn.silu(c_ctx)
    for layer in range(DEPTH):
        j = layer // 2
        need_ctx = layer < DEPTH - 1
        mx = (sc @ ada_w[layer] + ada_b[layer]).reshape(bn, 6, 1, D_MODEL)
        mc = (scc @ ada_w[layer] + ada_b[layer]).reshape(6, D_MODEL)
        hx = modulate(rmsnorm(x, norm1_g[layer]), mx[:, 0], mx[:, 1])
        hc = modulate(rmsnorm(ctx, norm1_g[layer]), mc[0], mc[1])
        if layer % 2 == 0:
            s5_params = (s5_lam_re[j], s5_lam_im[j], s5_log_dt[j], s5_b_re[j], s5_b_im[j],
                         s5_c_re[j], s5_c_im[j], s5_d[j], s5_glu_w[j], s5_glu_b[j])
            ssd_params = (ssd_conv_w[j], ssd_conv_b[j], ssd_dt_bias[j], ssd_a_log[j],
                          ssd_d[j], ssd_norm_g[j])
            ox, oc = even_mixer(hx, hc, ev_w_in[j], ev_w_out[j], s5_params, ssd_params, need_ctx)
        else:
            dn_params = (dn_conv_w[j], dn_conv_b[j], dn_dt_bias[j], dn_a_log[j], dn_norm_g[j])
            rg_params = (rg_conv_w[j], rg_conv_b[j], rg_wa[j], rg_ba[j], rg_wx[j], rg_bx[j], rg_lam[j])
            ox, oc = odd_mixer(to_col_major(hx, rows), hc, od_w_in[j], od_w_out[j],
                               dn_params, rg_params, need_ctx)
            ox = to_row_major(ox, rows)
        x = x + mx[:, 2] * ox
        hx = modulate(rmsnorm(x, norm2_g[layer]), mx[:, 3], mx[:, 4])
        u_bf = peer_u[layer].astype(BF16)
        vt_bf = peer_v[layer].T.astype(BF16)
        x = x + mx[:, 5] * peer(hx, peer_wq[layer], peer_k1[layer], peer_k2[layer], u_bf, vt_bf)
        if need_ctx:
            ctx = ctx + mc[2] * oc
            hc = modulate(rmsnorm(ctx, norm2_g[layer]), mc[3], mc[4])
            ctx = ctx + mc[5] * peer(hc, peer_wq[layer], peer_k1[layer], peer_k2[layer], u_bf, vt_bf)
    return rmsnorm(x, final_g)
```

```python
import functools
import math

import jax
import jax.numpy as jnp
from jax import lax
from jax.experimental import pallas as pl
from jax.experimental.pallas import tpu as pltpu

D_MODEL = 2048
DEPTH = 2
GRID_W = 64
EPS = 1e-6
CONV_W = 4
CONV_PAD = (CONV_W // 2, CONV_W - 1 - CONV_W // 2)

S5_W = D_MODEL // 2
S5_CH = 16
S5_G = S5_W // S5_CH
S5_N = 64
SSD_P = 64
SSD_H = D_MODEL // SSD_P
SSD_DI = SSD_H * SSD_P
SSD_N = 128
SSD_G = 4
SSD_CHUNK = 128
SSD_CONV_CH = SSD_DI + 2 * SSD_G * SSD_N
EV_IN = S5_W + SSD_DI + SSD_CONV_CH + 2 * SSD_H
EV_MIX = S5_W + SSD_DI
DN_DK = 128
DN_DV = 128
DN_H = D_MODEL // DN_DK
DN_CHUNK = 64
DN_QKV = DN_H * (2 * DN_DK + DN_DV)
RG_W = D_MODEL // 2
RG_H = 8
RG_BS = RG_W // RG_H
RG_C = 8.0
OD_IN = DN_QKV + DN_H * DN_DV + 4 * DN_H + 2 * RG_W
OD_MIX = DN_H * DN_DV + RG_W
OD_CUTS = [DN_QKV,
           DN_QKV + DN_H * DN_DV,
           DN_QKV + DN_H * DN_DV + 2 * DN_H,
           DN_QKV + DN_H * DN_DV + 4 * DN_H,
           DN_QKV + DN_H * DN_DV + 4 * DN_H + RG_W]
EV_CUTS = [S5_W, S5_W + SSD_DI, S5_W + SSD_DI + SSD_CONV_CH]
PEER_H = 8
PEER_DK = 256
PEER_HALF = PEER_DK // 2
PEER_NK = 128
PEER_E = PEER_NK * PEER_NK
PEER_TOPK = 16
PEER_BLOCK = 128

F32 = jnp.float32
BF16 = jnp.bfloat16

V7X_LANES = 128
V7X_SUBLANES = 8
V7X_VMEM_LIMIT = 56 * 1024 * 1024


def _mm_body(x_ref, w_ref, o_ref):
    o_ref[...] = jnp.dot(x_ref[...], w_ref[...], preferred_element_type=F32).astype(o_ref.dtype)


def _pick(n, cands):
    for c in cands:
        if n % c == 0:
            return c
    raise ValueError(f"no tile for {n}")


def mm(x, w, out_dtype=F32):
    m, k = x.shape
    n = w.shape[1]
    n_pad = -n % V7X_LANES
    if n_pad:
        w = jnp.pad(w, ((0, 0), (0, n_pad)))
    np_ = n + n_pad
    tm = _pick(m, (1024, 512, 256, 128, 64, 32, 16, 8))
    tn = _pick(np_, (512, 256, 128))
    out = pl.pallas_call(
        _mm_body,
        grid=(m // tm, np_ // tn),
        in_specs=[pl.BlockSpec((tm, k), lambda i, j: (i, 0)),
                  pl.BlockSpec((k, tn), lambda i, j: (0, j))],
        out_specs=pl.BlockSpec((tm, tn), lambda i, j: (i, j)),
        out_shape=jax.ShapeDtypeStruct((m, np_), out_dtype),
        compiler_params=pltpu.CompilerParams(
            dimension_semantics=("parallel", "arbitrary"),
            vmem_limit_bytes=V7X_VMEM_LIMIT),
        name="mm",
    )(x.astype(BF16), w.astype(BF16))
    return out[:, :n] if n_pad else out


def mm3(h, w):
    b, l, d = h.shape
    return mm(h.reshape(b * l, d), w).reshape(b, l, w.shape[1])


def rmsnorm(x, g):
    xf = x.astype(F32)
    y = xf * lax.rsqrt(jnp.mean(xf * xf, axis=-1, keepdims=True) + EPS)
    return (y * g.astype(F32)).astype(x.dtype)


def l2norm(x):
    return x * lax.rsqrt(jnp.sum(x * x, axis=-1, keepdims=True) + EPS)


def modulate(h, shift, scale):
    return h * (1 + scale) + shift


def dwconv(x, w, b):
    y = lax.conv_general_dilated(x, w[:, None, :], (1,), [CONV_PAD],
                                 dimension_numbers=('NWC', 'WIO', 'NWC'),
                                 feature_group_count=x.shape[-1])
    return y + b


def flip(t):
    return jnp.flip(t, axis=1)


def to_col_major(x, rows):
    b, l, d = x.shape
    return x.reshape(b, rows, GRID_W, d).transpose(0, 2, 1, 3).reshape(b, l, d)


def to_row_major(x, rows):
    b, l, d = x.shape
    return x.reshape(b, GRID_W, rows, d).transpose(0, 2, 1, 3).reshape(b, l, d)


def linear_scan(a, b, h0):
    b = b.at[:, 0].add(a[:, 0] * h0)

    def comb(e1, e2):
        return e1[0] * e2[0], e2[0] * e1[1] + e2[1]
    return lax.associative_scan(comb, (a, b), axis=1)[1]


S5_LANES = S5_G * S5_N
S5_TILE = 8
S5_SHIFTS = (1, 2, 4)
S5_BG = 8
S5_CG = 16
S5_CHUNK = 1024


def _s5_scan_body(u_ref, h0r_ref, h0i_ref, bdr_ref, bdi_ref, cdr_ref, cdi_ref, coef_ref,
                  y_ref, hfr_ref, hfi_ref, xr_s, xi_s, cr_s, ci_s, *, rev, tt):
    @pl.when(pl.program_id(1) == 0)
    def _():
        cr_s[...] = h0r_ref[0]
        ci_s[...] = h0i_ref[0]

    ub = u_ref[0].astype(BF16)
    kin = S5_BG * S5_CH
    kout = S5_BG * S5_N
    for j in range(S5_G // S5_BG):
        uj = ub[:, kin * j:kin * (j + 1)]
        xr_s[:, kout * j:kout * (j + 1)] = jnp.dot(uj, bdr_ref[j], preferred_element_type=F32)
        xi_s[:, kout * j:kout * (j + 1)] = jnp.dot(uj, bdi_ref[j], preferred_element_type=F32)

    nt = tt // S5_TILE
    carry_row = 0 if rev else S5_TILE - 1

    def tile(kk, carry):
        k = (nt - 1 - kk) if rev else kk
        r0 = pl.multiple_of(k * S5_TILE, S5_TILE)
        for c in range(S5_LANES // S5_CHUNK):
            sl = slice(S5_CHUNK * c, S5_CHUNK * (c + 1))
            xr = xr_s[pl.ds(r0, S5_TILE), sl]
            xi = xi_s[pl.ds(r0, S5_TILE), sl]
            for idx, sh in enumerate(S5_SHIFTS):
                pr = coef_ref[2 * idx, :, sl]
                pim = coef_ref[2 * idx + 1, :, sl]
                rs = (S5_TILE - sh) if rev else sh
                sr = pltpu.roll(xr, rs, 0)
                si = pltpu.roll(xi, rs, 0)
                xr, xi = xr + (pr * sr - pim * si), xi + (pr * si + pim * sr)
            pr = coef_ref[2 * len(S5_SHIFTS), :, sl]
            pim = coef_ref[2 * len(S5_SHIFTS) + 1, :, sl]
            hr = cr_s[:, sl]
            hi = ci_s[:, sl]
            xr, xi = xr + (pr * hr - pim * hi), xi + (pr * hi + pim * hr)
            xr_s[pl.ds(r0, S5_TILE), sl] = xr
            xi_s[pl.ds(r0, S5_TILE), sl] = xi
            cr_s[:, sl] = xr[carry_row:carry_row + 1, :]
            ci_s[:, sl] = xi[carry_row:carry_row + 1, :]
        return carry

    lax.fori_loop(0, nt, tile, 0)

    kc = S5_CG * S5_N
    nc = S5_CG * S5_CH
    for j in range(S5_G // S5_CG):
        sr = xr_s[:, kc * j:kc * (j + 1)].astype(BF16)
        si = xi_s[:, kc * j:kc * (j + 1)].astype(BF16)
        y_ref[0, :, nc * j:nc * (j + 1)] = (
            jnp.dot(sr, cdr_ref[j], preferred_element_type=F32)
            - jnp.dot(si, cdi_ref[j], preferred_element_type=F32))
    hfr_ref[0] = cr_s[...]
    hfi_ref[0] = ci_s[...]


def _s5_direction_params(lam_re, lam_im, log_dt, b_re, b_im, c_re, c_im, rev):
    dt = jnp.exp(log_dt)[:, None]
    mag = jnp.exp(lam_re * dt)
    ar, ai = mag * jnp.cos(lam_im * dt), mag * jnp.sin(lam_im * dt)
    den = lam_re * lam_re + lam_im * lam_im
    cr = ((ar - 1) * lam_re + ai * lam_im) / den
    ci = (ai * lam_re - (ar - 1) * lam_im) / den
    bb_re = cr[..., None] * b_re - ci[..., None] * b_im
    bb_im = cr[..., None] * b_im + ci[..., None] * b_re

    def bdiag_in(bb):
        t = bb.reshape(S5_G // S5_BG, S5_BG, S5_N, S5_CH)
        eye = jnp.eye(S5_BG, dtype=F32)
        return jnp.einsum('jgnc,gh->jgchn', t, eye).reshape(
            S5_G // S5_BG, S5_BG * S5_CH, S5_BG * S5_N).astype(BF16)

    def bdiag_out(cc):
        t = cc.reshape(S5_G // S5_CG, S5_CG, S5_CH, S5_N)
        eye = jnp.eye(S5_CG, dtype=F32)
        return jnp.einsum('jgcn,gh->jgnhc', t, eye).reshape(
            S5_G // S5_CG, S5_CG * S5_N, S5_CG * S5_CH).astype(BF16)

    a_r, a_i = ar.reshape(-1), ai.reshape(-1)
    pw = [(a_r, a_i)]
    for _ in range(S5_TILE - 1):
        pr, pim = pw[-1]
        pw.append((pr * a_r - pim * a_i, pr * a_i + pim * a_r))
    rows = jnp.arange(S5_TILE)[:, None]
    coef = []
    for sh in S5_SHIFTS:
        valid = (rows <= S5_TILE - 1 - sh) if rev else (rows >= sh)
        coef.append(jnp.where(valid, pw[sh - 1][0][None, :], 0.0))
        coef.append(jnp.where(valid, pw[sh - 1][1][None, :], 0.0))
    order = list(range(S5_TILE - 1, -1, -1)) if rev else list(range(S5_TILE))
    coef.append(jnp.stack([pw[k][0] for k in order]))
    coef.append(jnp.stack([pw[k][1] for k in order]))
    return (bdiag_in(bb_re), bdiag_in(bb_im), bdiag_out(c_re), bdiag_out(c_im),
            jnp.stack(coef).astype(F32))


def s5_scan(u, h0r, h0i, dparams, rev):
    bn, seq_len, _ = u.shape
    bdr, bdi, cdr, cdi, coef = dparams
    tt = min(256, seq_len)
    nblk = seq_len // tt
    tmap = (lambda b, i: (b, nblk - 1 - i, 0)) if rev else (lambda b, i: (b, i, 0))
    const3 = lambda b, i: (0, 0, 0)
    y, hfr, hfi = pl.pallas_call(
        functools.partial(_s5_scan_body, rev=rev, tt=tt),
        grid=(bn, nblk),
        in_specs=[pl.BlockSpec((1, tt, S5_W), tmap),
                  pl.BlockSpec((1, 1, S5_LANES), lambda b, i: (b, 0, 0)),
                  pl.BlockSpec((1, 1, S5_LANES), lambda b, i: (b, 0, 0)),
                  pl.BlockSpec(bdr.shape, const3),
                  pl.BlockSpec(bdi.shape, const3),
                  pl.BlockSpec(cdr.shape, const3),
                  pl.BlockSpec(cdi.shape, const3),
                  pl.BlockSpec(coef.shape, const3)],
        out_specs=[pl.BlockSpec((1, tt, S5_W), tmap),
                   pl.BlockSpec((1, 1, S5_LANES), lambda b, i: (b, 0, 0)),
                   pl.BlockSpec((1, 1, S5_LANES), lambda b, i: (b, 0, 0))],
        out_shape=[jax.ShapeDtypeStruct((bn, seq_len, S5_W), F32),
                   jax.ShapeDtypeStruct((bn, 1, S5_LANES), F32),
                   jax.ShapeDtypeStruct((bn, 1, S5_LANES), F32)],
        scratch_shapes=[pltpu.VMEM((tt, S5_LANES), F32), pltpu.VMEM((tt, S5_LANES), F32),
                        pltpu.VMEM((1, S5_LANES), F32), pltpu.VMEM((1, S5_LANES), F32)],
        compiler_params=pltpu.CompilerParams(
            dimension_semantics=("parallel", "arbitrary"),
            vmem_limit_bytes=V7X_VMEM_LIMIT),
        name="s5_scan_bwd" if rev else "s5_scan_fwd",
    )(u, h0r.reshape(bn, 1, S5_LANES), h0i.reshape(bn, 1, S5_LANES), bdr, bdi, cdr, cdi, coef)
    return y, hfr.reshape(bn, S5_LANES), hfi.reshape(bn, S5_LANES)


def _s5_glu_body(yf_ref, yb_ref, u_ref, d_ref, w_ref, b_ref, o_ref):
    y = yf_ref[...] + yb_ref[...] + d_ref[...] * u_ref[...]
    g = jax.nn.gelu(y)
    z = jnp.dot(g.astype(BF16), w_ref[...], preferred_element_type=F32) + b_ref[...]
    o_ref[...] = (g * jax.nn.sigmoid(z)).astype(o_ref.dtype)


def s5_glu(yf, yb, u, d, glu_w, glu_b):
    m = yf.shape[0]
    tm = _pick(m, (512, 256, 128, 64, 32, 16, 8))
    row = pl.BlockSpec((tm, S5_W), lambda i: (i, 0))
    vec = pl.BlockSpec((1, S5_W), lambda i: (0, 0))
    return pl.pallas_call(
        _s5_glu_body,
        grid=(m // tm,),
        in_specs=[row, row, row, vec, pl.BlockSpec((S5_W, S5_W), lambda i: (0, 0)), vec],
        out_specs=row,
        out_shape=jax.ShapeDtypeStruct((m, S5_W), F32),
        compiler_params=pltpu.CompilerParams(
            dimension_semantics=("parallel",), vmem_limit_bytes=V7X_VMEM_LIMIT),
        name="s5_glu",
    )(yf, yb, u, d.reshape(1, S5_W), glu_w.astype(BF16), glu_b.reshape(1, S5_W))


def s5_stream(u, state, lam_re, lam_im, log_dt, b_re, b_im, c_re, c_im, d, glu_w, glu_b):
    bn, seq_len, _ = u.shape
    pf = _s5_direction_params(lam_re[0], lam_im[0], log_dt[0], b_re[0], b_im[0], c_re[0], c_im[0], False)
    pb = _s5_direction_params(lam_re[1], lam_im[1], log_dt[1], b_re[1], b_im[1], c_re[1], c_im[1], True)
    yf, fr, fi = s5_scan(u, state[0], state[1], pf, False)
    yb, br, bi = s5_scan(u, state[2], state[3], pb, True)
    m = bn * seq_len
    out = s5_glu(yf.reshape(m, S5_W), yb.reshape(m, S5_W), u.reshape(m, S5_W), d, glu_w, glu_b)
    return out.reshape(bn, seq_len, S5_W), (fr, fi, br, bi)


def ssd_scan(x, dt, a_neg, bm, cm, h0):
    bn, seq_len, nh, hp = x.shape
    ng, ns = bm.shape[2], bm.shape[3]
    nr = nh // ng
    q = SSD_CHUNK
    nc = seq_len // q
    xdt = (x * dt[..., None]).reshape(bn, nc, q, ng, nr, hp)
    cum = jnp.cumsum((dt * a_neg).reshape(bn, nc, q, ng, nr), axis=2)
    bm = bm.reshape(bn, nc, q, ng, ns)
    cm = cm.reshape(bn, nc, q, ng, ns)
    idx = jnp.arange(q)
    incl = idx[:, None] >= idx[None, :]
    cum_t = jnp.moveaxis(cum, 2, -1)
    seg = jnp.exp(jnp.where(incl, cum_t[..., :, None] - cum_t[..., None, :], -jnp.inf))
    cb = jnp.einsum('bclgn,bcsgn->bcgls', cm, bm)
    y_diag = jnp.einsum('bcgrls,bcsgrp->bclgrp', seg * cb[:, :, :, None], xdt)

    def step(h, inp):
        c_c, b_c, xdt_c, cum_c = inp
        y_off = jnp.einsum('blgn,bgrpn->blgrp', c_c, h) * jnp.exp(cum_c)[..., None]
        dte = jnp.exp(cum_c[:, -1:] - cum_c)
        st = jnp.einsum('blgn,blgrp->bgrpn', b_c, xdt_c * dte[..., None])
        h = h * jnp.exp(cum_c[:, -1])[..., None, None] + st
        return h, y_off
    xs = tuple(jnp.moveaxis(t, 1, 0) for t in (cm, bm, xdt, cum))
    h_last, y_off = lax.scan(step, h0.reshape(bn, ng, nr, hp, ns), xs)
    y = y_diag + jnp.moveaxis(y_off, 0, 1)
    return y.reshape(bn, seq_len, nh, hp), h_last.reshape(bn, nh, hp, ns)


def ssd_stream(z, xbc, dt_raw, state, conv_w, conv_b, dt_bias, a_log, d, norm_g):
    bn, seq_len, _ = z.shape
    xbc = jax.nn.silu(dwconv(xbc, conv_w, conv_b))
    xs, bm, cm = jnp.split(xbc, [SSD_DI, SSD_DI + SSD_G * SSD_N], axis=-1)
    xs = xs.reshape(bn, seq_len, SSD_H, SSD_P)
    bm = bm.reshape(bn, seq_len, SSD_G, SSD_N)
    cm = cm.reshape(bn, seq_len, SSD_G, SSD_N)
    dt = jax.nn.softplus(dt_raw.reshape(bn, seq_len, 2, SSD_H) + dt_bias)
    a_neg = -jnp.exp(a_log)
    yf, hf = ssd_scan(xs, dt[:, :, 0], a_neg[0], bm, cm, state[0])
    yb, hb = ssd_scan(flip(xs), flip(dt[:, :, 1]), a_neg[1], flip(bm), flip(cm), state[1])
    y = yf + flip(yb) + d[:, None] * xs
    y = y.reshape(bn, seq_len, SSD_DI) * jax.nn.silu(z)
    return rmsnorm(y, norm_g), (hf, hb)


def gated_delta_rule(q, k, v, beta, g, s0):
    bn, seq_len, nh, dk = q.shape
    dv = v.shape[-1]
    qn = DN_CHUNK
    nc = seq_len // qn

    def chunks(t):
        return jnp.swapaxes(t.reshape((bn, nc, qn) + t.shape[2:]), 2, 3)
    qc, kc, vc, bc, gc = (chunks(t) for t in (q, k, v, beta, g))
    gcum = jnp.cumsum(gc, axis=-1)
    idx = jnp.arange(qn)
    incl = idx[:, None] >= idx[None, :]
    strict = idx[:, None] > idx[None, :]
    dmat = jnp.exp(jnp.where(incl, gcum[..., :, None] - gcum[..., None, :], -jnp.inf))
    kb = kc * bc[..., None]
    m = jnp.where(strict, jnp.einsum('bnhid,bnhjd->bnhij', kb, kc) * dmat, 0.0)
    rhs = jnp.concatenate([vc * bc[..., None], kb * jnp.exp(gcum)[..., None]], axis=-1)
    nmat = -m
    tinv = jnp.eye(qn, dtype=m.dtype) + nmat
    npow = nmat
    for _ in range(DN_NEUMANN):
        npow = jnp.einsum('...ij,...jk->...ik', npow, npow, precision=lax.Precision.HIGHEST)
        tinv = tinv + jnp.einsum('...ij,...jk->...ik', tinv, npow, precision=lax.Precision.HIGHEST)
    sol = jnp.einsum('...ij,...jk->...ik', tinv, rhs)
    u, w = sol[..., :dv], sol[..., dv:]
    qk = jnp.einsum('bnhid,bnhjd->bnhij', qc, kc) * dmat
    q_dec = qc * jnp.exp(gcum)[..., None]
    k_tail = kc * jnp.exp(gcum[..., -1:] - gcum)[..., None]
    tot = jnp.exp(gcum[..., -1])

    def step(s, inp):
        u_c, w_c, qk_c, qd_c, kt_c, tot_c = inp
        v_new = u_c - jnp.einsum('bhqk,bhkv->bhqv', w_c, s)
        o = jnp.einsum('bhqk,bhkv->bhqv', qd_c, s) + jnp.einsum('bhij,bhjv->bhiv', qk_c, v_new)
        s = s * tot_c[..., None, None] + jnp.einsum('bhqk,bhqv->bhkv', kt_c, v_new)
        return s, o
    xs = tuple(jnp.moveaxis(t, 1, 0) for t in (u, w, qk, q_dec, k_tail, tot))
    s_last, o = lax.scan(step, s0, xs)
    o = jnp.swapaxes(jnp.moveaxis(o, 0, 1), 2, 3).reshape(bn, seq_len, nh, dv)
    return o, s_last


DN_TT = 256
DN_HB = 8
DN_W = DN_H * DN_DK
DN_NEUMANN = 5


def _dn_prep_body(xp_ref, x_ref, xn_ref, cw_ref, cb_ref, o_ref, *, tt, nblk):
    blk = pl.program_id(1)
    part = pl.program_id(2)
    x = x_ref[0]
    xp = xp_ref[0, tt - V7X_SUBLANES:, :] * (blk > 0).astype(F32)
    xn = xn_ref[0, :V7X_SUBLANES, :] * (blk < nblk - 1).astype(F32)
    xc = jnp.concatenate([xp, x, xn], axis=0)
    base = V7X_SUBLANES - CONV_PAD[0]
    y = cb_ref[...] + sum(cw_ref[j:j + 1, :] * xc[base + j:base + j + tt, :] for j in range(CONV_W))
    y = jax.nn.silu(y)
    scale = jnp.where(part == 0, DN_DK ** -0.5, 1.0)
    for h in range(DN_H):
        yh = y[:, DN_DK * h:DN_DK * (h + 1)]
        yn = yh * lax.rsqrt(jnp.sum(yh * yh, axis=-1, keepdims=True) + EPS) * scale
        o_ref[0, 0, h] = jnp.where(part == 2, yh, yn)


def dn_prep(qkv, conv_w, conv_b):
    bn, seq_len, _ = qkv.shape
    tt = min(DN_TT, seq_len)
    nblk = seq_len // tt
    cur = lambda b, i, p: (b, i, p)
    prev = lambda b, i, p: (b, jnp.maximum(i - 1, 0), p)
    nxt = lambda b, i, p: (b, jnp.minimum(i + 1, nblk - 1), p)
    return pl.pallas_call(
        functools.partial(_dn_prep_body, tt=tt, nblk=nblk),
        grid=(bn, nblk, 3),
        in_specs=[pl.BlockSpec((1, tt, DN_W), prev), pl.BlockSpec((1, tt, DN_W), cur),
                  pl.BlockSpec((1, tt, DN_W), nxt),
                  pl.BlockSpec((CONV_W, DN_W), lambda b, i, p: (0, p)),
                  pl.BlockSpec((1, DN_W), lambda b, i, p: (0, p))],
        out_specs=pl.BlockSpec((1, 1, DN_H, tt, DN_DK), lambda b, i, p: (p, b, 0, i, 0)),
        out_shape=jax.ShapeDtypeStruct((3, bn, DN_H, seq_len, DN_DK), F32),
        compiler_params=pltpu.CompilerParams(
            dimension_semantics=("parallel", "parallel", "arbitrary"),
            vmem_limit_bytes=V7X_VMEM_LIMIT),
        name="dn_prep",
    )(qkv, qkv, qkv, conv_w, conv_b.reshape(1, 3 * DN_W))


def _dn_delta_body(prm_ref, q_ref, k_ref, v_ref, ac_ref, bc_ref, ar_ref, s0_ref,
                   o_ref, sf_ref, s_s, *, rev):
    hg = pl.program_id(1)
    c = pl.program_id(2)

    @pl.when(c == 0)
    def _():
        s_s[...] = s0_ref[0]

    qn = DN_CHUNK
    ri = lax.broadcasted_iota(jnp.int32, (qn, qn), 0)
    ci = lax.broadcasted_iota(jnp.int32, (qn, qn), 1)
    incl = (ri <= ci) if rev else (ri >= ci)
    incl_t = (ri >= ci) if rev else (ri <= ci)
    strict = (ri < ci) if rev else (ri > ci)
    eye = (ri == ci).astype(F32)
    end = 0 if rev else qn - 1
    nt = (((1,), (1,)), ((), ()))
    tn = (((0,), (0,)), ((), ()))
    heads = range(DN_HB)
    qs = [q_ref[0, j] for j in heads]
    ks = [k_ref[0, j] for j in heads]
    kbfs = [k.astype(BF16) for k in ks]
    gcs, dmats, kbs, betas, ns = [], [], [], [], []
    for j in heads:
        hd = hg * DN_HB + j
        neg_a = prm_ref[0, hd]
        dtb = prm_ref[1, hd]
        g_col = neg_a * jax.nn.softplus(ac_ref[0, 0, :, j:j + 1] + dtb)
        g_row = neg_a * jax.nn.softplus(ar_ref[0, 0, 0, j:j + 1, :] + dtb)
        beta = jax.nn.sigmoid(bc_ref[0, 0, :, j:j + 1])
        gc_col = jnp.sum(jnp.where(incl, g_row, 0.0), axis=1, keepdims=True)
        gc_row = jnp.sum(jnp.where(incl_t, g_col, 0.0), axis=0, keepdims=True)
        dmat = jnp.where(incl, jnp.exp(gc_col - gc_row), 0.0)
        kb = ks[j] * beta
        kk = lax.dot_general(kb.astype(BF16), kbfs[j], nt, preferred_element_type=F32)
        gcs.append(gc_col)
        dmats.append(dmat)
        kbs.append(kb)
        betas.append(beta)
        ns.append(jnp.where(strict, -(kk * dmat), 0.0))
    tinvs = [eye + n for n in ns]
    npows = ns
    for _ in range(DN_NEUMANN):
        nbs = [n.astype(BF16) for n in npows]
        npows = [jnp.dot(nb, nb, preferred_element_type=F32) for nb in nbs]
        tinvs = [t + jnp.dot(t.astype(BF16), n.astype(BF16), preferred_element_type=F32)
                 for t, n in zip(tinvs, npows)]
    egs = [jnp.exp(gc) for gc in gcs]
    sols = [jnp.dot(tinvs[j].astype(BF16),
                    jnp.concatenate([v_ref[0, j] * betas[j], kbs[j] * egs[j]], axis=1).astype(BF16),
                    preferred_element_type=F32) for j in heads]
    qks = [lax.dot_general(qs[j].astype(BF16), kbfs[j], nt, preferred_element_type=F32) * dmats[j]
           for j in heads]
    ss = [s_s[j] for j in heads]
    sbs = [s.astype(BF16) for s in ss]
    vnbs = [(sols[j][:, :DN_DV]
             - jnp.dot(sols[j][:, DN_DV:].astype(BF16), sbs[j], preferred_element_type=F32)).astype(BF16)
            for j in heads]
    for j in heads:
        o_ref[0, j] = (jnp.dot((qs[j] * egs[j]).astype(BF16), sbs[j], preferred_element_type=F32)
                       + jnp.dot(qks[j].astype(BF16), vnbs[j], preferred_element_type=F32))
    for j in heads:
        gend = gcs[j][end:end + 1, :]
        k_tail = (ks[j] * jnp.exp(gend - gcs[j])).astype(BF16)
        s_new = ss[j] * jnp.exp(gend) + lax.dot_general(k_tail, vnbs[j], tn, preferred_element_type=F32)
        s_s[j] = s_new
        sf_ref[0, j] = s_new


def dn_delta(qkvh, a_col, b_col, a_row, prm, s0, rev):
    _, bn, _, seq_len, _ = qkvh.shape
    qn = DN_CHUNK
    nc = seq_len // qn
    cidx = (lambda c: nc - 1 - c) if rev else (lambda c: c)
    seq = lambda part: pl.BlockSpec((None, 1, DN_HB, qn, DN_DK),
                                    lambda b, g, c, part=part: (part, b, g, cidx(c), 0))
    col = pl.BlockSpec((1, 1, qn, DN_HB), lambda b, g, c: (b, g, cidx(c), 0))
    row = pl.BlockSpec((1, 1, 1, DN_HB, qn), lambda b, g, c: (b, g, cidx(c), 0, 0))
    st = pl.BlockSpec((1, DN_HB, DN_DK, DN_DV), lambda b, g, c: (b, g, 0, 0))
    return pl.pallas_call(
        functools.partial(_dn_delta_body, rev=rev),
        grid=(bn, DN_H // DN_HB, nc),
        in_specs=[pl.BlockSpec(memory_space=pltpu.SMEM), seq(0), seq(1), seq(2), col, col, row, st],
        out_specs=[pl.BlockSpec((1, DN_HB, qn, DN_DV), lambda b, g, c: (b, g, cidx(c), 0)), st],
        out_shape=[jax.ShapeDtypeStruct((bn, DN_H, seq_len, DN_DV), F32),
                   jax.ShapeDtypeStruct((bn, DN_H, DN_DK, DN_DV), F32)],
        scratch_shapes=[pltpu.VMEM((DN_HB, DN_DK, DN_DV), F32)],
        compiler_params=pltpu.CompilerParams(
            dimension_semantics=("parallel", "parallel", "arbitrary"),
            vmem_limit_bytes=V7X_VMEM_LIMIT),
        name="dn_delta_bwd" if rev else "dn_delta_fwd",
    )(prm, qkvh, qkvh, qkvh, a_col, b_col, a_row, s0)


def _dn_combine_body(of_ref, ob_ref, z_ref, g_ref, o_ref):
    for h in range(DN_H):
        o = of_ref[0, h] + ob_ref[0, h]
        y = o * lax.rsqrt(jnp.mean(o * o, axis=-1, keepdims=True) + EPS) * g_ref[...]
        o_ref[0, :, DN_DV * h:DN_DV * (h + 1)] = y * jax.nn.silu(z_ref[0, :, DN_DV * h:DN_DV * (h + 1)])


def dn_combine(of, ob, z, norm_g):
    bn, _, seq_len, _ = of.shape
    tt = min(DN_TT, seq_len)
    hm = pl.BlockSpec((1, DN_H, tt, DN_DV), lambda b, i: (b, 0, i, 0))
    tm = pl.BlockSpec((1, tt, DN_H * DN_DV), lambda b, i: (b, i, 0))
    return pl.pallas_call(
        _dn_combine_body,
        grid=(bn, seq_len // tt),
        in_specs=[hm, hm, tm, pl.BlockSpec((1, DN_DV), lambda b, i: (0, 0))],
        out_specs=tm,
        out_shape=jax.ShapeDtypeStruct((bn, seq_len, DN_H * DN_DV), F32),
        compiler_params=pltpu.CompilerParams(
            dimension_semantics=("parallel", "parallel"), vmem_limit_bytes=V7X_VMEM_LIMIT),
        name="dn_combine",
    )(of, ob, z, norm_g.reshape(1, DN_DV))


def deltanet_stream(qkv, z, a_raw, b_raw, state, conv_w, conv_b, dt_bias, a_log, norm_g):
    bn, seq_len, _ = z.shape
    qkv = jax.nn.silu(dwconv(qkv, conv_w, conv_b))
    q, k, v = jnp.split(qkv, [DN_H * DN_DK, 2 * DN_H * DN_DK], axis=-1)
    q = l2norm(q.reshape(bn, seq_len, DN_H, DN_DK)) * (DN_DK ** -0.5)
    k = l2norm(k.reshape(bn, seq_len, DN_H, DN_DK))
    v = v.reshape(bn, seq_len, DN_H, DN_DV)
    beta = jax.nn.sigmoid(b_raw.reshape(bn, seq_len, 2, DN_H))
    g = -jnp.exp(a_log) * jax.nn.softplus(a_raw.reshape(bn, seq_len, 2, DN_H) + dt_bias)
    of, sf = gated_delta_rule(q, k, v, beta[:, :, 0], g[:, :, 0], state[0])
    ob, sb = gated_delta_rule(flip(q), flip(k), flip(v), flip(beta[:, :, 1]), flip(g[:, :, 1]), state[1])
    o = rmsnorm(of + flip(ob), norm_g)
    o = o.reshape(bn, seq_len, DN_H * DN_DV) * jax.nn.silu(z)
    return o, (sf, sb)


def deltanet_stream_pallas(qkv, z, a_raw, b_raw, state, conv_w, conv_b, dt_bias, a_log, norm_g):
    bn, seq_len, _ = z.shape
    qkvh = dn_prep(qkv, conv_w, conv_b)
    ng = DN_H // DN_HB
    nc = seq_len // DN_CHUNK
    outs = []
    for dr in range(2):
        a = a_raw.reshape(bn, seq_len, 2, ng, DN_HB)[:, :, dr]
        b = b_raw.reshape(bn, seq_len, 2, ng, DN_HB)[:, :, dr]
        a_col = jnp.transpose(a, (0, 2, 1, 3))
        b_col = jnp.transpose(b, (0, 2, 1, 3))
        a_row = jnp.transpose(a.reshape(bn, nc, DN_CHUNK, ng, DN_HB), (0, 3, 1, 4, 2))
        prm = jnp.stack([-jnp.exp(a_log[dr]), dt_bias[dr]]).astype(F32)
        outs.append(dn_delta(qkvh, a_col, b_col, a_row, prm, state[dr], dr == 1))
    (of, sf), (ob, sb) = outs
    return dn_combine(of, ob, z, norm_g), (sf, sb)


RG_TT = 256


def _rglru_body(xp_ref, x_ref, xn_ref, h0_ref, cw_ref, cb_ref, wa_ref, ba_ref, wx_ref, bx_ref, c_ref,
                h_ref, hf_ref, carry_s, *, rev, tt, nblk):
    i = pl.program_id(1)
    blk = (nblk - 1 - i) if rev else i

    @pl.when(i == 0)
    def _():
        carry_s[...] = h0_ref[0]

    x = x_ref[0]
    xp = xp_ref[0, tt - V7X_SUBLANES:, :] * (blk > 0).astype(F32)
    xn = xn_ref[0, :V7X_SUBLANES, :] * (blk < nblk - 1).astype(F32)
    xc = jnp.concatenate([xp, x, xn], axis=0)
    base = V7X_SUBLANES - CONV_PAD[0]
    xh = cb_ref[...] + sum(cw_ref[j:j + 1, :] * xc[base + j:base + j + tt, :] for j in range(CONV_W))

    xb = xh.astype(BF16)
    rs, is_ = [], []
    for h in range(RG_H):
        xs = xb[:, RG_BS * h:RG_BS * (h + 1)]
        rs.append(jnp.dot(xs, wa_ref[h], preferred_element_type=F32))
        is_.append(jnp.dot(xs, wx_ref[h], preferred_element_type=F32))
    r = jax.nn.sigmoid(jnp.concatenate(rs, axis=1) + ba_ref[...])
    ig = jax.nn.sigmoid(jnp.concatenate(is_, axis=1) + bx_ref[...])
    log_a = r * c_ref[...]
    a = jnp.exp(log_a)
    b = jnp.sqrt(1.0 - jnp.exp(2.0 * log_a)) * (ig * xh)

    row = lax.broadcasted_iota(jnp.int32, (tt, RG_W), 0)
    s = 1
    while s < tt:
        valid = (row < tt - s) if rev else (row >= s)
        shift = (tt - s) if rev else s
        a_s = jnp.where(valid, pltpu.roll(a, shift, 0), 1.0)
        b_s = jnp.where(valid, pltpu.roll(b, shift, 0), 0.0)
        b = a * b_s + b
        a = a * a_s
        s *= 2
    hcur = b + a * carry_s[...]
    h_ref[0] = hcur
    last = 0 if rev else tt - 1
    carry_s[...] = hcur[last:last + 1, :]
    hf_ref[0] = hcur[last:last + 1, :]


def rglru_scan(xr, h0, conv_w, conv_b, wa, ba, wx, bx, lam, rev):
    bn, seq_len, _ = xr.shape
    tt = min(RG_TT, seq_len)
    nblk = seq_len // tt
    blk = (lambda i: nblk - 1 - i) if rev else (lambda i: i)
    cur = lambda b, i: (b, blk(i), 0)
    prev = lambda b, i: (b, jnp.maximum(blk(i) - 1, 0), 0)
    nxt = lambda b, i: (b, jnp.minimum(blk(i) + 1, nblk - 1), 0)
    vec = pl.BlockSpec((1, RG_W), lambda b, i: (0, 0))
    wspec = pl.BlockSpec((RG_H, RG_BS, RG_BS), lambda b, i: (0, 0, 0))
    st = pl.BlockSpec((1, 1, RG_W), lambda b, i: (b, 0, 0))
    c = (-RG_C * jax.nn.softplus(-lam)).reshape(1, RG_W)
    h, hf = pl.pallas_call(
        functools.partial(_rglru_body, rev=rev, tt=tt, nblk=nblk),
        grid=(bn, nblk),
        in_specs=[pl.BlockSpec((1, tt, RG_W), prev), pl.BlockSpec((1, tt, RG_W), cur),
                  pl.BlockSpec((1, tt, RG_W), nxt), st,
                  pl.BlockSpec((CONV_W, RG_W), lambda b, i: (0, 0)), vec,
                  wspec, vec, wspec, vec, vec],
        out_specs=[pl.BlockSpec((1, tt, RG_W), cur), st],
        out_shape=[jax.ShapeDtypeStruct((bn, seq_len, RG_W), F32),
                   jax.ShapeDtypeStruct((bn, 1, RG_W), F32)],
        scratch_shapes=[pltpu.VMEM((1, RG_W), F32)],
        compiler_params=pltpu.CompilerParams(
            dimension_semantics=("parallel", "arbitrary"),
            vmem_limit_bytes=V7X_VMEM_LIMIT),
        name="rglru_bwd" if rev else "rglru_fwd",
    )(xr, xr, xr, h0.reshape(bn, 1, RG_W), conv_w, conv_b.reshape(1, RG_W),
      wa.astype(BF16), ba.reshape(1, RG_W), wx.astype(BF16), bx.reshape(1, RG_W), c)
    return h, hf.reshape(bn, RG_W)


def _rg_combine_body(hf_ref, hb_ref, g_ref, o_ref):
    o_ref[...] = (hf_ref[...] + hb_ref[...]) * jax.nn.gelu(g_ref[...])


def rg_combine(hf, hb, gate):
    m = hf.shape[0]
    tm = _pick(m, (1024, 512, 256, 128, 64, 32, 16, 8))
    row = pl.BlockSpec((tm, RG_W), lambda i: (i, 0))
    return pl.pallas_call(
        _rg_combine_body, grid=(m // tm,), in_specs=[row, row, row], out_specs=row,
        out_shape=jax.ShapeDtypeStruct((m, RG_W), F32),
        compiler_params=pltpu.CompilerParams(dimension_semantics=("parallel",)),
        name="rg_combine",
    )(hf, hb, gate)


def rglru_stream(xr, gate, state, conv_w, conv_b, wa, ba, wx, bx, lam):
    bn, seq_len, _ = xr.shape
    hf, sf = rglru_scan(xr, state[0], conv_w, conv_b, wa[0], ba[0], wx[0], bx[0], lam[0], False)
    hb, sb = rglru_scan(xr, state[1], conv_w, conv_b, wa[1], ba[1], wx[1], bx[1], lam[1], True)
    m = bn * seq_len
    y = rg_combine(hf.reshape(m, RG_W), hb.reshape(m, RG_W), gate.reshape(m, RG_W))
    return y.reshape(bn, seq_len, RG_W), (sf, sb)


def even_mixer(hx, hc, w_in, w_out, s5_params, ssd_params, need_ctx):
    bn = hx.shape[0]

    def stream(h, st_s5, st_ssd):
        p = mm3(h, w_in)
        u, z, xbc, dt_raw = jnp.split(p, EV_CUTS, axis=-1)
        ya, st_a = s5_stream(u, st_s5, *s5_params)
        yb, st_b = ssd_stream(z, xbc, dt_raw, st_ssd, *ssd_params)
        return jnp.concatenate([ya, yb], axis=-1), st_a, st_b
    zs5 = jnp.zeros((bn, S5_LANES), F32)
    zssd = jnp.zeros((bn, SSD_H, SSD_P, SSD_N), F32)
    yc, st_a, st_b = stream(hc, (zs5, zs5, zs5, zs5), (zssd, zssd))
    yx, _, _ = stream(hx, st_a, st_b)
    return mm3(yx, w_out), (mm3(yc, w_out) if need_ctx else None)


def odd_mixer(hx, hc, w_in, w_out, dn_params, rg_params, need_ctx):
    bn = hx.shape[0]

    def stream(h, st_dn, st_rg):
        p = mm3(h, w_in)
        qkv, z, a_raw, b_raw, xr, gate = jnp.split(p, OD_CUTS, axis=-1)
        yd, st_d = deltanet_stream(qkv, z, a_raw, b_raw, st_dn, *dn_params)
        yr, st_r = rglru_stream(xr, gate, st_rg, *rg_params)
        return jnp.concatenate([yd, yr], axis=-1), st_d, st_r
    zdn = jnp.zeros((bn, DN_H, DN_DK, DN_DV), F32)
    zrg = jnp.zeros((bn, RG_W), F32)
    yc, st_d, st_r = stream(hc, (zdn, zdn), (zrg, zrg))
    yx, _, _ = stream(hx, st_d, st_r)
    return mm3(yx, w_out), (mm3(yc, w_out) if need_ctx else None)


PEER_TM = 512
PEER_EB = 512
PEER_RT = 16
PEER_SEL = PEER_H * PEER_TOPK


def _peer_body(flags_ref, ht_ref, u_ref, vt_ref, s1_ref, s2_ref, e1_ref, e2_ref, tau_ref,
               pidx_ref, ridx_ref, gate_ref, o_ref, st_s, g_s, *, tm, eb):
    i = pl.program_id(0)
    j = pl.program_id(1)
    npk = eb // PEER_NK

    @pl.when(j == 0)
    def _():
        o_ref[...] = jnp.zeros_like(o_ref)

    @pl.when(flags_ref[i] == 0)
    def _():
        st_s[...] = jnp.dot(u_ref[...], ht_ref[...], preferred_element_type=F32)
        for lg in range(tm // V7X_LANES):
            ls = slice(V7X_LANES * lg, V7X_LANES * (lg + 1))
            t_rows = [tau_ref[h:h + 1, ls] for h in range(PEER_H)]
            a_tiles = [s1_ref[j * npk + pk, :, ls] for pk in range(npk)]
            z_tiles = [e1_ref[j * npk + pk, :, ls] for pk in range(npk)]
            for rt in range(PEER_NK // PEER_RT):
                rs = slice(rt * PEER_RT, (rt + 1) * PEER_RT)
                s2 = [s2_ref[h, rs, ls] for h in range(PEER_H)]
                e2 = [e2_ref[h, rs, ls] for h in range(PEER_H)]
                for pk in range(npk):
                    acc = jnp.zeros((PEER_RT, V7X_LANES), F32)
                    for h in range(PEER_H):
                        hit = a_tiles[pk][h:h + 1, :] + s2[h] >= t_rows[h]
                        acc = acc + jnp.where(hit, z_tiles[pk][h:h + 1, :] * e2[h], 0.0)
                    rows = slice(pk * PEER_NK + rt * PEER_RT, pk * PEER_NK + (rt + 1) * PEER_RT)
                    g_s[rows, ls] = (acc * jax.nn.gelu(st_s[rows, ls])).astype(BF16)
        o_ref[...] += jnp.dot(vt_ref[...], g_s[...], preferred_element_type=F32)

    @pl.when(flags_ref[i] != 0)
    def _():
        st_s[...] = jnp.dot(u_ref[...], ht_ref[...], preferred_element_type=F32)
        riota = lax.broadcasted_iota(jnp.int32, (PEER_NK, V7X_LANES), 0)
        for lg in range(tm // V7X_LANES):
            ls = slice(V7X_LANES * lg, V7X_LANES * (lg + 1))
            for pk in range(npk):
                p = j * npk + pk

                def kbody(k8, w, ls=ls, p=p):
                    k0 = pl.multiple_of(k8 * V7X_SUBLANES, V7X_SUBLANES)
                    ptile = pidx_ref[pl.ds(k0, V7X_SUBLANES), ls]
                    rtile = ridx_ref[pl.ds(k0, V7X_SUBLANES), ls]
                    gtile = gate_ref[pl.ds(k0, V7X_SUBLANES), ls]
                    ctile = jnp.where(ptile == p, gtile, 0.0)
                    for k in range(V7X_SUBLANES):
                        w = w + jnp.where(rtile[k:k + 1, :] == riota, ctile[k:k + 1, :], 0.0)
                    return w

                w = lax.fori_loop(0, PEER_SEL // V7X_SUBLANES, kbody,
                                  jnp.zeros((PEER_NK, V7X_LANES), F32))
                rows = slice(pk * PEER_NK, (pk + 1) * PEER_NK)
                g_s[rows, ls] = (w * jax.nn.gelu(st_s[rows, ls])).astype(BF16)
        o_ref[...] += jnp.dot(vt_ref[...], g_s[...], preferred_element_type=F32)


def peer_experts(flags, ht, u_bf, vt_bf, s1t, s2t, e1t, e2t, taut, pidx, ridx, gate, tm):
    dm, t = ht.shape
    eb = PEER_EB
    head3 = pl.BlockSpec((PEER_H, PEER_NK, tm), lambda i, j, f: (0, 0, i))
    key3 = pl.BlockSpec((PEER_NK, PEER_H, tm), lambda i, j, f: (0, 0, i))
    sel2 = pl.BlockSpec((PEER_SEL, tm), lambda i, j, f: (0, i))
    grid_spec = pltpu.PrefetchScalarGridSpec(
        num_scalar_prefetch=1,
        grid=(t // tm, PEER_E // eb),
        in_specs=[pl.BlockSpec((dm, tm), lambda i, j, f: (0, i)),
                  pl.BlockSpec((eb, dm), lambda i, j, f: (j, 0)),
                  pl.BlockSpec((dm, eb), lambda i, j, f: (0, j)),
                  key3, head3, key3, head3,
                  pl.BlockSpec((PEER_H, tm), lambda i, j, f: (0, i)),
                  sel2, sel2, sel2],
        out_specs=pl.BlockSpec((dm, tm), lambda i, j, f: (0, i)),
        scratch_shapes=[pltpu.VMEM((eb, tm), F32), pltpu.VMEM((eb, tm), BF16)])
    return pl.pallas_call(
        functools.partial(_peer_body, tm=tm, eb=eb),
        grid_spec=grid_spec,
        out_shape=jax.ShapeDtypeStruct((dm, t), F32),
        compiler_params=pltpu.CompilerParams(
            dimension_semantics=("parallel", "arbitrary"),
            vmem_limit_bytes=V7X_VMEM_LIMIT),
        name="peer_experts",
    )(flags, ht, u_bf, vt_bf, s1t, s2t, e1t, e2t, taut, pidx, ridx, gate)


def _topk_rows(x, k):
    n = x.shape[0]
    iota = lax.broadcasted_iota(jnp.int32, x.shape, 0)
    vals, idxs = [], []
    for r in range(k):
        m = jnp.max(x, axis=0, keepdims=True)
        idx = jnp.min(jnp.where(x == m, iota, n), axis=0, keepdims=True)
        vals.append(m)
        idxs.append(idx)
        if r < k - 1:
            x = jnp.where(iota == idx, -jnp.inf, x)
    return vals, idxs


def _peer_select_body(q_ref, k1_ref, k2_ref, s1_ref, s2_ref, e1_ref, e2_ref, tau_ref, tie_ref,
                      pidx_ref, ridx_ref, gate_ref):
    qb = q_ref[...].astype(BF16)
    dn = (((1,), (1,)), ((), ()))
    s1 = lax.dot_general(k1_ref[0], qb[:, :PEER_HALF], dn, preferred_element_type=F32)
    s2 = lax.dot_general(k2_ref[0], qb[:, PEER_HALF:], dn, preferred_element_type=F32)
    t1, i1 = _topk_rows(s1, PEER_TOPK + 1)
    t2, i2 = _topk_rows(s2, PEER_TOPK + 1)
    t2blk = jnp.concatenate(t2[:PEER_TOPK], axis=0)
    cand = jnp.concatenate([t1[a] + t2blk for a in range(PEER_TOPK)], axis=0)
    top, ci = _topk_rows(cand, PEER_TOPK)
    topb = jnp.concatenate(top, axis=0)
    cib = jnp.concatenate(ci, axis=0)
    ca = cib // PEER_TOPK
    cb = cib % PEER_TOPK
    pid = jnp.zeros_like(cib)
    rid = jnp.zeros_like(cib)
    for a in range(PEER_TOPK):
        pid = pid + jnp.where(ca == a, i1[a], 0)
        rid = rid + jnp.where(cb == a, i2[a], 0)
    ex = jnp.exp(topb - top[0])
    zsum = jnp.sum(ex, axis=0, keepdims=True)
    tau = top[PEER_TOPK - 1]
    cnt = jnp.sum((cand >= tau).astype(jnp.int32), axis=0, keepdims=True)
    tie = ((cnt != PEER_TOPK) | (t1[PEER_TOPK] + t2[0] >= tau) | (t1[0] + t2[PEER_TOPK] >= tau))
    s1_ref[0] = s1
    s2_ref[0] = s2
    e1_ref[0] = jnp.exp(s1 - t1[0]) / zsum
    e2_ref[0] = jnp.exp(s2 - t2[0])
    tau_ref[0] = tau
    tie_ref[0] = tie.astype(jnp.int32)
    pidx_ref[...] = pid
    ridx_ref[...] = rid
    gate_ref[...] = ex / zsum


def peer_select(q, k1_bf, k2_bf):
    t = q.shape[0]
    tl = V7X_LANES
    head3 = pl.BlockSpec((1, PEER_NK, tl), lambda i, h: (h, 0, i))
    row3 = pl.BlockSpec((1, 1, tl), lambda i, h: (h, 0, i))
    sel2 = pl.BlockSpec((PEER_TOPK, tl), lambda i, h: (h, i))
    key_spec = pl.BlockSpec((1, PEER_NK, PEER_HALF), lambda i, h: (h, 0, 0))
    f3 = jax.ShapeDtypeStruct((PEER_H, PEER_NK, t), F32)
    return pl.pallas_call(
        _peer_select_body,
        grid=(t // tl, PEER_H),
        in_specs=[pl.BlockSpec((tl, PEER_DK), lambda i, h: (i, h)), key_spec, key_spec],
        out_specs=[head3, head3, head3, head3, row3, row3, sel2, sel2, sel2],
        out_shape=[f3, f3, f3, f3,
                   jax.ShapeDtypeStruct((PEER_H, 1, t), F32),
                   jax.ShapeDtypeStruct((PEER_H, 1, t), jnp.int32),
                   jax.ShapeDtypeStruct((PEER_SEL, t), jnp.int32),
                   jax.ShapeDtypeStruct((PEER_SEL, t), jnp.int32),
                   jax.ShapeDtypeStruct((PEER_SEL, t), F32)],
        compiler_params=pltpu.CompilerParams(
            dimension_semantics=("parallel", "arbitrary"),
            vmem_limit_bytes=V7X_VMEM_LIMIT),
        name="peer_select",
    )(q, k1_bf, k2_bf)


def peer(h, w_q, k1, k2, u_bf, vt_bf, tm=PEER_TM):
    bn, seq_len, dm = h.shape
    t = bn * seq_len
    q = mm(h.reshape(t, dm), w_q)
    s1t, s2t, e1t, e2t, taut, tiet, pidx, ridx, gate = peer_select(q, k1.astype(BF16), k2.astype(BF16))
    flags = jnp.any(tiet.reshape(PEER_H, t // tm, tm) != 0, axis=(0, 2)).astype(jnp.int32)
    outt = peer_experts(
        flags, h.reshape(t, dm).T.astype(BF16), u_bf, vt_bf,
        jnp.transpose(s1t, (1, 0, 2)), s2t, jnp.transpose(e1t, (1, 0, 2)), e2t, taut.reshape(PEER_H, t),
        pidx, ridx, gate, tm)
    return outt.T.reshape(bn, seq_len, dm)


def kernel(x, c, ctx, c_ctx, ada_w, ada_b, norm1_g, norm2_g, final_g, ev_w_in, ev_w_out, s5_lam_re, s5_lam_im, s5_log_dt, s5_b_re, s5_b_im, s5_c_re, s5_c_im, s5_d, s5_glu_w, s5_glu_b, ssd_conv_w, ssd_conv_b, ssd_dt_bias, ssd_a_log, ssd_d, ssd_norm_g, od_w_in, od_w_out, dn_conv_w, dn_conv_b, dn_dt_bias, dn_a_log, dn_norm_g, rg_conv_w, rg_conv_b, rg_wa, rg_ba, rg_wx, rg_bx, rg_lam, peer_wq, peer_k1, peer_k2, peer_u, peer_v):
    bn, seq_len, _ = x.shape
    rows = seq_len // GRID_W
    sc = jax.nn.silu(c)
    scc = jax.nn.silu(c_ctx)
    for layer in range(DEPTH):
        j = layer // 2
        need_ctx = layer < DEPTH - 1
        mx = (sc @ ada_w[layer] + ada_b[layer]).reshape(bn, 6, 1, D_MODEL)
        mc = (scc @ ada_w[layer] + ada_b[layer]).reshape(6, D_MODEL)
        hx = modulate(rmsnorm(x, norm1_g[layer]), mx[:, 0], mx[:, 1])
        hc = modulate(rmsnorm(ctx, norm1_g[layer]), mc[0], mc[1])
        if layer % 2 == 0:
            s5_params = (s5_lam_re[j], s5_lam_im[j], s5_log_dt[j], s5_b_re[j], s5_b_im[j],
                         s5_c_re[j], s5_c_im[j], s5_d[j], s5_glu_w[j], s5_glu_b[j])
            ssd_params = (ssd_conv_w[j], ssd_conv_b[j], ssd_dt_bias[j], ssd_a_log[j],
                          ssd_d[j], ssd_norm_g[j])
            ox, oc = even_mixer(hx, hc, ev_w_in[j], ev_w_out[j], s5_params, ssd_params, need_ctx)
        else:
            dn_params = (dn_conv_w[j], dn_conv_b[j], dn_dt_bias[j], dn_a_log[j], dn_norm_g[j])
            rg_params = (rg_conv_w[j], rg_conv_b[j], rg_wa[j], rg_ba[j], rg_wx[j], rg_bx[j], rg_lam[j])
            ox, oc = odd_mixer(to_col_major(hx, rows), hc, od_w_in[j], od_w_out[j],
                               dn_params, rg_params, need_ctx)
            ox = to_row_major(ox, rows)
        x = x + mx[:, 2] * ox
        hx = modulate(rmsnorm(x, norm2_g[layer]), mx[:, 3], mx[:, 4])
        u_bf = peer_u[layer].astype(BF16)
        vt_bf = peer_v[layer].T.astype(BF16)
        x = x + mx[:, 5] * peer(hx, peer_wq[layer], peer_k1[layer], peer_k2[layer], u_bf, vt_bf)
        if need_ctx:
            ctx = ctx + mc[2] * oc
            hc = modulate(rmsnorm(ctx, norm2_g[layer]), mc[3], mc[4])
            ctx = ctx + mc[5] * peer(hc, peer_wq[layer], peer_k1[layer], peer_k2[layer], u_bf, vt_bf)
    return rmsnorm(x, final_g)
```

```python
import functools
import math

import jax
import jax.numpy as jnp
from jax import lax
from jax.experimental import pallas as pl
from jax.experimental.pallas import tpu as pltpu

D_MODEL = 2048
DEPTH = 2
GRID_W = 64
EPS = 1e-6
CONV_W = 4
CONV_PAD = (CONV_W // 2, CONV_W - 1 - CONV_W // 2)

S5_W = D_MODEL // 2
S5_CH = 16
S5_G = S5_W // S5_CH
S5_N = 64
SSD_P = 64
SSD_H = D_MODEL // SSD_P
SSD_DI = SSD_H * SSD_P
SSD_N = 128
SSD_G = 4
SSD_CHUNK = 128
SSD_CONV_CH = SSD_DI + 2 * SSD_G * SSD_N
EV_IN = S5_W + SSD_DI + SSD_CONV_CH + 2 * SSD_H
EV_MIX = S5_W + SSD_DI
DN_DK = 128
DN_DV = 128
DN_H = D_MODEL // DN_DK
DN_CHUNK = 64
DN_QKV = DN_H * (2 * DN_DK + DN_DV)
RG_W = D_MODEL // 2
RG_H = 8
RG_BS = RG_W // RG_H
RG_C = 8.0
OD_IN = DN_QKV + DN_H * DN_DV + 4 * DN_H + 2 * RG_W
OD_MIX = DN_H * DN_DV + RG_W
OD_CUTS = [DN_QKV,
           DN_QKV + DN_H * DN_DV,
           DN_QKV + DN_H * DN_DV + 2 * DN_H,
           DN_QKV + DN_H * DN_DV + 4 * DN_H,
           DN_QKV + DN_H * DN_DV + 4 * DN_H + RG_W]
EV_CUTS = [S5_W, S5_W + SSD_DI, S5_W + SSD_DI + SSD_CONV_CH]
PEER_H = 8
PEER_DK = 256
PEER_HALF = PEER_DK // 2
PEER_NK = 128
PEER_E = PEER_NK * PEER_NK
PEER_TOPK = 16
PEER_BLOCK = 128

F32 = jnp.float32
BF16 = jnp.bfloat16

V7X_LANES = 128
V7X_SUBLANES = 8
V7X_VMEM_LIMIT = 56 * 1024 * 1024


def _mm_body(x_ref, w_ref, o_ref):
    o_ref[...] = jnp.dot(x_ref[...], w_ref[...], preferred_element_type=F32).astype(o_ref.dtype)


def _pick(n, cands):
    for c in cands:
        if n % c == 0:
            return c
    raise ValueError(f"no tile for {n}")


def mm(x, w, out_dtype=F32):
    m, k = x.shape
    n = w.shape[1]
    n_pad = -n % V7X_LANES
    if n_pad:
        w = jnp.pad(w, ((0, 0), (0, n_pad)))
    np_ = n + n_pad
    tm = _pick(m, (1024, 512, 256, 128, 64, 32, 16, 8))
    tn = _pick(np_, (512, 256, 128))
    out = pl.pallas_call(
        _mm_body,
        grid=(m // tm, np_ // tn),
        in_specs=[pl.BlockSpec((tm, k), lambda i, j: (i, 0)),
                  pl.BlockSpec((k, tn), lambda i, j: (0, j))],
        out_specs=pl.BlockSpec((tm, tn), lambda i, j: (i, j)),
        out_shape=jax.ShapeDtypeStruct((m, np_), out_dtype),
        compiler_params=pltpu.CompilerParams(
            dimension_semantics=("parallel", "arbitrary"),
            vmem_limit_bytes=V7X_VMEM_LIMIT),
        name="mm",
    )(x.astype(BF16), w.astype(BF16))
    return out[:, :n] if n_pad else out


def mm3(h, w):
    b, l, d = h.shape
    return mm(h.reshape(b * l, d), w).reshape(b, l, w.shape[1])


def rmsnorm(x, g):
    xf = x.astype(F32)
    y = xf * lax.rsqrt(jnp.mean(xf * xf, axis=-1, keepdims=True) + EPS)
    return (y * g.astype(F32)).astype(x.dtype)


def l2norm(x):
    return x * lax.rsqrt(jnp.sum(x * x, axis=-1, keepdims=True) + EPS)


def modulate(h, shift, scale):
    return h * (1 + scale) + shift


def dwconv(x, w, b):
    y = lax.conv_general_dilated(x, w[:, None, :], (1,), [CONV_PAD],
                                 dimension_numbers=('NWC', 'WIO', 'NWC'),
                                 feature_group_count=x.shape[-1])
    return y + b


def flip(t):
    return jnp.flip(t, axis=1)


def to_col_major(x, rows):
    b, l, d = x.shape
    return x.reshape(b, rows, GRID_W, d).transpose(0, 2, 1, 3).reshape(b, l, d)


def to_row_major(x, rows):
    b, l, d = x.shape
    return x.reshape(b, GRID_W, rows, d).transpose(0, 2, 1, 3).reshape(b, l, d)


def linear_scan(a, b, h0):
    b = b.at[:, 0].add(a[:, 0] * h0)

    def comb(e1, e2):
        return e1[0] * e2[0], e2[0] * e1[1] + e2[1]
    return lax.associative_scan(comb, (a, b), axis=1)[1]


S5_LANES = S5_G * S5_N
S5_TILE = 8
S5_SHIFTS = (1, 2, 4)
S5_BG = 8
S5_CG = 16
S5_CHUNK = 1024


def _s5_scan_body(u_ref, h0r_ref, h0i_ref, bdr_ref, bdi_ref, cdr_ref, cdi_ref, coef_ref,
                  y_ref, hfr_ref, hfi_ref, xr_s, xi_s, cr_s, ci_s, *, rev, tt):
    @pl.when(pl.program_id(1) == 0)
    def _():
        cr_s[...] = h0r_ref[0]
        ci_s[...] = h0i_ref[0]

    ub = u_ref[0].astype(BF16)
    kin = S5_BG * S5_CH
    kout = S5_BG * S5_N
    for j in range(S5_G // S5_BG):
        uj = ub[:, kin * j:kin * (j + 1)]
        xr_s[:, kout * j:kout * (j + 1)] = jnp.dot(uj, bdr_ref[j], preferred_element_type=F32)
        xi_s[:, kout * j:kout * (j + 1)] = jnp.dot(uj, bdi_ref[j], preferred_element_type=F32)

    nt = tt // S5_TILE
    carry_row = 0 if rev else S5_TILE - 1

    def tile(kk, carry):
        k = (nt - 1 - kk) if rev else kk
        r0 = pl.multiple_of(k * S5_TILE, S5_TILE)
        for c in range(S5_LANES // S5_CHUNK):
            sl = slice(S5_CHUNK * c, S5_CHUNK * (c + 1))
            xr = xr_s[pl.ds(r0, S5_TILE), sl]
            xi = xi_s[pl.ds(r0, S5_TILE), sl]
            for idx, sh in enumerate(S5_SHIFTS):
                pr = coef_ref[2 * idx, :, sl]
                pim = coef_ref[2 * idx + 1, :, sl]
                rs = (S5_TILE - sh) if rev else sh
                sr = pltpu.roll(xr, rs, 0)
                si = pltpu.roll(xi, rs, 0)
                xr, xi = xr + (pr * sr - pim * si), xi + (pr * si + pim * sr)
            pr = coef_ref[2 * len(S5_SHIFTS), :, sl]
            pim = coef_ref[2 * len(S5_SHIFTS) + 1, :, sl]
            hr = cr_s[:, sl]
            hi = ci_s[:, sl]
            xr, xi = xr + (pr * hr - pim * hi), xi + (pr * hi + pim * hr)
            xr_s[pl.ds(r0, S5_TILE), sl] = xr
            xi_s[pl.ds(r0, S5_TILE), sl] = xi
            cr_s[:, sl] = xr[carry_row:carry_row + 1, :]
            ci_s[:, sl] = xi[carry_row:carry_row + 1, :]
        return carry

    lax.fori_loop(0, nt, tile, 0)

    kc = S5_CG * S5_N
    nc = S5_CG * S5_CH
    for j in range(S5_G // S5_CG):
        sr = xr_s[:, kc * j:kc * (j + 1)].astype(BF16)
        si = xi_s[:, kc * j:kc * (j + 1)].astype(BF16)
        y_ref[0, :, nc * j:nc * (j + 1)] = (
            jnp.dot(sr, cdr_ref[j], preferred_element_type=F32)
            - jnp.dot(si, cdi_ref[j], preferred_element_type=F32))
    hfr_ref[0] = cr_s[...]
    hfi_ref[0] = ci_s[...]


def _s5_direction_params(lam_re, lam_im, log_dt, b_re, b_im, c_re, c_im, rev):
    dt = jnp.exp(log_dt)[:, None]
    mag = jnp.exp(lam_re * dt)
    ar, ai = mag * jnp.cos(lam_im * dt), mag * jnp.sin(lam_im * dt)
    den = lam_re * lam_re + lam_im * lam_im
    cr = ((ar - 1) * lam_re + ai * lam_im) / den
    ci = (ai * lam_re - (ar - 1) * lam_im) / den
    bb_re = cr[..., None] * b_re - ci[..., None] * b_im
    bb_im = cr[..., None] * b_im + ci[..., None] * b_re

    def bdiag_in(bb):
        t = bb.reshape(S5_G // S5_BG, S5_BG, S5_N, S5_CH)
        eye = jnp.eye(S5_BG, dtype=F32)
        return jnp.einsum('jgnc,gh->jgchn', t, eye).reshape(
            S5_G // S5_BG, S5_BG * S5_CH, S5_BG * S5_N).astype(BF16)

    def bdiag_out(cc):
        t = cc.reshape(S5_G // S5_CG, S5_CG, S5_CH, S5_N)
        eye = jnp.eye(S5_CG, dtype=F32)
        return jnp.einsum('jgcn,gh->jgnhc', t, eye).reshape(
            S5_G // S5_CG, S5_CG * S5_N, S5_CG * S5_CH).astype(BF16)

    a_r, a_i = ar.reshape(-1), ai.reshape(-1)
    pw = [(a_r, a_i)]
    for _ in range(S5_TILE - 1):
        pr, pim = pw[-1]
        pw.append((pr * a_r - pim * a_i, pr * a_i + pim * a_r))
    rows = jnp.arange(S5_TILE)[:, None]
    coef = []
    for sh in S5_SHIFTS:
        valid = (rows <= S5_TILE - 1 - sh) if rev else (rows >= sh)
        coef.append(jnp.where(valid, pw[sh - 1][0][None, :], 0.0))
        coef.append(jnp.where(valid, pw[sh - 1][1][None, :], 0.0))
    order = list(range(S5_TILE - 1, -1, -1)) if rev else list(range(S5_TILE))
    coef.append(jnp.stack([pw[k][0] for k in order]))
    coef.append(jnp.stack([pw[k][1] for k in order]))
    return (bdiag_in(bb_re), bdiag_in(bb_im), bdiag_out(c_re), bdiag_out(c_im),
            jnp.stack(coef).astype(F32))


def s5_scan(u, h0r, h0i, dparams, rev):
    bn, seq_len, _ = u.shape
    bdr, bdi, cdr, cdi, coef = dparams
    tt = min(256, seq_len)
    nblk = seq_len // tt
    tmap = (lambda b, i: (b, nblk - 1 - i, 0)) if rev else (lambda b, i: (b, i, 0))
    const3 = lambda b, i: (0, 0, 0)
    y, hfr, hfi = pl.pallas_call(
        functools.partial(_s5_scan_body, rev=rev, tt=tt),
        grid=(bn, nblk),
        in_specs=[pl.BlockSpec((1, tt, S5_W), tmap),
                  pl.BlockSpec((1, 1, S5_LANES), lambda b, i: (b, 0, 0)),
                  pl.BlockSpec((1, 1, S5_LANES), lambda b, i: (b, 0, 0)),
                  pl.BlockSpec(bdr.shape, const3),
                  pl.BlockSpec(bdi.shape, const3),
                  pl.BlockSpec(cdr.shape, const3),
                  pl.BlockSpec(cdi.shape, const3),
                  pl.BlockSpec(coef.shape, const3)],
        out_specs=[pl.BlockSpec((1, tt, S5_W), tmap),
                   pl.BlockSpec((1, 1, S5_LANES), lambda b, i: (b, 0, 0)),
                   pl.BlockSpec((1, 1, S5_LANES), lambda b, i: (b, 0, 0))],
        out_shape=[jax.ShapeDtypeStruct((bn, seq_len, S5_W), F32),
                   jax.ShapeDtypeStruct((bn, 1, S5_LANES), F32),
                   jax.ShapeDtypeStruct((bn, 1, S5_LANES), F32)],
        scratch_shapes=[pltpu.VMEM((tt, S5_LANES), F32), pltpu.VMEM((tt, S5_LANES), F32),
                        pltpu.VMEM((1, S5_LANES), F32), pltpu.VMEM((1, S5_LANES), F32)],
        compiler_params=pltpu.CompilerParams(
            dimension_semantics=("parallel", "arbitrary"),
            vmem_limit_bytes=V7X_VMEM_LIMIT),
        name="s5_scan_bwd" if rev else "s5_scan_fwd",
    )(u, h0r.reshape(bn, 1, S5_LANES), h0i.reshape(bn, 1, S5_LANES), bdr, bdi, cdr, cdi, coef)
    return y, hfr.reshape(bn, S5_LANES), hfi.reshape(bn, S5_LANES)


def _s5_glu_body(yf_ref, yb_ref, u_ref, d_ref, w_ref, b_ref, o_ref):
    y = yf_ref[...] + yb_ref[...] + d_ref[...] * u_ref[...]
    g = jax.nn.gelu(y)
    z = jnp.dot(g.astype(BF16), w_ref[...], preferred_element_type=F32) + b_ref[...]
    o_ref[...] = (g * jax.nn.sigmoid(z)).astype(o_ref.dtype)


def s5_glu(yf, yb, u, d, glu_w, glu_b):
    m = yf.shape[0]
    tm = _pick(m, (512, 256, 128, 64, 32, 16, 8))
    row = pl.BlockSpec((tm, S5_W), lambda i: (i, 0))
    vec = pl.BlockSpec((1, S5_W), lambda i: (0, 0))
    return pl.pallas_call(
        _s5_glu_body,
        grid=(m // tm,),
        in_specs=[row, row, row, vec, pl.BlockSpec((S5_W, S5_W), lambda i: (0, 0)), vec],
        out_specs=row,
        out_shape=jax.ShapeDtypeStruct((m, S5_W), F32),
        compiler_params=pltpu.CompilerParams(
            dimension_semantics=("parallel",), vmem_limit_bytes=V7X_VMEM_LIMIT),
        name="s5_glu",
    )(yf, yb, u, d.reshape(1, S5_W), glu_w.astype(BF16), glu_b.reshape(1, S5_W))


def s5_stream(u, state, lam_re, lam_im, log_dt, b_re, b_im, c_re, c_im, d, glu_w, glu_b):
    bn, seq_len, _ = u.shape
    pf = _s5_direction_params(lam_re[0], lam_im[0], log_dt[0], b_re[0], b_im[0], c_re[0], c_im[0], False)
    pb = _s5_direction_params(lam_re[1], lam_im[1], log_dt[1], b_re[1], b_im[1], c_re[1], c_im[1], True)
    yf, fr, fi = s5_scan(u, state[0], state[1], pf, False)
    yb, br, bi = s5_scan(u, state[2], state[3], pb, True)
    m = bn * seq_len
    out = s5_glu(yf.reshape(m, S5_W), yb.reshape(m, S5_W), u.reshape(m, S5_W), d, glu_w, glu_b)
    return out.reshape(bn, seq_len, S5_W), (fr, fi, br, bi)


def ssd_scan(x, dt, a_neg, bm, cm, h0):
    bn, seq_len, nh, hp = x.shape
    ng, ns = bm.shape[2], bm.shape[3]
    nr = nh // ng
    q = SSD_CHUNK
    nc = seq_len // q
    xdt = (x * dt[..., None]).reshape(bn, nc, q, ng, nr, hp)
    cum = jnp.cumsum((dt * a_neg).reshape(bn, nc, q, ng, nr), axis=2)
    bm = bm.reshape(bn, nc, q, ng, ns)
    cm = cm.reshape(bn, nc, q, ng, ns)
    idx = jnp.arange(q)
    incl = idx[:, None] >= idx[None, :]
    cum_t = jnp.moveaxis(cum, 2, -1)
    seg = jnp.exp(jnp.where(incl, cum_t[..., :, None] - cum_t[..., None, :], -jnp.inf))
    cb = jnp.einsum('bclgn,bcsgn->bcgls', cm, bm)
    y_diag = jnp.einsum('bcgrls,bcsgrp->bclgrp', seg * cb[:, :, :, None], xdt)

    def step(h, inp):
        c_c, b_c, xdt_c, cum_c = inp
        y_off = jnp.einsum('blgn,bgrpn->blgrp', c_c, h) * jnp.exp(cum_c)[..., None]
        dte = jnp.exp(cum_c[:, -1:] - cum_c)
        st = jnp.einsum('blgn,blgrp->bgrpn', b_c, xdt_c * dte[..., None])
        h = h * jnp.exp(cum_c[:, -1])[..., None, None] + st
        return h, y_off
    xs = tuple(jnp.moveaxis(t, 1, 0) for t in (cm, bm, xdt, cum))
    h_last, y_off = lax.scan(step, h0.reshape(bn, ng, nr, hp, ns), xs)
    y = y_diag + jnp.moveaxis(y_off, 0, 1)
    return y.reshape(bn, seq_len, nh, hp), h_last.reshape(bn, nh, hp, ns)


SSD_TT = 256
SSD_CW = 1024
SSD_R = SSD_H // SSD_G


def _conv_silu_body(xp_ref, x_ref, xn_ref, cw_ref, cb_ref, o_ref, *, tt, nblk):
    blk = pl.program_id(1)
    x = x_ref[0]
    xp = xp_ref[0, tt - V7X_SUBLANES:, :] * (blk > 0).astype(F32)
    xn = xn_ref[0, :V7X_SUBLANES, :] * (blk < nblk - 1).astype(F32)
    xc = jnp.concatenate([xp, x, xn], axis=0)
    base = V7X_SUBLANES - CONV_PAD[0]
    y = cb_ref[...] + sum(cw_ref[j:j + 1, :] * xc[base + j:base + j + tt, :] for j in range(CONV_W))
    o_ref[0] = jax.nn.silu(y)


def conv_silu(x, conv_w, conv_b):
    bn, seq_len, ch = x.shape
    tt = min(SSD_TT, seq_len)
    nblk = seq_len // tt
    cur = lambda b, i, p: (b, i, p)
    prev = lambda b, i, p: (b, jnp.maximum(i - 1, 0), p)
    nxt = lambda b, i, p: (b, jnp.minimum(i + 1, nblk - 1), p)
    return pl.pallas_call(
        functools.partial(_conv_silu_body, tt=tt, nblk=nblk),
        grid=(bn, nblk, ch // SSD_CW),
        in_specs=[pl.BlockSpec((1, tt, SSD_CW), prev), pl.BlockSpec((1, tt, SSD_CW), cur),
                  pl.BlockSpec((1, tt, SSD_CW), nxt),
                  pl.BlockSpec((CONV_W, SSD_CW), lambda b, i, p: (0, p)),
                  pl.BlockSpec((1, SSD_CW), lambda b, i, p: (0, p))],
        out_specs=pl.BlockSpec((1, tt, SSD_CW), cur),
        out_shape=jax.ShapeDtypeStruct((bn, seq_len, ch), F32),
        compiler_params=pltpu.CompilerParams(
            dimension_semantics=("parallel", "parallel", "arbitrary"),
            vmem_limit_bytes=V7X_VMEM_LIMIT),
        name="conv_silu",
    )(x, x, x, conv_w, conv_b.reshape(1, ch))


def _ssd_scan_body(prm_ref, x_ref, b_ref, c_ref, dc_ref, dr_ref, s0_ref, y_ref, sf_ref, s_s, *, rev):
    c = pl.program_id(1)

    @pl.when(c == 0)
    def _():
        s_s[...] = s0_ref[0]

    qn = SSD_CHUNK
    ri = lax.broadcasted_iota(jnp.int32, (qn, qn), 0)
    ci = lax.broadcasted_iota(jnp.int32, (qn, qn), 1)
    incl = (ri <= ci) if rev else (ri >= ci)
    incl_t = (ri >= ci) if rev else (ri <= ci)
    end = 0 if rev else qn - 1
    nt = (((1,), (1,)), ((), ()))
    tn = (((0,), (0,)), ((), ()))
    for g in range(SSD_G):
        bm = b_ref[0, :, SSD_N * g:SSD_N * (g + 1)].astype(BF16)
        cm = c_ref[0, :, SSD_N * g:SSD_N * (g + 1)].astype(BF16)
        cb = lax.dot_general(cm, bm, nt, preferred_element_type=F32)
        heads = range(SSD_R * g, SSD_R * (g + 1))
        dts, cums, segs = [], [], []
        for h in heads:
            a_neg = prm_ref[0, h]
            dtb = prm_ref[1, h]
            dt_col = jax.nn.softplus(dc_ref[0, :, h:h + 1] + dtb)
            dt_row = jax.nn.softplus(dr_ref[0, 0, h:h + 1, :] + dtb)
            cum_col = jnp.sum(jnp.where(incl, dt_row * a_neg, 0.0), axis=1, keepdims=True)
            cum_row = jnp.sum(jnp.where(incl_t, dt_col * a_neg, 0.0), axis=0, keepdims=True)
            dts.append(dt_col)
            cums.append(cum_col)
            segs.append(jnp.where(incl, jnp.exp(cum_col - cum_row), 0.0) * cb)
        xdts = [x_ref[0, :, SSD_P * h:SSD_P * (h + 1)] * dts[i] for i, h in enumerate(heads)]
        ydiags = [jnp.dot(segs[i].astype(BF16), xdts[i].astype(BF16), preferred_element_type=F32)
                  for i in range(SSD_R)]
        ss = [s_s[h] for h in heads]
        yoffs = [lax.dot_general(cm, ss[i].astype(BF16), nt, preferred_element_type=F32) * jnp.exp(cums[i])
                 for i in range(SSD_R)]
        ys = [ydiags[i] + yoffs[i] for i in range(SSD_R)]
        for i in range(0, SSD_R, 2):
            lo = SSD_P * (SSD_R * g + i)
            y_ref[0, :, lo:lo + 2 * SSD_P] = jnp.concatenate([ys[i], ys[i + 1]], axis=1)
        for i, h in enumerate(heads):
            cend = cums[i][end:end + 1, :]
            xw = (xdts[i] * jnp.exp(cend - cums[i])).astype(BF16)
            s_new = ss[i] * jnp.exp(cend) + lax.dot_general(xw, bm, tn, preferred_element_type=F32)
            s_s[h] = s_new
            sf_ref[0, h] = s_new


def ssd_scan_pallas(xs, bm, cm, dt_col, dt_row, prm, s0, rev):
    bn, seq_len, _ = xs.shape
    qn = SSD_CHUNK
    nc = seq_len // qn
    cidx = (lambda c: nc - 1 - c) if rev else (lambda c: c)
    tok = lambda w: pl.BlockSpec((1, qn, w), lambda b, c: (b, cidx(c), 0))
    st = pl.BlockSpec((1, SSD_H, SSD_P, SSD_N), lambda b, c: (b, 0, 0, 0))
    return pl.pallas_call(
        functools.partial(_ssd_scan_body, rev=rev),
        grid=(bn, nc),
        in_specs=[pl.BlockSpec(memory_space=pltpu.SMEM), tok(SSD_DI), tok(SSD_G * SSD_N), tok(SSD_G * SSD_N),
                  tok(SSD_H), pl.BlockSpec((1, 1, SSD_H, qn), lambda b, c: (b, cidx(c), 0, 0)), st],
        out_specs=[tok(SSD_DI), st],
        out_shape=[jax.ShapeDtypeStruct((bn, seq_len, SSD_DI), F32),
                   jax.ShapeDtypeStruct((bn, SSD_H, SSD_P, SSD_N), F32)],
        scratch_shapes=[pltpu.VMEM((SSD_H, SSD_P, SSD_N), F32)],
        compiler_params=pltpu.CompilerParams(
            dimension_semantics=("parallel", "arbitrary"),
            vmem_limit_bytes=V7X_VMEM_LIMIT),
        name="ssd_scan_bwd" if rev else "ssd_scan_fwd",
    )(prm, xs, bm, cm, dt_col, dt_row, s0)


def _ssd_post_body(yf_ref, yb_ref, xs_ref, z_ref, d_ref, g_ref, o_ref):
    y = (yf_ref[...] + yb_ref[...] + d_ref[...] * xs_ref[...]) * jax.nn.silu(z_ref[...])
    o_ref[...] = y * lax.rsqrt(jnp.mean(y * y, axis=-1, keepdims=True) + EPS) * g_ref[...]


def ssd_post(yf, yb, xs, z, d_vec, norm_g):
    m = yf.shape[0]
    tm = _pick(m, (SSD_TT, 128, 64, 32, 16, 8))
    row = pl.BlockSpec((tm, SSD_DI), lambda i: (i, 0))
    vec = pl.BlockSpec((1, SSD_DI), lambda i: (0, 0))
    return pl.pallas_call(
        _ssd_post_body, grid=(m // tm,), in_specs=[row, row, row, row, vec, vec], out_specs=row,
        out_shape=jax.ShapeDtypeStruct((m, SSD_DI), F32),
        compiler_params=pltpu.CompilerParams(
            dimension_semantics=("parallel",), vmem_limit_bytes=V7X_VMEM_LIMIT),
        name="ssd_post",
    )(yf, yb, xs, z, d_vec, norm_g.reshape(1, SSD_DI))


def ssd_stream(z, xbc, dt_raw, state, conv_w, conv_b, dt_bias, a_log, d, norm_g):
    bn, seq_len, _ = z.shape
    xbc = conv_silu(xbc, conv_w, conv_b)
    xs, bm, cm = jnp.split(xbc, [SSD_DI, SSD_DI + SSD_G * SSD_N], axis=-1)
    nc = seq_len // SSD_CHUNK
    dtr = dt_raw.reshape(bn, seq_len, 2, SSD_H)
    outs = []
    for dr in range(2):
        dt_col = dtr[:, :, dr]
        dt_row = jnp.transpose(dt_col.reshape(bn, nc, SSD_CHUNK, SSD_H), (0, 1, 3, 2))
        prm = jnp.stack([-jnp.exp(a_log[dr]), dt_bias[dr]]).astype(F32)
        outs.append(ssd_scan_pallas(xs, bm, cm, dt_col, dt_row, prm, state[dr], dr == 1))
    (yf, hf), (yb, hb) = outs
    m = bn * seq_len
    d_vec = jnp.repeat(d, SSD_P).reshape(1, SSD_DI)
    out = ssd_post(yf.reshape(m, SSD_DI), yb.reshape(m, SSD_DI), xs.reshape(m, SSD_DI),
                   z.reshape(m, SSD_DI), d_vec, norm_g)
    return out.reshape(bn, seq_len, SSD_DI), (hf, hb)


def gated_delta_rule(q, k, v, beta, g, s0):
    bn, seq_len, nh, dk = q.shape
    dv = v.shape[-1]
    qn = DN_CHUNK
    nc = seq_len // qn

    def chunks(t):
        return jnp.swapaxes(t.reshape((bn, nc, qn) + t.shape[2:]), 2, 3)
    qc, kc, vc, bc, gc = (chunks(t) for t in (q, k, v, beta, g))
    gcum = jnp.cumsum(gc, axis=-1)
    idx = jnp.arange(qn)
    incl = idx[:, None] >= idx[None, :]
    strict = idx[:, None] > idx[None, :]
    dmat = jnp.exp(jnp.where(incl, gcum[..., :, None] - gcum[..., None, :], -jnp.inf))
    kb = kc * bc[..., None]
    m = jnp.where(strict, jnp.einsum('bnhid,bnhjd->bnhij', kb, kc) * dmat, 0.0)
    rhs = jnp.concatenate([vc * bc[..., None], kb * jnp.exp(gcum)[..., None]], axis=-1)
    nmat = -m
    tinv = jnp.eye(qn, dtype=m.dtype) + nmat
    npow = nmat
    for _ in range(DN_NEUMANN):
        npow = jnp.einsum('...ij,...jk->...ik', npow, npow, precision=lax.Precision.HIGHEST)
        tinv = tinv + jnp.einsum('...ij,...jk->...ik', tinv, npow, precision=lax.Precision.HIGHEST)
    sol = jnp.einsum('...ij,...jk->...ik', tinv, rhs)
    u, w = sol[..., :dv], sol[..., dv:]
    qk = jnp.einsum('bnhid,bnhjd->bnhij', qc, kc) * dmat
    q_dec = qc * jnp.exp(gcum)[..., None]
    k_tail = kc * jnp.exp(gcum[..., -1:] - gcum)[..., None]
    tot = jnp.exp(gcum[..., -1])

    def step(s, inp):
        u_c, w_c, qk_c, qd_c, kt_c, tot_c = inp
        v_new = u_c - jnp.einsum('bhqk,bhkv->bhqv', w_c, s)
        o = jnp.einsum('bhqk,bhkv->bhqv', qd_c, s) + jnp.einsum('bhij,bhjv->bhiv', qk_c, v_new)
        s = s * tot_c[..., None, None] + jnp.einsum('bhqk,bhqv->bhkv', kt_c, v_new)
        return s, o
    xs = tuple(jnp.moveaxis(t, 1, 0) for t in (u, w, qk, q_dec, k_tail, tot))
    s_last, o = lax.scan(step, s0, xs)
    o = jnp.swapaxes(jnp.moveaxis(o, 0, 1), 2, 3).reshape(bn, seq_len, nh, dv)
    return o, s_last


DN_TT = 256
DN_HB = 8
DN_W = DN_H * DN_DK
DN_NEUMANN = 5


def _dn_prep_body(xp_ref, x_ref, xn_ref, cw_ref, cb_ref, o_ref, *, tt, nblk):
    blk = pl.program_id(1)
    part = pl.program_id(2)
    x = x_ref[0]
    xp = xp_ref[0, tt - V7X_SUBLANES:, :] * (blk > 0).astype(F32)
    xn = xn_ref[0, :V7X_SUBLANES, :] * (blk < nblk - 1).astype(F32)
    xc = jnp.concatenate([xp, x, xn], axis=0)
    base = V7X_SUBLANES - CONV_PAD[0]
    y = cb_ref[...] + sum(cw_ref[j:j + 1, :] * xc[base + j:base + j + tt, :] for j in range(CONV_W))
    y = jax.nn.silu(y)
    scale = jnp.where(part == 0, DN_DK ** -0.5, 1.0)
    for h in range(DN_H):
        yh = y[:, DN_DK * h:DN_DK * (h + 1)]
        yn = yh * lax.rsqrt(jnp.sum(yh * yh, axis=-1, keepdims=True) + EPS) * scale
        o_ref[0, 0, h] = jnp.where(part == 2, yh, yn)


def dn_prep(qkv, conv_w, conv_b):
    bn, seq_len, _ = qkv.shape
    tt = min(DN_TT, seq_len)
    nblk = seq_len // tt
    cur = lambda b, i, p: (b, i, p)
    prev = lambda b, i, p: (b, jnp.maximum(i - 1, 0), p)
    nxt = lambda b, i, p: (b, jnp.minimum(i + 1, nblk - 1), p)
    return pl.pallas_call(
        functools.partial(_dn_prep_body, tt=tt, nblk=nblk),
        grid=(bn, nblk, 3),
        in_specs=[pl.BlockSpec((1, tt, DN_W), prev), pl.BlockSpec((1, tt, DN_W), cur),
                  pl.BlockSpec((1, tt, DN_W), nxt),
                  pl.BlockSpec((CONV_W, DN_W), lambda b, i, p: (0, p)),
                  pl.BlockSpec((1, DN_W), lambda b, i, p: (0, p))],
        out_specs=pl.BlockSpec((1, 1, DN_H, tt, DN_DK), lambda b, i, p: (p, b, 0, i, 0)),
        out_shape=jax.ShapeDtypeStruct((3, bn, DN_H, seq_len, DN_DK), F32),
        compiler_params=pltpu.CompilerParams(
            dimension_semantics=("parallel", "parallel", "arbitrary"),
            vmem_limit_bytes=V7X_VMEM_LIMIT),
        name="dn_prep",
    )(qkv, qkv, qkv, conv_w, conv_b.reshape(1, 3 * DN_W))


def _dn_delta_body(prm_ref, q_ref, k_ref, v_ref, ac_ref, bc_ref, ar_ref, s0_ref,
                   o_ref, sf_ref, s_s, *, rev):
    hg = pl.program_id(1)
    c = pl.program_id(2)

    @pl.when(c == 0)
    def _():
        s_s[...] = s0_ref[0]

    qn = DN_CHUNK
    ri = lax.broadcasted_iota(jnp.int32, (qn, qn), 0)
    ci = lax.broadcasted_iota(jnp.int32, (qn, qn), 1)
    incl = (ri <= ci) if rev else (ri >= ci)
    incl_t = (ri >= ci) if rev else (ri <= ci)
    strict = (ri < ci) if rev else (ri > ci)
    eye = (ri == ci).astype(F32)
    end = 0 if rev else qn - 1
    nt = (((1,), (1,)), ((), ()))
    tn = (((0,), (0,)), ((), ()))
    heads = range(DN_HB)
    qs = [q_ref[0, j] for j in heads]
    ks = [k_ref[0, j] for j in heads]
    kbfs = [k.astype(BF16) for k in ks]
    gcs, dmats, kbs, betas, ns = [], [], [], [], []
    for j in heads:
        hd = hg * DN_HB + j
        neg_a = prm_ref[0, hd]
        dtb = prm_ref[1, hd]
        g_col = neg_a * jax.nn.softplus(ac_ref[0, 0, :, j:j + 1] + dtb)
        g_row = neg_a * jax.nn.softplus(ar_ref[0, 0, 0, j:j + 1, :] + dtb)
        beta = jax.nn.sigmoid(bc_ref[0, 0, :, j:j + 1])
        gc_col = jnp.sum(jnp.where(incl, g_row, 0.0), axis=1, keepdims=True)
        gc_row = jnp.sum(jnp.where(incl_t, g_col, 0.0), axis=0, keepdims=True)
        dmat = jnp.where(incl, jnp.exp(gc_col - gc_row), 0.0)
        kb = ks[j] * beta
        kk = lax.dot_general(kb.astype(BF16), kbfs[j], nt, preferred_element_type=F32)
        gcs.append(gc_col)
        dmats.append(dmat)
        kbs.append(kb)
        betas.append(beta)
        ns.append(jnp.where(strict, -(kk * dmat), 0.0))
    tinvs = [eye + n for n in ns]
    npows = ns
    for _ in range(DN_NEUMANN):
        nbs = [n.astype(BF16) for n in npows]
        npows = [jnp.dot(nb, nb, preferred_element_type=F32) for nb in nbs]
        tinvs = [t + jnp.dot(t.astype(BF16), n.astype(BF16), preferred_element_type=F32)
                 for t, n in zip(tinvs, npows)]
    egs = [jnp.exp(gc) for gc in gcs]
    sols = [jnp.dot(tinvs[j].astype(BF16),
                    jnp.concatenate([v_ref[0, j] * betas[j], kbs[j] * egs[j]], axis=1).astype(BF16),
                    preferred_element_type=F32) for j in heads]
    qks = [lax.dot_general(qs[j].astype(BF16), kbfs[j], nt, preferred_element_type=F32) * dmats[j]
           for j in heads]
    ss = [s_s[j] for j in heads]
    sbs = [s.astype(BF16) for s in ss]
    vnbs = [(sols[j][:, :DN_DV]
             - jnp.dot(sols[j][:, DN_DV:].astype(BF16), sbs[j], preferred_element_type=F32)).astype(BF16)
            for j in heads]
    for j in heads:
        o_ref[0, j] = (jnp.dot((qs[j] * egs[j]).astype(BF16), sbs[j], preferred_element_type=F32)
                       + jnp.dot(qks[j].astype(BF16), vnbs[j], preferred_element_type=F32))
    for j in heads:
        gend = gcs[j][end:end + 1, :]
        k_tail = (ks[j] * jnp.exp(gend - gcs[j])).astype(BF16)
        s_new = ss[j] * jnp.exp(gend) + lax.dot_general(k_tail, vnbs[j], tn, preferred_element_type=F32)
        s_s[j] = s_new
        sf_ref[0, j] = s_new


def dn_delta(qkvh, a_col, b_col, a_row, prm, s0, rev):
    _, bn, _, seq_len, _ = qkvh.shape
    qn = DN_CHUNK
    nc = seq_len // qn
    cidx = (lambda c: nc - 1 - c) if rev else (lambda c: c)
    seq = lambda part: pl.BlockSpec((None, 1, DN_HB, qn, DN_DK),
                                    lambda b, g, c, part=part: (part, b, g, cidx(c), 0))
    col = pl.BlockSpec((1, 1, qn, DN_HB), lambda b, g, c: (b, g, cidx(c), 0))
    row = pl.BlockSpec((1, 1, 1, DN_HB, qn), lambda b, g, c: (b, g, cidx(c), 0, 0))
    st = pl.BlockSpec((1, DN_HB, DN_DK, DN_DV), lambda b, g, c: (b, g, 0, 0))
    return pl.pallas_call(
        functools.partial(_dn_delta_body, rev=rev),
        grid=(bn, DN_H // DN_HB, nc),
        in_specs=[pl.BlockSpec(memory_space=pltpu.SMEM), seq(0), seq(1), seq(2), col, col, row, st],
        out_specs=[pl.BlockSpec((1, DN_HB, qn, DN_DV), lambda b, g, c: (b, g, cidx(c), 0)), st],
        out_shape=[jax.ShapeDtypeStruct((bn, DN_H, seq_len, DN_DV), F32),
                   jax.ShapeDtypeStruct((bn, DN_H, DN_DK, DN_DV), F32)],
        scratch_shapes=[pltpu.VMEM((DN_HB, DN_DK, DN_DV), F32)],
        compiler_params=pltpu.CompilerParams(
            dimension_semantics=("parallel", "parallel", "arbitrary"),
            vmem_limit_bytes=V7X_VMEM_LIMIT),
        name="dn_delta_bwd" if rev else "dn_delta_fwd",
    )(prm, qkvh, qkvh, qkvh, a_col, b_col, a_row, s0)


def _dn_combine_body(of_ref, ob_ref, z_ref, g_ref, o_ref):
    for h in range(DN_H):
        o = of_ref[0, h] + ob_ref[0, h]
        y = o * lax.rsqrt(jnp.mean(o * o, axis=-1, keepdims=True) + EPS) * g_ref[...]
        o_ref[0, :, DN_DV * h:DN_DV * (h + 1)] = y * jax.nn.silu(z_ref[0, :, DN_DV * h:DN_DV * (h + 1)])


def dn_combine(of, ob, z, norm_g):
    bn, _, seq_len, _ = of.shape
    tt = min(DN_TT, seq_len)
    hm = pl.BlockSpec((1, DN_H, tt, DN_DV), lambda b, i: (b, 0, i, 0))
    tm = pl.BlockSpec((1, tt, DN_H * DN_DV), lambda b, i: (b, i, 0))
    return pl.pallas_call(
        _dn_combine_body,
        grid=(bn, seq_len // tt),
        in_specs=[hm, hm, tm, pl.BlockSpec((1, DN_DV), lambda b, i: (0, 0))],
        out_specs=tm,
        out_shape=jax.ShapeDtypeStruct((bn, seq_len, DN_H * DN_DV), F32),
        compiler_params=pltpu.CompilerParams(
            dimension_semantics=("parallel", "parallel"), vmem_limit_bytes=V7X_VMEM_LIMIT),
        name="dn_combine",
    )(of, ob, z, norm_g.reshape(1, DN_DV))


def deltanet_stream(qkv, z, a_raw, b_raw, state, conv_w, conv_b, dt_bias, a_log, norm_g):
    bn, seq_len, _ = z.shape
    qkv = jax.nn.silu(dwconv(qkv, conv_w, conv_b))
    q, k, v = jnp.split(qkv, [DN_H * DN_DK, 2 * DN_H * DN_DK], axis=-1)
    q = l2norm(q.reshape(bn, seq_len, DN_H, DN_DK)) * (DN_DK ** -0.5)
    k = l2norm(k.reshape(bn, seq_len, DN_H, DN_DK))
    v = v.reshape(bn, seq_len, DN_H, DN_DV)
    beta = jax.nn.sigmoid(b_raw.reshape(bn, seq_len, 2, DN_H))
    g = -jnp.exp(a_log) * jax.nn.softplus(a_raw.reshape(bn, seq_len, 2, DN_H) + dt_bias)
    of, sf = gated_delta_rule(q, k, v, beta[:, :, 0], g[:, :, 0], state[0])
    ob, sb = gated_delta_rule(flip(q), flip(k), flip(v), flip(beta[:, :, 1]), flip(g[:, :, 1]), state[1])
    o = rmsnorm(of + flip(ob), norm_g)
    o = o.reshape(bn, seq_len, DN_H * DN_DV) * jax.nn.silu(z)
    return o, (sf, sb)


def deltanet_stream_pallas(qkv, z, a_raw, b_raw, state, conv_w, conv_b, dt_bias, a_log, norm_g):
    bn, seq_len, _ = z.shape
    qkvh = dn_prep(qkv, conv_w, conv_b)
    ng = DN_H // DN_HB
    nc = seq_len // DN_CHUNK
    outs = []
    for dr in range(2):
        a = a_raw.reshape(bn, seq_len, 2, ng, DN_HB)[:, :, dr]
        b = b_raw.reshape(bn, seq_len, 2, ng, DN_HB)[:, :, dr]
        a_col = jnp.transpose(a, (0, 2, 1, 3))
        b_col = jnp.transpose(b, (0, 2, 1, 3))
        a_row = jnp.transpose(a.reshape(bn, nc, DN_CHUNK, ng, DN_HB), (0, 3, 1, 4, 2))
        prm = jnp.stack([-jnp.exp(a_log[dr]), dt_bias[dr]]).astype(F32)
        outs.append(dn_delta(qkvh, a_col, b_col, a_row, prm, state[dr], dr == 1))
    (of, sf), (ob, sb) = outs
    return dn_combine(of, ob, z, norm_g), (sf, sb)


RG_TT = 256


def _rglru_body(xp_ref, x_ref, xn_ref, h0_ref, cw_ref, cb_ref, wa_ref, ba_ref, wx_ref, bx_ref, c_ref,
                h_ref, hf_ref, carry_s, *, rev, tt, nblk):
    i = pl.program_id(1)
    blk = (nblk - 1 - i) if rev else i

    @pl.when(i == 0)
    def _():
        carry_s[...] = h0_ref[0]

    x = x_ref[0]
    xp = xp_ref[0, tt - V7X_SUBLANES:, :] * (blk > 0).astype(F32)
    xn = xn_ref[0, :V7X_SUBLANES, :] * (blk < nblk - 1).astype(F32)
    xc = jnp.concatenate([xp, x, xn], axis=0)
    base = V7X_SUBLANES - CONV_PAD[0]
    xh = cb_ref[...] + sum(cw_ref[j:j + 1, :] * xc[base + j:base + j + tt, :] for j in range(CONV_W))

    xb = xh.astype(BF16)
    rs, is_ = [], []
    for h in range(RG_H):
        xs = xb[:, RG_BS * h:RG_BS * (h + 1)]
        rs.append(jnp.dot(xs, wa_ref[h], preferred_element_type=F32))
        is_.append(jnp.dot(xs, wx_ref[h], preferred_element_type=F32))
    r = jax.nn.sigmoid(jnp.concatenate(rs, axis=1) + ba_ref[...])
    ig = jax.nn.sigmoid(jnp.concatenate(is_, axis=1) + bx_ref[...])
    log_a = r * c_ref[...]
    a = jnp.exp(log_a)
    b = jnp.sqrt(1.0 - jnp.exp(2.0 * log_a)) * (ig * xh)

    row = lax.broadcasted_iota(jnp.int32, (tt, RG_W), 0)
    s = 1
    while s < tt:
        valid = (row < tt - s) if rev else (row >= s)
        shift = (tt - s) if rev else s
        a_s = jnp.where(valid, pltpu.roll(a, shift, 0), 1.0)
        b_s = jnp.where(valid, pltpu.roll(b, shift, 0), 0.0)
        b = a * b_s + b
        a = a * a_s
        s *= 2
    hcur = b + a * carry_s[...]
    h_ref[0] = hcur
    last = 0 if rev else tt - 1
    carry_s[...] = hcur[last:last + 1, :]
    hf_ref[0] = hcur[last:last + 1, :]


def rglru_scan(xr, h0, conv_w, conv_b, wa, ba, wx, bx, lam, rev):
    bn, seq_len, _ = xr.shape
    tt = min(RG_TT, seq_len)
    nblk = seq_len // tt
    blk = (lambda i: nblk - 1 - i) if rev else (lambda i: i)
    cur = lambda b, i: (b, blk(i), 0)
    prev = lambda b, i: (b, jnp.maximum(blk(i) - 1, 0), 0)
    nxt = lambda b, i: (b, jnp.minimum(blk(i) + 1, nblk - 1), 0)
    vec = pl.BlockSpec((1, RG_W), lambda b, i: (0, 0))
    wspec = pl.BlockSpec((RG_H, RG_BS, RG_BS), lambda b, i: (0, 0, 0))
    st = pl.BlockSpec((1, 1, RG_W), lambda b, i: (b, 0, 0))
    c = (-RG_C * jax.nn.softplus(-lam)).reshape(1, RG_W)
    h, hf = pl.pallas_call(
        functools.partial(_rglru_body, rev=rev, tt=tt, nblk=nblk),
        grid=(bn, nblk),
        in_specs=[pl.BlockSpec((1, tt, RG_W), prev), pl.BlockSpec((1, tt, RG_W), cur),
                  pl.BlockSpec((1, tt, RG_W), nxt), st,
                  pl.BlockSpec((CONV_W, RG_W), lambda b, i: (0, 0)), vec,
                  wspec, vec, wspec, vec, vec],
        out_specs=[pl.BlockSpec((1, tt, RG_W), cur), st],
        out_shape=[jax.ShapeDtypeStruct((bn, seq_len, RG_W), F32),
                   jax.ShapeDtypeStruct((bn, 1, RG_W), F32)],
        scratch_shapes=[pltpu.VMEM((1, RG_W), F32)],
        compiler_params=pltpu.CompilerParams(
            dimension_semantics=("parallel", "arbitrary"),
            vmem_limit_bytes=V7X_VMEM_LIMIT),
        name="rglru_bwd" if rev else "rglru_fwd",
    )(xr, xr, xr, h0.reshape(bn, 1, RG_W), conv_w, conv_b.reshape(1, RG_W),
      wa.astype(BF16), ba.reshape(1, RG_W), wx.astype(BF16), bx.reshape(1, RG_W), c)
    return h, hf.reshape(bn, RG_W)


def _rg_combine_body(hf_ref, hb_ref, g_ref, o_ref):
    o_ref[...] = (hf_ref[...] + hb_ref[...]) * jax.nn.gelu(g_ref[...])


def rg_combine(hf, hb, gate):
    m = hf.shape[0]
    tm = _pick(m, (1024, 512, 256, 128, 64, 32, 16, 8))
    row = pl.BlockSpec((tm, RG_W), lambda i: (i, 0))
    return pl.pallas_call(
        _rg_combine_body, grid=(m // tm,), in_specs=[row, row, row], out_specs=row,
        out_shape=jax.ShapeDtypeStruct((m, RG_W), F32),
        compiler_params=pltpu.CompilerParams(dimension_semantics=("parallel",)),
        name="rg_combine",
    )(hf, hb, gate)


def rglru_stream(xr, gate, state, conv_w, conv_b, wa, ba, wx, bx, lam):
    bn, seq_len, _ = xr.shape
    hf, sf = rglru_scan(xr, state[0], conv_w, conv_b, wa[0], ba[0], wx[0], bx[0], lam[0], False)
    hb, sb = rglru_scan(xr, state[1], conv_w, conv_b, wa[1], ba[1], wx[1], bx[1], lam[1], True)
    m = bn * seq_len
    y = rg_combine(hf.reshape(m, RG_W), hb.reshape(m, RG_W), gate.reshape(m, RG_W))
    return y.reshape(bn, seq_len, RG_W), (sf, sb)


def even_mixer(hx, hc, w_in, w_out, s5_params, ssd_params, need_ctx):
    bn = hx.shape[0]

    def stream(h, st_s5, st_ssd):
        p = mm3(h, w_in)
        u, z, xbc, dt_raw = jnp.split(p, EV_CUTS, axis=-1)
        ya, st_a = s5_stream(u, st_s5, *s5_params)
        yb, st_b = ssd_stream(z, xbc, dt_raw, st_ssd, *ssd_params)
        return jnp.concatenate([ya, yb], axis=-1), st_a, st_b
    zs5 = jnp.zeros((bn, S5_LANES), F32)
    zssd = jnp.zeros((bn, SSD_H, SSD_P, SSD_N), F32)
    yc, st_a, st_b = stream(hc, (zs5, zs5, zs5, zs5), (zssd, zssd))
    yx, _, _ = stream(hx, st_a, st_b)
    return mm3(yx, w_out), (mm3(yc, w_out) if need_ctx else None)


def odd_mixer(hx, hc, w_in, w_out, dn_params, rg_params, need_ctx):
    bn = hx.shape[0]

    def stream(h, st_dn, st_rg):
        p = mm3(h, w_in)
        qkv, z, a_raw, b_raw, xr, gate = jnp.split(p, OD_CUTS, axis=-1)
        yd, st_d = deltanet_stream(qkv, z, a_raw, b_raw, st_dn, *dn_params)
        yr, st_r = rglru_stream(xr, gate, st_rg, *rg_params)
        return jnp.concatenate([yd, yr], axis=-1), st_d, st_r
    zdn = jnp.zeros((bn, DN_H, DN_DK, DN_DV), F32)
    zrg = jnp.zeros((bn, RG_W), F32)
    yc, st_d, st_r = stream(hc, (zdn, zdn), (zrg, zrg))
    yx, _, _ = stream(hx, st_d, st_r)
    return mm3(yx, w_out), (mm3(yc, w_out) if need_ctx else None)


PEER_TM = 512
PEER_EB = 512
PEER_RT = 16
PEER_SEL = PEER_H * PEER_TOPK


def _peer_body(flags_ref, ht_ref, u_ref, vt_ref, s1_ref, s2_ref, e1_ref, e2_ref, tau_ref,
               pidx_ref, ridx_ref, gate_ref, o_ref, st_s, g_s, *, tm, eb):
    i = pl.program_id(0)
    j = pl.program_id(1)
    npk = eb // PEER_NK

    @pl.when(j == 0)
    def _():
        o_ref[...] = jnp.zeros_like(o_ref)

    @pl.when(flags_ref[i] == 0)
    def _():
        st_s[...] = jnp.dot(u_ref[...], ht_ref[...], preferred_element_type=F32)
        for lg in range(tm // V7X_LANES):
            ls = slice(V7X_LANES * lg, V7X_LANES * (lg + 1))
            t_rows = [tau_ref[h:h + 1, ls] for h in range(PEER_H)]
            a_tiles = [s1_ref[j * npk + pk, :, ls] for pk in range(npk)]
            z_tiles = [e1_ref[j * npk + pk, :, ls] for pk in range(npk)]
            for rt in range(PEER_NK // PEER_RT):
                rs = slice(rt * PEER_RT, (rt + 1) * PEER_RT)
                s2 = [s2_ref[h, rs, ls] for h in range(PEER_H)]
                e2 = [e2_ref[h, rs, ls] for h in range(PEER_H)]
                for pk in range(npk):
                    acc = jnp.zeros((PEER_RT, V7X_LANES), F32)
                    for h in range(PEER_H):
                        hit = a_tiles[pk][h:h + 1, :] + s2[h] >= t_rows[h]
                        acc = acc + jnp.where(hit, z_tiles[pk][h:h + 1, :] * e2[h], 0.0)
                    rows = slice(pk * PEER_NK + rt * PEER_RT, pk * PEER_NK + (rt + 1) * PEER_RT)
                    g_s[rows, ls] = (acc * jax.nn.gelu(st_s[rows, ls])).astype(BF16)
        o_ref[...] += jnp.dot(vt_ref[...], g_s[...], preferred_element_type=F32)

    @pl.when(flags_ref[i] != 0)
    def _():
        st_s[...] = jnp.dot(u_ref[...], ht_ref[...], preferred_element_type=F32)
        riota = lax.broadcasted_iota(jnp.int32, (PEER_NK, V7X_LANES), 0)
        for lg in range(tm // V7X_LANES):
            ls = slice(V7X_LANES * lg, V7X_LANES * (lg + 1))
            for pk in range(npk):
                p = j * npk + pk

                def kbody(k8, w, ls=ls, p=p):
                    k0 = pl.multiple_of(k8 * V7X_SUBLANES, V7X_SUBLANES)
                    ptile = pidx_ref[pl.ds(k0, V7X_SUBLANES), ls]
                    rtile = ridx_ref[pl.ds(k0, V7X_SUBLANES), ls]
                    gtile = gate_ref[pl.ds(k0, V7X_SUBLANES), ls]
                    ctile = jnp.where(ptile == p, gtile, 0.0)
                    for k in range(V7X_SUBLANES):
                        w = w + jnp.where(rtile[k:k + 1, :] == riota, ctile[k:k + 1, :], 0.0)
                    return w

                w = lax.fori_loop(0, PEER_SEL // V7X_SUBLANES, kbody,
                                  jnp.zeros((PEER_NK, V7X_LANES), F32))
                rows = slice(pk * PEER_NK, (pk + 1) * PEER_NK)
                g_s[rows, ls] = (w * jax.nn.gelu(st_s[rows, ls])).astype(BF16)
        o_ref[...] += jnp.dot(vt_ref[...], g_s[...], preferred_element_type=F32)


def peer_experts(flags, ht, u_bf, vt_bf, s1t, s2t, e1t, e2t, taut, pidx, ridx, gate, tm):
    dm, t = ht.shape
    eb = PEER_EB
    head3 = pl.BlockSpec((PEER_H, PEER_NK, tm), lambda i, j, f: (0, 0, i))
    key3 = pl.BlockSpec((PEER_NK, PEER_H, tm), lambda i, j, f: (0, 0, i))
    sel2 = pl.BlockSpec((PEER_SEL, tm), lambda i, j, f: (0, i))
    grid_spec = pltpu.PrefetchScalarGridSpec(
        num_scalar_prefetch=1,
        grid=(t // tm, PEER_E // eb),
        in_specs=[pl.BlockSpec((dm, tm), lambda i, j, f: (0, i)),
                  pl.BlockSpec((eb, dm), lambda i, j, f: (j, 0)),
                  pl.BlockSpec((dm, eb), lambda i, j, f: (0, j)),
                  key3, head3, key3, head3,
                  pl.BlockSpec((PEER_H, tm), lambda i, j, f: (0, i)),
                  sel2, sel2, sel2],
        out_specs=pl.BlockSpec((dm, tm), lambda i, j, f: (0, i)),
        scratch_shapes=[pltpu.VMEM((eb, tm), F32), pltpu.VMEM((eb, tm), BF16)])
    return pl.pallas_call(
        functools.partial(_peer_body, tm=tm, eb=eb),
        grid_spec=grid_spec,
        out_shape=jax.ShapeDtypeStruct((dm, t), F32),
        compiler_params=pltpu.CompilerParams(
            dimension_semantics=("parallel", "arbitrary"),
            vmem_limit_bytes=V7X_VMEM_LIMIT),
        name="peer_experts",
    )(flags, ht, u_bf, vt_bf, s1t, s2t, e1t, e2t, taut, pidx, ridx, gate)


def _topk_rows(x, k):
    n = x.shape[0]
    iota = lax.broadcasted_iota(jnp.int32, x.shape, 0)
    vals, idxs = [], []
    for r in range(k):
        m = jnp.max(x, axis=0, keepdims=True)
        idx = jnp.min(jnp.where(x == m, iota, n), axis=0, keepdims=True)
        vals.append(m)
        idxs.append(idx)
        if r < k - 1:
            x = jnp.where(iota == idx, -jnp.inf, x)
    return vals, idxs


def _peer_select_body(q_ref, k1_ref, k2_ref, s1_ref, s2_ref, e1_ref, e2_ref, tau_ref, tie_ref,
                      pidx_ref, ridx_ref, gate_ref):
    qb = q_ref[...].astype(BF16)
    dn = (((1,), (1,)), ((), ()))
    s1 = lax.dot_general(k1_ref[0], qb[:, :PEER_HALF], dn, preferred_element_type=F32)
    s2 = lax.dot_general(k2_ref[0], qb[:, PEER_HALF:], dn, preferred_element_type=F32)
    t1, i1 = _topk_rows(s1, PEER_TOPK + 1)
    t2, i2 = _topk_rows(s2, PEER_TOPK + 1)
    t2blk = jnp.concatenate(t2[:PEER_TOPK], axis=0)
    cand = jnp.concatenate([t1[a] + t2blk for a in range(PEER_TOPK)], axis=0)
    top, ci = _topk_rows(cand, PEER_TOPK)
    topb = jnp.concatenate(top, axis=0)
    cib = jnp.concatenate(ci, axis=0)
    ca = cib // PEER_TOPK
    cb = cib % PEER_TOPK
    pid = jnp.zeros_like(cib)
    rid = jnp.zeros_like(cib)
    for a in range(PEER_TOPK):
        pid = pid + jnp.where(ca == a, i1[a], 0)
        rid = rid + jnp.where(cb == a, i2[a], 0)
    ex = jnp.exp(topb - top[0])
    zsum = jnp.sum(ex, axis=0, keepdims=True)
    tau = top[PEER_TOPK - 1]
    cnt = jnp.sum((cand >= tau).astype(jnp.int32), axis=0, keepdims=True)
    tie = ((cnt != PEER_TOPK) | (t1[PEER_TOPK] + t2[0] >= tau) | (t1[0] + t2[PEER_TOPK] >= tau))
    s1_ref[0] = s1
    s2_ref[0] = s2
    e1_ref[0] = jnp.exp(s1 - t1[0]) / zsum
    e2_ref[0] = jnp.exp(s2 - t2[0])
    tau_ref[0] = tau
    tie_ref[0] = tie.astype(jnp.int32)
    pidx_ref[...] = pid
    ridx_ref[...] = rid
    gate_ref[...] = ex / zsum


def peer_select(q, k1_bf, k2_bf):
    t = q.shape[0]
    tl = V7X_LANES
    head3 = pl.BlockSpec((1, PEER_NK, tl), lambda i, h: (h, 0, i))
    row3 = pl.BlockSpec((1, 1, tl), lambda i, h: (h, 0, i))
    sel2 = pl.BlockSpec((PEER_TOPK, tl), lambda i, h: (h, i))
    key_spec = pl.BlockSpec((1, PEER_NK, PEER_HALF), lambda i, h: (h, 0, 0))
    f3 = jax.ShapeDtypeStruct((PEER_H, PEER_NK, t), F32)
    return pl.pallas_call(
        _peer_select_body,
        grid=(t // tl, PEER_H),
        in_specs=[pl.BlockSpec((tl, PEER_DK), lambda i, h: (i, h)), key_spec, key_spec],
        out_specs=[head3, head3, head3, head3, row3, row3, sel2, sel2, sel2],
        out_shape=[f3, f3, f3, f3,
                   jax.ShapeDtypeStruct((PEER_H, 1, t), F32),
                   jax.ShapeDtypeStruct((PEER_H, 1, t), jnp.int32),
                   jax.ShapeDtypeStruct((PEER_SEL, t), jnp.int32),
                   jax.ShapeDtypeStruct((PEER_SEL, t), jnp.int32),
                   jax.ShapeDtypeStruct((PEER_SEL, t), F32)],
        compiler_params=pltpu.CompilerParams(
            dimension_semantics=("parallel", "arbitrary"),
            vmem_limit_bytes=V7X_VMEM_LIMIT),
        name="peer_select",
    )(q, k1_bf, k2_bf)


def peer(h, w_q, k1, k2, u_bf, vt_bf, tm=PEER_TM):
    bn, seq_len, dm = h.shape
    t = bn * seq_len
    q = mm(h.reshape(t, dm), w_q)
    s1t, s2t, e1t, e2t, taut, tiet, pidx, ridx, gate = peer_select(q, k1.astype(BF16), k2.astype(BF16))
    flags = jnp.any(tiet.reshape(PEER_H, t // tm, tm) != 0, axis=(0, 2)).astype(jnp.int32)
    outt = peer_experts(
        flags, h.reshape(t, dm).T.astype(BF16), u_bf, vt_bf,
        jnp.transpose(s1t, (1, 0, 2)), s2t, jnp.transpose(e1t, (1, 0, 2)), e2t, taut.reshape(PEER_H, t),
        pidx, ridx, gate, tm)
    return outt.T.reshape(bn, seq_len, dm)


def kernel(x, c, ctx, c_ctx, ada_w, ada_b, norm1_g, norm2_g, final_g, ev_w_in, ev_w_out, s5_lam_re, s5_lam_im, s5_log_dt, s5_b_re, s5_b_im, s5_c_re, s5_c_im, s5_d, s5_glu_w, s5_glu_b, ssd_conv_w, ssd_conv_b, ssd_dt_bias, ssd_a_log, ssd_d, ssd_norm_g, od_w_in, od_w_out, dn_conv_w, dn_conv_b, dn_dt_bias, dn_a_log, dn_norm_g, rg_conv_w, rg_conv_b, rg_wa, rg_ba, rg_wx, rg_bx, rg_lam, peer_wq, peer_k1, peer_k2, peer_u, peer_v):
    bn, seq_len, _ = x.shape
    rows = seq_len // GRID_W
    sc = jax.nn.silu(c)
    scc = jax.nn.silu(c_ctx)
    for layer in range(DEPTH):
        j = layer // 2
        need_ctx = layer < DEPTH - 1
        mx = (sc @ ada_w[layer] + ada_b[layer]).reshape(bn, 6, 1, D_MODEL)
        mc = (scc @ ada_w[layer] + ada_b[layer]).reshape(6, D_MODEL)
        hx = modulate(rmsnorm(x, norm1_g[layer]), mx[:, 0], mx[:, 1])
        hc = modulate(rmsnorm(ctx, norm1_g[layer]), mc[0], mc[1])
        if layer % 2 == 0:
            s5_params = (s5_lam_re[j], s5_lam_im[j], s5_log_dt[j], s5_b_re[j], s5_b_im[j],
                         s5_c_re[j], s5_c_im[j], s5_d[j], s5_glu_w[j], s5_glu_b[j])
            ssd_params = (ssd_conv_w[j], ssd_conv_b[j], ssd_dt_bias[j], ssd_a_log[j],
                          ssd_d[j], ssd_norm_g[j])
            ox, oc = even_mixer(hx, hc, ev_w_in[j], ev_w_out[j], s5_params, ssd_params, need_ctx)
        else:
            dn_params = (dn_conv_w[j], dn_conv_b[j], dn_dt_bias[j], dn_a_log[j], dn_norm_g[j])
            rg_params = (rg_conv_w[j], rg_conv_b[j], rg_wa[j], rg_ba[j], rg_wx[j], rg_bx[j], rg_lam[j])
            ox, oc = odd_mixer(to_col_major(hx, rows), hc, od_w_in[j], od_w_out[j],
                               dn_params, rg_params, need_ctx)
            ox = to_row_major(ox, rows)
        x = x + mx[:, 2] * ox
        hx = modulate(rmsnorm(x, norm2_g[layer]), mx[:, 3], mx[:, 4])
        u_bf = peer_u[layer].astype(BF16)
        vt_bf = peer_v[layer].T.astype(BF16)
        x = x + mx[:, 5] * peer(hx, peer_wq[layer], peer_k1[layer], peer_k2[layer], u_bf, vt_bf)
        if need_ctx:
            ctx = ctx + mc[2] * oc
            hc = modulate(rmsnorm(ctx, norm2_g[layer]), mc[3], mc[4])
            ctx = ctx + mc[5] * peer(hc, peer_wq[layer], peer_k1[layer], peer_k2[layer], u_bf, vt_bf)
    return rmsnorm(x, final_g)
```

```python
import functools
import math

import jax
import jax.numpy as jnp
from jax import lax
from jax.experimental import pallas as pl
from jax.experimental.pallas import tpu as pltpu

D_MODEL = 2048
DEPTH = 2
GRID_W = 64
EPS = 1e-6
CONV_W = 4
CONV_PAD = (CONV_W // 2, CONV_W - 1 - CONV_W // 2)

S5_W = D_MODEL // 2
S5_CH = 16
S5_G = S5_W // S5_CH
S5_N = 64
SSD_P = 64
SSD_H = D_MODEL // SSD_P
SSD_DI = SSD_H * SSD_P
SSD_N = 128
SSD_G = 4
SSD_CHUNK = 128
SSD_CONV_CH = SSD_DI + 2 * SSD_G * SSD_N
EV_IN = S5_W + SSD_DI + SSD_CONV_CH + 2 * SSD_H
EV_MIX = S5_W + SSD_DI
DN_DK = 128
DN_DV = 128
DN_H = D_MODEL // DN_DK
DN_CHUNK = 64
DN_QKV = DN_H * (2 * DN_DK + DN_DV)
RG_W = D_MODEL // 2
RG_H = 8
RG_BS = RG_W // RG_H
RG_C = 8.0
OD_IN = DN_QKV + DN_H * DN_DV + 4 * DN_H + 2 * RG_W
OD_MIX = DN_H * DN_DV + RG_W
OD_CUTS = [DN_QKV,
           DN_QKV + DN_H * DN_DV,
           DN_QKV + DN_H * DN_DV + 2 * DN_H,
           DN_QKV + DN_H * DN_DV + 4 * DN_H,
           DN_QKV + DN_H * DN_DV + 4 * DN_H + RG_W]
EV_CUTS = [S5_W, S5_W + SSD_DI, S5_W + SSD_DI + SSD_CONV_CH]
PEER_H = 8
PEER_DK = 256
PEER_HALF = PEER_DK // 2
PEER_NK = 128
PEER_E = PEER_NK * PEER_NK
PEER_TOPK = 16

F32 = jnp.float32
BF16 = jnp.bfloat16

V7X_LANES = 128
V7X_SUBLANES = 8
V7X_VMEM_LIMIT = 56 * 1024 * 1024


def _mm_body(x_ref, w_ref, o_ref):
    o_ref[...] = jnp.dot(x_ref[...], w_ref[...], preferred_element_type=F32).astype(o_ref.dtype)


def _pick(n, cands):
    for c in cands:
        if n % c == 0:
            return c
    raise ValueError(f"no tile for {n}")


def mm(x, w, out_dtype=F32):
    m, k = x.shape
    n = w.shape[1]
    n_pad = -n % V7X_LANES
    if n_pad:
        w = jnp.pad(w, ((0, 0), (0, n_pad)))
    np_ = n + n_pad
    tm = _pick(m, (1024, 512, 256, 128, 64, 32, 16, 8))
    tn = _pick(np_, (512, 256, 128))
    out = pl.pallas_call(
        _mm_body,
        grid=(m // tm, np_ // tn),
        in_specs=[pl.BlockSpec((tm, k), lambda i, j: (i, 0)),
                  pl.BlockSpec((k, tn), lambda i, j: (0, j))],
        out_specs=pl.BlockSpec((tm, tn), lambda i, j: (i, j)),
        out_shape=jax.ShapeDtypeStruct((m, np_), out_dtype),
        compiler_params=pltpu.CompilerParams(
            dimension_semantics=("parallel", "arbitrary"),
            vmem_limit_bytes=V7X_VMEM_LIMIT),
        name="mm",
    )(x.astype(BF16), w.astype(BF16))
    return out[:, :n] if n_pad else out


def mm3(h, w):
    b, l, d = h.shape
    return mm(h.reshape(b * l, d), w).reshape(b, l, w.shape[1])


def rmsnorm(x, g):
    xf = x.astype(F32)
    y = xf * lax.rsqrt(jnp.mean(xf * xf, axis=-1, keepdims=True) + EPS)
    return (y * g.astype(F32)).astype(x.dtype)


def l2norm(x):
    return x * lax.rsqrt(jnp.sum(x * x, axis=-1, keepdims=True) + EPS)


def modulate(h, shift, scale):
    return h * (1 + scale) + shift


def dwconv(x, w, b):
    y = lax.conv_general_dilated(x, w[:, None, :], (1,), [CONV_PAD],
                                 dimension_numbers=('NWC', 'WIO', 'NWC'),
                                 feature_group_count=x.shape[-1])
    return y + b


def flip(t):
    return jnp.flip(t, axis=1)


def to_col_major(x, rows):
    b, l, d = x.shape
    return x.reshape(b, rows, GRID_W, d).transpose(0, 2, 1, 3).reshape(b, l, d)


def to_row_major(x, rows):
    b, l, d = x.shape
    return x.reshape(b, GRID_W, rows, d).transpose(0, 2, 1, 3).reshape(b, l, d)


S5_LANES = S5_G * S5_N
S5_TILE = 8
S5_SHIFTS = (1, 2, 4)
S5_BG = 8
S5_CG = 16
S5_CHUNK = 1024


def _s5_scan_body(u_ref, h0r_ref, h0i_ref, bdr_ref, bdi_ref, cdr_ref, cdi_ref, coef_ref,
                  y_ref, hfr_ref, hfi_ref, xr_s, xi_s, cr_s, ci_s, *, rev, tt):
    @pl.when(pl.program_id(1) == 0)
    def _():
        cr_s[...] = h0r_ref[0]
        ci_s[...] = h0i_ref[0]

    ub = u_ref[0].astype(BF16)
    kin = S5_BG * S5_CH
    kout = S5_BG * S5_N
    for j in range(S5_G // S5_BG):
        uj = ub[:, kin * j:kin * (j + 1)]
        xr_s[:, kout * j:kout * (j + 1)] = jnp.dot(uj, bdr_ref[j], preferred_element_type=F32)
        xi_s[:, kout * j:kout * (j + 1)] = jnp.dot(uj, bdi_ref[j], preferred_element_type=F32)

    nt = tt // S5_TILE
    carry_row = 0 if rev else S5_TILE - 1

    def tile(kk, carry):
        k = (nt - 1 - kk) if rev else kk
        r0 = pl.multiple_of(k * S5_TILE, S5_TILE)
        for c in range(S5_LANES // S5_CHUNK):
            sl = slice(S5_CHUNK * c, S5_CHUNK * (c + 1))
            xr = xr_s[pl.ds(r0, S5_TILE), sl]
            xi = xi_s[pl.ds(r0, S5_TILE), sl]
            for idx, sh in enumerate(S5_SHIFTS):
                pr = coef_ref[2 * idx, :, sl]
                pim = coef_ref[2 * idx + 1, :, sl]
                rs = (S5_TILE - sh) if rev else sh
                sr = pltpu.roll(xr, rs, 0)
                si = pltpu.roll(xi, rs, 0)
                xr, xi = xr + (pr * sr - pim * si), xi + (pr * si + pim * sr)
            pr = coef_ref[2 * len(S5_SHIFTS), :, sl]
            pim = coef_ref[2 * len(S5_SHIFTS) + 1, :, sl]
            hr = cr_s[:, sl]
            hi = ci_s[:, sl]
            xr, xi = xr + (pr * hr - pim * hi), xi + (pr * hi + pim * hr)
            xr_s[pl.ds(r0, S5_TILE), sl] = xr
            xi_s[pl.ds(r0, S5_TILE), sl] = xi
            cr_s[:, sl] = xr[carry_row:carry_row + 1, :]
            ci_s[:, sl] = xi[carry_row:carry_row + 1, :]
        return carry

    lax.fori_loop(0, nt, tile, 0)

    kc = S5_CG * S5_N
    nc = S5_CG * S5_CH
    for j in range(S5_G // S5_CG):
        sr = xr_s[:, kc * j:kc * (j + 1)].astype(BF16)
        si = xi_s[:, kc * j:kc * (j + 1)].astype(BF16)
        y_ref[0, :, nc * j:nc * (j + 1)] = (
            jnp.dot(sr, cdr_ref[j], preferred_element_type=F32)
            - jnp.dot(si, cdi_ref[j], preferred_element_type=F32))
    hfr_ref[0] = cr_s[...]
    hfi_ref[0] = ci_s[...]


def _s5_direction_params(lam_re, lam_im, log_dt, b_re, b_im, c_re, c_im, rev):
    dt = jnp.exp(log_dt)[:, None]
    mag = jnp.exp(lam_re * dt)
    ar, ai = mag * jnp.cos(lam_im * dt), mag * jnp.sin(lam_im * dt)
    den = lam_re * lam_re + lam_im * lam_im
    cr = ((ar - 1) * lam_re + ai * lam_im) / den
    ci = (ai * lam_re - (ar - 1) * lam_im) / den
    bb_re = cr[..., None] * b_re - ci[..., None] * b_im
    bb_im = cr[..., None] * b_im + ci[..., None] * b_re

    def bdiag_in(bb):
        t = bb.reshape(S5_G // S5_BG, S5_BG, S5_N, S5_CH)
        eye = jnp.eye(S5_BG, dtype=F32)
        return jnp.einsum('jgnc,gh->jgchn', t, eye).reshape(
            S5_G // S5_BG, S5_BG * S5_CH, S5_BG * S5_N).astype(BF16)

    def bdiag_out(cc):
        t = cc.reshape(S5_G // S5_CG, S5_CG, S5_CH, S5_N)
        eye = jnp.eye(S5_CG, dtype=F32)
        return jnp.einsum('jgcn,gh->jgnhc', t, eye).reshape(
            S5_G // S5_CG, S5_CG * S5_N, S5_CG * S5_CH).astype(BF16)

    a_r, a_i = ar.reshape(-1), ai.reshape(-1)
    pw = [(a_r, a_i)]
    for _ in range(S5_TILE - 1):
        pr, pim = pw[-1]
        pw.append((pr * a_r - pim * a_i, pr * a_i + pim * a_r))
    rows = jnp.arange(S5_TILE)[:, None]
    coef = []
    for sh in S5_SHIFTS:
        valid = (rows <= S5_TILE - 1 - sh) if rev else (rows >= sh)
        coef.append(jnp.where(valid, pw[sh - 1][0][None, :], 0.0))
        coef.append(jnp.where(valid, pw[sh - 1][1][None, :], 0.0))
    order = list(range(S5_TILE - 1, -1, -1)) if rev else list(range(S5_TILE))
    coef.append(jnp.stack([pw[k][0] for k in order]))
    coef.append(jnp.stack([pw[k][1] for k in order]))
    return (bdiag_in(bb_re), bdiag_in(bb_im), bdiag_out(c_re), bdiag_out(c_im),
            jnp.stack(coef).astype(F32))


def s5_scan(u, h0r, h0i, dparams, rev):
    bn, seq_len, _ = u.shape
    bdr, bdi, cdr, cdi, coef = dparams
    tt = min(256, seq_len)
    nblk = seq_len // tt
    tmap = (lambda b, i: (b, nblk - 1 - i, 0)) if rev else (lambda b, i: (b, i, 0))
    const3 = lambda b, i: (0, 0, 0)
    y, hfr, hfi = pl.pallas_call(
        functools.partial(_s5_scan_body, rev=rev, tt=tt),
        grid=(bn, nblk),
        in_specs=[pl.BlockSpec((1, tt, S5_W), tmap),
                  pl.BlockSpec((1, 1, S5_LANES), lambda b, i: (b, 0, 0)),
                  pl.BlockSpec((1, 1, S5_LANES), lambda b, i: (b, 0, 0)),
                  pl.BlockSpec(bdr.shape, const3),
                  pl.BlockSpec(bdi.shape, const3),
                  pl.BlockSpec(cdr.shape, const3),
                  pl.BlockSpec(cdi.shape, const3),
                  pl.BlockSpec(coef.shape, const3)],
        out_specs=[pl.BlockSpec((1, tt, S5_W), tmap),
                   pl.BlockSpec((1, 1, S5_LANES), lambda b, i: (b, 0, 0)),
                   pl.BlockSpec((1, 1, S5_LANES), lambda b, i: (b, 0, 0))],
        out_shape=[jax.ShapeDtypeStruct((bn, seq_len, S5_W), F32),
                   jax.ShapeDtypeStruct((bn, 1, S5_LANES), F32),
                   jax.ShapeDtypeStruct((bn, 1, S5_LANES), F32)],
        scratch_shapes=[pltpu.VMEM((tt, S5_LANES), F32), pltpu.VMEM((tt, S5_LANES), F32),
                        pltpu.VMEM((1, S5_LANES), F32), pltpu.VMEM((1, S5_LANES), F32)],
        compiler_params=pltpu.CompilerParams(
            dimension_semantics=("parallel", "arbitrary"),
            vmem_limit_bytes=V7X_VMEM_LIMIT),
        name="s5_scan_bwd" if rev else "s5_scan_fwd",
    )(u, h0r.reshape(bn, 1, S5_LANES), h0i.reshape(bn, 1, S5_LANES), bdr, bdi, cdr, cdi, coef)
    return y, hfr.reshape(bn, S5_LANES), hfi.reshape(bn, S5_LANES)


def _s5_glu_body(yf_ref, yb_ref, u_ref, d_ref, w_ref, b_ref, o_ref):
    y = yf_ref[...] + yb_ref[...] + d_ref[...] * u_ref[...]
    g = jax.nn.gelu(y)
    z = jnp.dot(g.astype(BF16), w_ref[...], preferred_element_type=F32) + b_ref[...]
    o_ref[...] = (g * jax.nn.sigmoid(z)).astype(o_ref.dtype)


def s5_glu(yf, yb, u, d, glu_w, glu_b):
    m = yf.shape[0]
    tm = _pick(m, (512, 256, 128, 64, 32, 16, 8))
    row = pl.BlockSpec((tm, S5_W), lambda i: (i, 0))
    vec = pl.BlockSpec((1, S5_W), lambda i: (0, 0))
    return pl.pallas_call(
        _s5_glu_body,
        grid=(m // tm,),
        in_specs=[row, row, row, vec, pl.BlockSpec((S5_W, S5_W), lambda i: (0, 0)), vec],
        out_specs=row,
        out_shape=jax.ShapeDtypeStruct((m, S5_W), F32),
        compiler_params=pltpu.CompilerParams(
            dimension_semantics=("parallel",), vmem_limit_bytes=V7X_VMEM_LIMIT),
        name="s5_glu",
    )(yf, yb, u, d.reshape(1, S5_W), glu_w.astype(BF16), glu_b.reshape(1, S5_W))


def s5_stream(u, state, lam_re, lam_im, log_dt, b_re, b_im, c_re, c_im, d, glu_w, glu_b):
    bn, seq_len, _ = u.shape
    pf = _s5_direction_params(lam_re[0], lam_im[0], log_dt[0], b_re[0], b_im[0], c_re[0], c_im[0], False)
    pb = _s5_direction_params(lam_re[1], lam_im[1], log_dt[1], b_re[1], b_im[1], c_re[1], c_im[1], True)
    yf, fr, fi = s5_scan(u, state[0], state[1], pf, False)
    yb, br, bi = s5_scan(u, state[2], state[3], pb, True)
    m = bn * seq_len
    out = s5_glu(yf.reshape(m, S5_W), yb.reshape(m, S5_W), u.reshape(m, S5_W), d, glu_w, glu_b)
    return out.reshape(bn, seq_len, S5_W), (fr, fi, br, bi)


SSD_TT = 256
SSD_CW = 1024
SSD_R = SSD_H // SSD_G


def _conv_silu_body(xp_ref, x_ref, xn_ref, cw_ref, cb_ref, o_ref, *, tt, nblk):
    blk = pl.program_id(1)
    x = x_ref[0]
    xp = xp_ref[0, tt - V7X_SUBLANES:, :] * (blk > 0).astype(F32)
    xn = xn_ref[0, :V7X_SUBLANES, :] * (blk < nblk - 1).astype(F32)
    xc = jnp.concatenate([xp, x, xn], axis=0)
    base = V7X_SUBLANES - CONV_PAD[0]
    y = cb_ref[...] + sum(cw_ref[j:j + 1, :] * xc[base + j:base + j + tt, :] for j in range(CONV_W))
    o_ref[0] = jax.nn.silu(y)


def conv_silu(x, conv_w, conv_b):
    bn, seq_len, ch = x.shape
    tt = min(SSD_TT, seq_len)
    nblk = seq_len // tt
    cur = lambda b, i, p: (b, i, p)
    prev = lambda b, i, p: (b, jnp.maximum(i - 1, 0), p)
    nxt = lambda b, i, p: (b, jnp.minimum(i + 1, nblk - 1), p)
    return pl.pallas_call(
        functools.partial(_conv_silu_body, tt=tt, nblk=nblk),
        grid=(bn, nblk, ch // SSD_CW),
        in_specs=[pl.BlockSpec((1, tt, SSD_CW), prev), pl.BlockSpec((1, tt, SSD_CW), cur),
                  pl.BlockSpec((1, tt, SSD_CW), nxt),
                  pl.BlockSpec((CONV_W, SSD_CW), lambda b, i, p: (0, p)),
                  pl.BlockSpec((1, SSD_CW), lambda b, i, p: (0, p))],
        out_specs=pl.BlockSpec((1, tt, SSD_CW), cur),
        out_shape=jax.ShapeDtypeStruct((bn, seq_len, ch), F32),
        compiler_params=pltpu.CompilerParams(
            dimension_semantics=("parallel", "parallel", "arbitrary"),
            vmem_limit_bytes=V7X_VMEM_LIMIT),
        name="conv_silu",
    )(x, x, x, conv_w, conv_b.reshape(1, ch))


def _ssd_scan_body(prm_ref, x_ref, b_ref, c_ref, dc_ref, dr_ref, s0_ref, y_ref, sf_ref, s_s, *, rev):
    c = pl.program_id(1)

    @pl.when(c == 0)
    def _():
        s_s[...] = s0_ref[0]

    qn = SSD_CHUNK
    ri = lax.broadcasted_iota(jnp.int32, (qn, qn), 0)
    ci = lax.broadcasted_iota(jnp.int32, (qn, qn), 1)
    incl = (ri <= ci) if rev else (ri >= ci)
    incl_t = (ri >= ci) if rev else (ri <= ci)
    end = 0 if rev else qn - 1
    nt = (((1,), (1,)), ((), ()))
    tn = (((0,), (0,)), ((), ()))
    for g in range(SSD_G):
        bm = b_ref[0, :, SSD_N * g:SSD_N * (g + 1)].astype(BF16)
        cm = c_ref[0, :, SSD_N * g:SSD_N * (g + 1)].astype(BF16)
        cb = lax.dot_general(cm, bm, nt, preferred_element_type=F32)
        heads = range(SSD_R * g, SSD_R * (g + 1))
        dts, cums, segs = [], [], []
        for h in heads:
            a_neg = prm_ref[0, h]
            dtb = prm_ref[1, h]
            dt_col = jax.nn.softplus(dc_ref[0, :, h:h + 1] + dtb)
            dt_row = jax.nn.softplus(dr_ref[0, 0, h:h + 1, :] + dtb)
            cum_col = jnp.sum(jnp.where(incl, dt_row * a_neg, 0.0), axis=1, keepdims=True)
            cum_row = jnp.sum(jnp.where(incl_t, dt_col * a_neg, 0.0), axis=0, keepdims=True)
            dts.append(dt_col)
            cums.append(cum_col)
            segs.append(jnp.where(incl, jnp.exp(cum_col - cum_row), 0.0) * cb)
        xdts = [x_ref[0, :, SSD_P * h:SSD_P * (h + 1)] * dts[i] for i, h in enumerate(heads)]
        ydiags = [jnp.dot(segs[i].astype(BF16), xdts[i].astype(BF16), preferred_element_type=F32)
                  for i in range(SSD_R)]
        ss = [s_s[h] for h in heads]
        yoffs = [lax.dot_general(cm, ss[i].astype(BF16), nt, preferred_element_type=F32) * jnp.exp(cums[i])
                 for i in range(SSD_R)]
        ys = [ydiags[i] + yoffs[i] for i in range(SSD_R)]
        for i in range(0, SSD_R, 2):
            lo = SSD_P * (SSD_R * g + i)
            y_ref[0, :, lo:lo + 2 * SSD_P] = jnp.concatenate([ys[i], ys[i + 1]], axis=1)
        for i, h in enumerate(heads):
            cend = cums[i][end:end + 1, :]
            xw = (xdts[i] * jnp.exp(cend - cums[i])).astype(BF16)
            s_new = ss[i] * jnp.exp(cend) + lax.dot_general(xw, bm, tn, preferred_element_type=F32)
            s_s[h] = s_new
            sf_ref[0, h] = s_new


def ssd_scan_pallas(xs, bm, cm, dt_col, dt_row, prm, s0, rev):
    bn, seq_len, _ = xs.shape
    qn = SSD_CHUNK
    nc = seq_len // qn
    cidx = (lambda c: nc - 1 - c) if rev else (lambda c: c)
    tok = lambda w: pl.BlockSpec((1, qn, w), lambda b, c: (b, cidx(c), 0))
    st = pl.BlockSpec((1, SSD_H, SSD_P, SSD_N), lambda b, c: (b, 0, 0, 0))
    return pl.pallas_call(
        functools.partial(_ssd_scan_body, rev=rev),
        grid=(bn, nc),
        in_specs=[pl.BlockSpec(memory_space=pltpu.SMEM), tok(SSD_DI), tok(SSD_G * SSD_N), tok(SSD_G * SSD_N),
                  tok(SSD_H), pl.BlockSpec((1, 1, SSD_H, qn), lambda b, c: (b, cidx(c), 0, 0)), st],
        out_specs=[tok(SSD_DI), st],
        out_shape=[jax.ShapeDtypeStruct((bn, seq_len, SSD_DI), F32),
                   jax.ShapeDtypeStruct((bn, SSD_H, SSD_P, SSD_N), F32)],
        scratch_shapes=[pltpu.VMEM((SSD_H, SSD_P, SSD_N), F32)],
        compiler_params=pltpu.CompilerParams(
            dimension_semantics=("parallel", "arbitrary"),
            vmem_limit_bytes=V7X_VMEM_LIMIT),
        name="ssd_scan_bwd" if rev else "ssd_scan_fwd",
    )(prm, xs, bm, cm, dt_col, dt_row, s0)


def _ssd_post_body(yf_ref, yb_ref, xs_ref, z_ref, d_ref, g_ref, o_ref):
    y = (yf_ref[...] + yb_ref[...] + d_ref[...] * xs_ref[...]) * jax.nn.silu(z_ref[...])
    o_ref[...] = y * lax.rsqrt(jnp.mean(y * y, axis=-1, keepdims=True) + EPS) * g_ref[...]


def ssd_post(yf, yb, xs, z, d_vec, norm_g):
    m = yf.shape[0]
    tm = _pick(m, (SSD_TT, 128, 64, 32, 16, 8))
    row = pl.BlockSpec((tm, SSD_DI), lambda i: (i, 0))
    vec = pl.BlockSpec((1, SSD_DI), lambda i: (0, 0))
    return pl.pallas_call(
        _ssd_post_body, grid=(m // tm,), in_specs=[row, row, row, row, vec, vec], out_specs=row,
        out_shape=jax.ShapeDtypeStruct((m, SSD_DI), F32),
        compiler_params=pltpu.CompilerParams(
            dimension_semantics=("parallel",), vmem_limit_bytes=V7X_VMEM_LIMIT),
        name="ssd_post",
    )(yf, yb, xs, z, d_vec, norm_g.reshape(1, SSD_DI))


def ssd_stream(z, xbc, dt_raw, state, conv_w, conv_b, dt_bias, a_log, d, norm_g):
    bn, seq_len, _ = z.shape
    xbc = conv_silu(xbc, conv_w, conv_b)
    xs, bm, cm = jnp.split(xbc, [SSD_DI, SSD_DI + SSD_G * SSD_N], axis=-1)
    nc = seq_len // SSD_CHUNK
    dtr = dt_raw.reshape(bn, seq_len, 2, SSD_H)
    outs = []
    for dr in range(2):
        dt_col = dtr[:, :, dr]
        dt_row = jnp.transpose(dt_col.reshape(bn, nc, SSD_CHUNK, SSD_H), (0, 1, 3, 2))
        prm = jnp.stack([-jnp.exp(a_log[dr]), dt_bias[dr]]).astype(F32)
        outs.append(ssd_scan_pallas(xs, bm, cm, dt_col, dt_row, prm, state[dr], dr == 1))
    (yf, hf), (yb, hb) = outs
    m = bn * seq_len
    d_vec = jnp.repeat(d, SSD_P).reshape(1, SSD_DI)
    out = ssd_post(yf.reshape(m, SSD_DI), yb.reshape(m, SSD_DI), xs.reshape(m, SSD_DI),
                   z.reshape(m, SSD_DI), d_vec, norm_g)
    return out.reshape(bn, seq_len, SSD_DI), (hf, hb)


def gated_delta_rule(q, k, v, beta, g, s0):
    bn, seq_len, nh, dk = q.shape
    dv = v.shape[-1]
    qn = DN_CHUNK
    nc = seq_len // qn

    def chunks(t):
        return jnp.swapaxes(t.reshape((bn, nc, qn) + t.shape[2:]), 2, 3)
    qc, kc, vc, bc, gc = (chunks(t) for t in (q, k, v, beta, g))
    gcum = jnp.cumsum(gc, axis=-1)
    idx = jnp.arange(qn)
    incl = idx[:, None] >= idx[None, :]
    strict = idx[:, None] > idx[None, :]
    dmat = jnp.exp(jnp.where(incl, gcum[..., :, None] - gcum[..., None, :], -jnp.inf))
    kb = kc * bc[..., None]
    m = jnp.where(strict, jnp.einsum('bnhid,bnhjd->bnhij', kb, kc) * dmat, 0.0)
    rhs = jnp.concatenate([vc * bc[..., None], kb * jnp.exp(gcum)[..., None]], axis=-1)
    nmat = -m
    tinv = jnp.eye(qn, dtype=m.dtype) + nmat
    npow = nmat
    for _ in range(DN_NEUMANN):
        npow = jnp.einsum('...ij,...jk->...ik', npow, npow, precision=lax.Precision.HIGHEST)
        tinv = tinv + jnp.einsum('...ij,...jk->...ik', tinv, npow, precision=lax.Precision.HIGHEST)
    sol = jnp.einsum('...ij,...jk->...ik', tinv, rhs)
    u, w = sol[..., :dv], sol[..., dv:]
    qk = jnp.einsum('bnhid,bnhjd->bnhij', qc, kc) * dmat
    q_dec = qc * jnp.exp(gcum)[..., None]
    k_tail = kc * jnp.exp(gcum[..., -1:] - gcum)[..., None]
    tot = jnp.exp(gcum[..., -1])

    def step(s, inp):
        u_c, w_c, qk_c, qd_c, kt_c, tot_c = inp
        v_new = u_c - jnp.einsum('bhqk,bhkv->bhqv', w_c, s)
        o = jnp.einsum('bhqk,bhkv->bhqv', qd_c, s) + jnp.einsum('bhij,bhjv->bhiv', qk_c, v_new)
        s = s * tot_c[..., None, None] + jnp.einsum('bhqk,bhqv->bhkv', kt_c, v_new)
        return s, o
    xs = tuple(jnp.moveaxis(t, 1, 0) for t in (u, w, qk, q_dec, k_tail, tot))
    s_last, o = lax.scan(step, s0, xs)
    o = jnp.swapaxes(jnp.moveaxis(o, 0, 1), 2, 3).reshape(bn, seq_len, nh, dv)
    return o, s_last


DN_NEUMANN = 5


def deltanet_stream(qkv, z, a_raw, b_raw, state, conv_w, conv_b, dt_bias, a_log, norm_g):
    bn, seq_len, _ = z.shape
    qkv = jax.nn.silu(dwconv(qkv, conv_w, conv_b))
    q, k, v = jnp.split(qkv, [DN_H * DN_DK, 2 * DN_H * DN_DK], axis=-1)
    q = l2norm(q.reshape(bn, seq_len, DN_H, DN_DK)) * (DN_DK ** -0.5)
    k = l2norm(k.reshape(bn, seq_len, DN_H, DN_DK))
    v = v.reshape(bn, seq_len, DN_H, DN_DV)
    beta = jax.nn.sigmoid(b_raw.reshape(bn, seq_len, 2, DN_H))
    g = -jnp.exp(a_log) * jax.nn.softplus(a_raw.reshape(bn, seq_len, 2, DN_H) + dt_bias)
    of, sf = gated_delta_rule(q, k, v, beta[:, :, 0], g[:, :, 0], state[0])
    ob, sb = gated_delta_rule(flip(q), flip(k), flip(v), flip(beta[:, :, 1]), flip(g[:, :, 1]), state[1])
    o = rmsnorm(of + flip(ob), norm_g)
    o = o.reshape(bn, seq_len, DN_H * DN_DV) * jax.nn.silu(z)
    return o, (sf, sb)


RG_TT = 256


def _rglru_body(xp_ref, x_ref, xn_ref, h0_ref, cw_ref, cb_ref, wa_ref, ba_ref, wx_ref, bx_ref, c_ref,
                h_ref, hf_ref, carry_s, *, rev, tt, nblk):
    i = pl.program_id(1)
    blk = (nblk - 1 - i) if rev else i

    @pl.when(i == 0)
    def _():
        carry_s[...] = h0_ref[0]

    x = x_ref[0]
    xp = xp_ref[0, tt - V7X_SUBLANES:, :] * (blk > 0).astype(F32)
    xn = xn_ref[0, :V7X_SUBLANES, :] * (blk < nblk - 1).astype(F32)
    xc = jnp.concatenate([xp, x, xn], axis=0)
    base = V7X_SUBLANES - CONV_PAD[0]
    xh = cb_ref[...] + sum(cw_ref[j:j + 1, :] * xc[base + j:base + j + tt, :] for j in range(CONV_W))

    xb = xh.astype(BF16)
    rs, is_ = [], []
    for h in range(RG_H):
        xs = xb[:, RG_BS * h:RG_BS * (h + 1)]
        rs.append(jnp.dot(xs, wa_ref[h], preferred_element_type=F32))
        is_.append(jnp.dot(xs, wx_ref[h], preferred_element_type=F32))
    r = jax.nn.sigmoid(jnp.concatenate(rs, axis=1) + ba_ref[...])
    ig = jax.nn.sigmoid(jnp.concatenate(is_, axis=1) + bx_ref[...])
    log_a = r * c_ref[...]
    a = jnp.exp(log_a)
    b = jnp.sqrt(1.0 - jnp.exp(2.0 * log_a)) * (ig * xh)

    row = lax.broadcasted_iota(jnp.int32, (tt, RG_W), 0)
    s = 1
    while s < tt:
        valid = (row < tt - s) if rev else (row >= s)
        shift = (tt - s) if rev else s
        a_s = jnp.where(valid, pltpu.roll(a, shift, 0), 1.0)
        b_s = jnp.where(valid, pltpu.roll(b, shift, 0), 0.0)
        b = a * b_s + b
        a = a * a_s
        s *= 2
    hcur = b + a * carry_s[...]
    h_ref[0] = hcur
    last = 0 if rev else tt - 1
    carry_s[...] = hcur[last:last + 1, :]
    hf_ref[0] = hcur[last:last + 1, :]


def rglru_scan(xr, h0, conv_w, conv_b, wa, ba, wx, bx, lam, rev):
    bn, seq_len, _ = xr.shape
    tt = min(RG_TT, seq_len)
    nblk = seq_len // tt
    blk = (lambda i: nblk - 1 - i) if rev else (lambda i: i)
    cur = lambda b, i: (b, blk(i), 0)
    prev = lambda b, i: (b, jnp.maximum(blk(i) - 1, 0), 0)
    nxt = lambda b, i: (b, jnp.minimum(blk(i) + 1, nblk - 1), 0)
    vec = pl.BlockSpec((1, RG_W), lambda b, i: (0, 0))
    wspec = pl.BlockSpec((RG_H, RG_BS, RG_BS), lambda b, i: (0, 0, 0))
    st = pl.BlockSpec((1, 1, RG_W), lambda b, i: (b, 0, 0))
    c = (-RG_C * jax.nn.softplus(-lam)).reshape(1, RG_W)
    h, hf = pl.pallas_call(
        functools.partial(_rglru_body, rev=rev, tt=tt, nblk=nblk),
        grid=(bn, nblk),
        in_specs=[pl.BlockSpec((1, tt, RG_W), prev), pl.BlockSpec((1, tt, RG_W), cur),
                  pl.BlockSpec((1, tt, RG_W), nxt), st,
                  pl.BlockSpec((CONV_W, RG_W), lambda b, i: (0, 0)), vec,
                  wspec, vec, wspec, vec, vec],
        out_specs=[pl.BlockSpec((1, tt, RG_W), cur), st],
        out_shape=[jax.ShapeDtypeStruct((bn, seq_len, RG_W), F32),
                   jax.ShapeDtypeStruct((bn, 1, RG_W), F32)],
        scratch_shapes=[pltpu.VMEM((1, RG_W), F32)],
        compiler_params=pltpu.CompilerParams(
            dimension_semantics=("parallel", "arbitrary"),
            vmem_limit_bytes=V7X_VMEM_LIMIT),
        name="rglru_bwd" if rev else "rglru_fwd",
    )(xr, xr, xr, h0.reshape(bn, 1, RG_W), conv_w, conv_b.reshape(1, RG_W),
      wa.astype(BF16), ba.reshape(1, RG_W), wx.astype(BF16), bx.reshape(1, RG_W), c)
    return h, hf.reshape(bn, RG_W)


def _rg_combine_body(hf_ref, hb_ref, g_ref, o_ref):
    o_ref[...] = (hf_ref[...] + hb_ref[...]) * jax.nn.gelu(g_ref[...])


def rg_combine(hf, hb, gate):
    m = hf.shape[0]
    tm = _pick(m, (1024, 512, 256, 128, 64, 32, 16, 8))
    row = pl.BlockSpec((tm, RG_W), lambda i: (i, 0))
    return pl.pallas_call(
        _rg_combine_body, grid=(m // tm,), in_specs=[row, row, row], out_specs=row,
        out_shape=jax.ShapeDtypeStruct((m, RG_W), F32),
        compiler_params=pltpu.CompilerParams(dimension_semantics=("parallel",)),
        name="rg_combine",
    )(hf, hb, gate)


def rglru_stream(xr, gate, state, conv_w, conv_b, wa, ba, wx, bx, lam):
    bn, seq_len, _ = xr.shape
    hf, sf = rglru_scan(xr, state[0], conv_w, conv_b, wa[0], ba[0], wx[0], bx[0], lam[0], False)
    hb, sb = rglru_scan(xr, state[1], conv_w, conv_b, wa[1], ba[1], wx[1], bx[1], lam[1], True)
    m = bn * seq_len
    y = rg_combine(hf.reshape(m, RG_W), hb.reshape(m, RG_W), gate.reshape(m, RG_W))
    return y.reshape(bn, seq_len, RG_W), (sf, sb)


def even_mixer(hx, hc, w_in, w_out, s5_params, ssd_params, need_ctx):
    bn = hx.shape[0]

    def stream(h, st_s5, st_ssd):
        p = mm3(h, w_in)
        u, z, xbc, dt_raw = jnp.split(p, EV_CUTS, axis=-1)
        ya, st_a = s5_stream(u, st_s5, *s5_params)
        yb, st_b = ssd_stream(z, xbc, dt_raw, st_ssd, *ssd_params)
        return jnp.concatenate([ya, yb], axis=-1), st_a, st_b
    zs5 = jnp.zeros((bn, S5_LANES), F32)
    zssd = jnp.zeros((bn, SSD_H, SSD_P, SSD_N), F32)
    yc, st_a, st_b = stream(hc, (zs5, zs5, zs5, zs5), (zssd, zssd))
    yx, _, _ = stream(hx, st_a, st_b)
    return mm3(yx, w_out), (mm3(yc, w_out) if need_ctx else None)


def odd_mixer(hx, hc, w_in, w_out, dn_params, rg_params, need_ctx):
    bn = hx.shape[0]

    def stream(h, st_dn, st_rg):
        p = mm3(h, w_in)
        qkv, z, a_raw, b_raw, xr, gate = jnp.split(p, OD_CUTS, axis=-1)
        yd, st_d = deltanet_stream(qkv, z, a_raw, b_raw, st_dn, *dn_params)
        yr, st_r = rglru_stream(xr, gate, st_rg, *rg_params)
        return jnp.concatenate([yd, yr], axis=-1), st_d, st_r
    zdn = jnp.zeros((bn, DN_H, DN_DK, DN_DV), F32)
    zrg = jnp.zeros((bn, RG_W), F32)
    yc, st_d, st_r = stream(hc, (zdn, zdn), (zrg, zrg))
    yx, _, _ = stream(hx, st_d, st_r)
    return mm3(yx, w_out), (mm3(yc, w_out) if need_ctx else None)


PEER_TM = 512
PEER_EB = 512
PEER_RT = 16
PEER_SEL = PEER_H * PEER_TOPK


def _peer_body(flags_ref, ht_ref, u_ref, vt_ref, s1_ref, s2_ref, e1_ref, e2_ref, tau_ref,
               pidx_ref, ridx_ref, gate_ref, o_ref, st_s, g_s, *, tm, eb):
    i = pl.program_id(0)
    j = pl.program_id(1)
    npk = eb // PEER_NK

    @pl.when(j == 0)
    def _():
        o_ref[...] = jnp.zeros_like(o_ref)

    @pl.when(flags_ref[i] == 0)
    def _():
        st_s[...] = jnp.dot(u_ref[...], ht_ref[...], preferred_element_type=F32)
        for lg in range(tm // V7X_LANES):
            ls = slice(V7X_LANES * lg, V7X_LANES * (lg + 1))
            t_rows = [tau_ref[h:h + 1, ls] for h in range(PEER_H)]
            a_tiles = [s1_ref[j * npk + pk, :, ls] for pk in range(npk)]
            z_tiles = [e1_ref[j * npk + pk, :, ls] for pk in range(npk)]
            for rt in range(PEER_NK // PEER_RT):
                rs = slice(rt * PEER_RT, (rt + 1) * PEER_RT)
                s2 = [s2_ref[h, rs, ls] for h in range(PEER_H)]
                e2 = [e2_ref[h, rs, ls] for h in range(PEER_H)]
                for pk in range(npk):
                    acc = jnp.zeros((PEER_RT, V7X_LANES), F32)
                    for h in range(PEER_H):
                        hit = a_tiles[pk][h:h + 1, :] + s2[h] >= t_rows[h]
                        acc = acc + jnp.where(hit, z_tiles[pk][h:h + 1, :] * e2[h], 0.0)
                    rows = slice(pk * PEER_NK + rt * PEER_RT, pk * PEER_NK + (rt + 1) * PEER_RT)
                    g_s[rows, ls] = (acc * jax.nn.gelu(st_s[rows, ls])).astype(BF16)
        o_ref[...] += jnp.dot(vt_ref[...], g_s[...], preferred_element_type=F32)

    @pl.when(flags_ref[i] != 0)
    def _():
        st_s[...] = jnp.dot(u_ref[...], ht_ref[...], preferred_element_type=F32)
        riota = lax.broadcasted_iota(jnp.int32, (PEER_NK, V7X_LANES), 0)
        for lg in range(tm // V7X_LANES):
            ls = slice(V7X_LANES * lg, V7X_LANES * (lg + 1))
            for pk in range(npk):
                p = j * npk + pk

                def kbody(k8, w, ls=ls, p=p):
                    k0 = pl.multiple_of(k8 * V7X_SUBLANES, V7X_SUBLANES)
                    ptile = pidx_ref[pl.ds(k0, V7X_SUBLANES), ls]
                    rtile = ridx_ref[pl.ds(k0, V7X_SUBLANES), ls]
                    gtile = gate_ref[pl.ds(k0, V7X_SUBLANES), ls]
                    ctile = jnp.where(ptile == p, gtile, 0.0)
                    for k in range(V7X_SUBLANES):
                        w = w + jnp.where(rtile[k:k + 1, :] == riota, ctile[k:k + 1, :], 0.0)
                    return w

                w = lax.fori_loop(0, PEER_SEL // V7X_SUBLANES, kbody,
                                  jnp.zeros((PEER_NK, V7X_LANES), F32))
                rows = slice(pk * PEER_NK, (pk + 1) * PEER_NK)
                g_s[rows, ls] = (w * jax.nn.gelu(st_s[rows, ls])).astype(BF16)
        o_ref[...] += jnp.dot(vt_ref[...], g_s[...], preferred_element_type=F32)


def peer_experts(flags, ht, u_bf, vt_bf, s1t, s2t, e1t, e2t, taut, pidx, ridx, gate, tm):
    dm, t = ht.shape
    eb = PEER_EB
    head3 = pl.BlockSpec((PEER_H, PEER_NK, tm), lambda i, j, f: (0, 0, i))
    key3 = pl.BlockSpec((PEER_NK, PEER_H, tm), lambda i, j, f: (0, 0, i))
    sel2 = pl.BlockSpec((PEER_SEL, tm), lambda i, j, f: (0, i))
    grid_spec = pltpu.PrefetchScalarGridSpec(
        num_scalar_prefetch=1,
        grid=(t // tm, PEER_E // eb),
        in_specs=[pl.BlockSpec((dm, tm), lambda i, j, f: (0, i)),
                  pl.BlockSpec((eb, dm), lambda i, j, f: (j, 0)),
                  pl.BlockSpec((dm, eb), lambda i, j, f: (0, j)),
                  key3, head3, key3, head3,
                  pl.BlockSpec((PEER_H, tm), lambda i, j, f: (0, i)),
                  sel2, sel2, sel2],
        out_specs=pl.BlockSpec((dm, tm), lambda i, j, f: (0, i)),
        scratch_shapes=[pltpu.VMEM((eb, tm), F32), pltpu.VMEM((eb, tm), BF16)])
    return pl.pallas_call(
        functools.partial(_peer_body, tm=tm, eb=eb),
        grid_spec=grid_spec,
        out_shape=jax.ShapeDtypeStruct((dm, t), F32),
        compiler_params=pltpu.CompilerParams(
            dimension_semantics=("parallel", "arbitrary"),
            vmem_limit_bytes=V7X_VMEM_LIMIT),
        name="peer_experts",
    )(flags, ht, u_bf, vt_bf, s1t, s2t, e1t, e2t, taut, pidx, ridx, gate)


def _topk_rows(x, k):
    n = x.shape[0]
    iota = lax.broadcasted_iota(jnp.int32, x.shape, 0)
    vals, idxs = [], []
    for r in range(k):
        m = jnp.max(x, axis=0, keepdims=True)
        idx = jnp.min(jnp.where(x == m, iota, n), axis=0, keepdims=True)
        vals.append(m)
        idxs.append(idx)
        if r < k - 1:
            x = jnp.where(iota == idx, -jnp.inf, x)
    return vals, idxs


def _peer_select_body(q_ref, k1_ref, k2_ref, s1_ref, s2_ref, e1_ref, e2_ref, tau_ref, tie_ref,
                      pidx_ref, ridx_ref, gate_ref):
    qb = q_ref[...].astype(BF16)
    dn = (((1,), (1,)), ((), ()))
    s1 = lax.dot_general(k1_ref[0], qb[:, :PEER_HALF], dn, preferred_element_type=F32)
    s2 = lax.dot_general(k2_ref[0], qb[:, PEER_HALF:], dn, preferred_element_type=F32)
    t1, i1 = _topk_rows(s1, PEER_TOPK + 1)
    t2, i2 = _topk_rows(s2, PEER_TOPK + 1)
    t1blk = jnp.concatenate(t1[:PEER_TOPK], axis=0)
    t2blk = jnp.concatenate(t2[:PEER_TOPK], axis=0)
    hs = V7X_SUBLANES
    cand = jnp.concatenate([t1[0] + t2blk] + [t1[a] + t2blk[:hs] for a in range(1, hs)]
                           + [t1blk[hs:] + t2[0]], axis=0)
    top, ci = _topk_rows(cand, PEER_TOPK)
    topb = jnp.concatenate(top, axis=0)
    rowb = jnp.concatenate(ci, axis=0)
    mid = rowb - PEER_TOPK
    ca = jnp.where(rowb < PEER_TOPK, 0, jnp.where(mid < hs * (hs - 1), 1 + (mid >> 3), mid - hs * (hs - 2)))
    cb = jnp.where(rowb < PEER_TOPK, rowb, jnp.where(mid < hs * (hs - 1), mid & (hs - 1), 0))
    pid = jnp.zeros_like(rowb)
    rid = jnp.zeros_like(rowb)
    for a in range(PEER_TOPK):
        pid = pid + jnp.where(ca == a, i1[a], 0)
        rid = rid + jnp.where(cb == a, i2[a], 0)
    ex = jnp.exp(topb - top[0])
    zsum = jnp.sum(ex, axis=0, keepdims=True)
    tau = top[PEER_TOPK - 1]
    hits = sum(((t1[a] + t2blk) >= tau).astype(jnp.int32) for a in range(PEER_TOPK))
    cnt = jnp.sum(hits, axis=0, keepdims=True)
    tie = ((cnt != PEER_TOPK) | (t1[PEER_TOPK] + t2[0] >= tau) | (t1[0] + t2[PEER_TOPK] >= tau))
    s1_ref[0] = s1
    s2_ref[0] = s2
    e1_ref[0] = jnp.exp(s1 - t1[0]) / zsum
    e2_ref[0] = jnp.exp(s2 - t2[0])
    tau_ref[0] = tau
    tie_ref[0] = tie.astype(jnp.int32)
    pidx_ref[...] = pid
    ridx_ref[...] = rid
    gate_ref[...] = ex / zsum


def peer_select(q, k1_bf, k2_bf):
    t = q.shape[0]
    tl = V7X_LANES
    head3 = pl.BlockSpec((1, PEER_NK, tl), lambda i, h: (h, 0, i))
    row3 = pl.BlockSpec((1, 1, tl), lambda i, h: (h, 0, i))
    sel2 = pl.BlockSpec((PEER_TOPK, tl), lambda i, h: (h, i))
    key_spec = pl.BlockSpec((1, PEER_NK, PEER_HALF), lambda i, h: (h, 0, 0))
    f3 = jax.ShapeDtypeStruct((PEER_H, PEER_NK, t), F32)
    return pl.pallas_call(
        _peer_select_body,
        grid=(t // tl, PEER_H),
        in_specs=[pl.BlockSpec((tl, PEER_DK), lambda i, h: (i, h)), key_spec, key_spec],
        out_specs=[head3, head3, head3, head3, row3, row3, sel2, sel2, sel2],
        out_shape=[f3, f3, f3, f3,
                   jax.ShapeDtypeStruct((PEER_H, 1, t), F32),
                   jax.ShapeDtypeStruct((PEER_H, 1, t), jnp.int32),
                   jax.ShapeDtypeStruct((PEER_SEL, t), jnp.int32),
                   jax.ShapeDtypeStruct((PEER_SEL, t), jnp.int32),
                   jax.ShapeDtypeStruct((PEER_SEL, t), F32)],
        compiler_params=pltpu.CompilerParams(
            dimension_semantics=("parallel", "arbitrary"),
            vmem_limit_bytes=V7X_VMEM_LIMIT),
        name="peer_select",
    )(q, k1_bf, k2_bf)


def peer(h, w_q, k1, k2, u_bf, vt_bf, tm=PEER_TM):
    bn, seq_len, dm = h.shape
    t = bn * seq_len
    q = mm(h.reshape(t, dm), w_q)
    s1t, s2t, e1t, e2t, taut, tiet, pidx, ridx, gate = peer_select(q, k1.astype(BF16), k2.astype(BF16))
    flags = jnp.any(tiet.reshape(PEER_H, t // tm, tm) != 0, axis=(0, 2)).astype(jnp.int32)
    outt = peer_experts(
        flags, h.reshape(t, dm).T.astype(BF16), u_bf, vt_bf,
        jnp.transpose(s1t, (1, 0, 2)), s2t, jnp.transpose(e1t, (1, 0, 2)), e2t, taut.reshape(PEER_H, t),
        pidx, ridx, gate, tm)
    return outt.T.reshape(bn, seq_len, dm)


def kernel(x, c, ctx, c_ctx, ada_w, ada_b, norm1_g, norm2_g, final_g, ev_w_in, ev_w_out, s5_lam_re, s5_lam_im, s5_log_dt, s5_b_re, s5_b_im, s5_c_re, s5_c_im, s5_d, s5_glu_w, s5_glu_b, ssd_conv_w, ssd_conv_b, ssd_dt_bias, ssd_a_log, ssd_d, ssd_norm_g, od_w_in, od_w_out, dn_conv_w, dn_conv_b, dn_dt_bias, dn_a_log, dn_norm_g, rg_conv_w, rg_conv_b, rg_wa, rg_ba, rg_wx, rg_bx, rg_lam, peer_wq, peer_k1, peer_k2, peer_u, peer_v):
    bn, seq_len, _ = x.shape
    rows = seq_len // GRID_W
    sc = jax.nn.silu(c)
    scc = jax.nn.silu(c_ctx)
    for layer in range(DEPTH):
        j = layer // 2
        need_ctx = layer < DEPTH - 1
        mx = (sc @ ada_w[layer] + ada_b[layer]).reshape(bn, 6, 1, D_MODEL)
        mc = (scc @ ada_w[layer] + ada_b[layer]).reshape(6, D_MODEL)
        hx = modulate(rmsnorm(x, norm1_g[layer]), mx[:, 0], mx[:, 1])
        hc = modulate(rmsnorm(ctx, norm1_g[layer]), mc[0], mc[1])
        if layer % 2 == 0:
            s5_params = (s5_lam_re[j], s5_lam_im[j], s5_log_dt[j], s5_b_re[j], s5_b_im[j],
                         s5_c_re[j], s5_c_im[j], s5_d[j], s5_glu_w[j], s5_glu_b[j])
            ssd_params = (ssd_conv_w[j], ssd_conv_b[j], ssd_dt_bias[j], ssd_a_log[j],
                          ssd_d[j], ssd_norm_g[j])
            ox, oc = even_mixer(hx, hc, ev_w_in[j], ev_w_out[j], s5_params, ssd_params, need_ctx)
        else:
            dn_params = (dn_conv_w[j], dn_conv_b[j], dn_dt_bias[j], dn_a_log[j], dn_norm_g[j])
            rg_params = (rg_conv_w[j], rg_conv_b[j], rg_wa[j], rg_ba[j], rg_wx[j], rg_bx[j], rg_lam[j])
            ox, oc = odd_mixer(to_col_major(hx, rows), hc, od_w_in[j], od_w_out[j],
                               dn_params, rg_params, need_ctx)
            ox = to_row_major(ox, rows)
        x = x + mx[:, 2] * ox
        hx = modulate(rmsnorm(x, norm2_g[layer]), mx[:, 3], mx[:, 4])
        u_bf = peer_u[layer].astype(BF16)
        vt_bf = peer_v[layer].T.astype(BF16)
        x = x + mx[:, 5] * peer(hx, peer_wq[layer], peer_k1[layer], peer_k2[layer], u_bf, vt_bf)
        if need_ctx:
            ctx = ctx + mc[2] * oc
            hc = modulate(rmsnorm(ctx, norm2_g[layer]), mc[3], mc[4])
            ctx = ctx + mc[5] * peer(hc, peer_wq[layer], peer_k1[layer], peer_k2[layer], u_bf, vt_bf)
    return rmsnorm(x, final_g)
```

```python
import functools
import math

import jax
import jax.numpy as jnp
from jax import lax
from jax.experimental import pallas as pl
from jax.experimental.pallas import tpu as pltpu

D_MODEL = 2048
DEPTH = 2
GRID_W = 64
EPS = 1e-6
CONV_W = 4
CONV_PAD = (CONV_W // 2, CONV_W - 1 - CONV_W // 2)

S5_W = D_MODEL // 2
S5_CH = 16
S5_G = S5_W // S5_CH
S5_N = 64
SSD_P = 64
SSD_H = D_MODEL // SSD_P
SSD_DI = SSD_H * SSD_P
SSD_N = 128
SSD_G = 4
SSD_CHUNK = 128
SSD_CONV_CH = SSD_DI + 2 * SSD_G * SSD_N
EV_IN = S5_W + SSD_DI + SSD_CONV_CH + 2 * SSD_H
EV_MIX = S5_W + SSD_DI
DN_DK = 128
DN_DV = 128
DN_H = D_MODEL // DN_DK
DN_CHUNK = 64
DN_QKV = DN_H * (2 * DN_DK + DN_DV)
RG_W = D_MODEL // 2
RG_H = 8
RG_BS = RG_W // RG_H
RG_C = 8.0
OD_IN = DN_QKV + DN_H * DN_DV + 4 * DN_H + 2 * RG_W
OD_MIX = DN_H * DN_DV + RG_W
OD_CUTS = [DN_QKV,
           DN_QKV + DN_H * DN_DV,
           DN_QKV + DN_H * DN_DV + 2 * DN_H,
           DN_QKV + DN_H * DN_DV + 4 * DN_H,
           DN_QKV + DN_H * DN_DV + 4 * DN_H + RG_W]
EV_CUTS = [S5_W, S5_W + SSD_DI, S5_W + SSD_DI + SSD_CONV_CH]
PEER_H = 8
PEER_DK = 256
PEER_HALF = PEER_DK // 2
PEER_NK = 128
PEER_E = PEER_NK * PEER_NK
PEER_TOPK = 16

F32 = jnp.float32
BF16 = jnp.bfloat16

V7X_LANES = 128
V7X_SUBLANES = 8
V7X_VMEM_LIMIT = 56 * 1024 * 1024
MM_TN = 512


def _mm_body(x_ref, w_ref, o_ref):
    o_ref[...] = jnp.dot(x_ref[...], w_ref[...], preferred_element_type=F32).astype(o_ref.dtype)


def _pick(n, cands):
    for c in cands:
        if n % c == 0:
            return c
    raise ValueError(f"no tile for {n}")


def mm(x, w, out_dtype=F32):
    m, k = x.shape
    n = w.shape[1]
    n_pad = -n % (MM_TN if n > MM_TN else V7X_LANES)
    if n_pad:
        w = jnp.pad(w, ((0, 0), (0, n_pad)))
    np_ = n + n_pad
    tm = _pick(m, (1024, 512, 256, 128, 64, 32, 16, 8))
    tn = _pick(np_, (MM_TN, 256, 128))
    out = pl.pallas_call(
        _mm_body,
        grid=(m // tm, np_ // tn),
        in_specs=[pl.BlockSpec((tm, k), lambda i, j: (i, 0)),
                  pl.BlockSpec((k, tn), lambda i, j: (0, j))],
        out_specs=pl.BlockSpec((tm, tn), lambda i, j: (i, j)),
        out_shape=jax.ShapeDtypeStruct((m, np_), out_dtype),
        compiler_params=pltpu.CompilerParams(
            dimension_semantics=("parallel", "arbitrary"),
            vmem_limit_bytes=V7X_VMEM_LIMIT),
        name="mm",
    )(x.astype(BF16), w.astype(BF16))
    return out[:, :n] if n_pad else out


def mm3(h, w):
    b, l, d = h.shape
    return mm(h.reshape(b * l, d), w).reshape(b, l, w.shape[1])


def rmsnorm(x, g):
    xf = x.astype(F32)
    y = xf * lax.rsqrt(jnp.mean(xf * xf, axis=-1, keepdims=True) + EPS)
    return (y * g.astype(F32)).astype(x.dtype)


def l2norm(x):
    return x * lax.rsqrt(jnp.sum(x * x, axis=-1, keepdims=True) + EPS)


def modulate(h, shift, scale):
    return h * (1 + scale) + shift


def dwconv(x, w, b):
    y = lax.conv_general_dilated(x, w[:, None, :], (1,), [CONV_PAD],
                                 dimension_numbers=('NWC', 'WIO', 'NWC'),
                                 feature_group_count=x.shape[-1])
    return y + b


def flip(t):
    return jnp.flip(t, axis=1)


def to_col_major(x, rows):
    b, l, d = x.shape
    return x.reshape(b, rows, GRID_W, d).transpose(0, 2, 1, 3).reshape(b, l, d)


def to_row_major(x, rows):
    b, l, d = x.shape
    return x.reshape(b, GRID_W, rows, d).transpose(0, 2, 1, 3).reshape(b, l, d)


S5_LANES = S5_G * S5_N
S5_TILE = 8
S5_SHIFTS = (1, 2, 4)
S5_BG = 8
S5_CG = 16
S5_CHUNK = 1024


def _s5_scan_body(u_ref, h0r_ref, h0i_ref, bdr_ref, bdi_ref, cdr_ref, cdi_ref, coef_ref,
                  y_ref, hfr_ref, hfi_ref, xr_s, xi_s, cr_s, ci_s, *, rev, tt):
    @pl.when(pl.program_id(1) == 0)
    def _():
        cr_s[...] = h0r_ref[0]
        ci_s[...] = h0i_ref[0]

    ub = u_ref[0].astype(BF16)
    kin = S5_BG * S5_CH
    kout = S5_BG * S5_N
    for j in range(S5_G // S5_BG):
        uj = ub[:, kin * j:kin * (j + 1)]
        xr_s[:, kout * j:kout * (j + 1)] = jnp.dot(uj, bdr_ref[j], preferred_element_type=F32)
        xi_s[:, kout * j:kout * (j + 1)] = jnp.dot(uj, bdi_ref[j], preferred_element_type=F32)

    nt = tt // S5_TILE
    carry_row = 0 if rev else S5_TILE - 1

    def tile(kk, carry):
        k = (nt - 1 - kk) if rev else kk
        r0 = pl.multiple_of(k * S5_TILE, S5_TILE)
        for c in range(S5_LANES // S5_CHUNK):
            sl = slice(S5_CHUNK * c, S5_CHUNK * (c + 1))
            xr = xr_s[pl.ds(r0, S5_TILE), sl]
            xi = xi_s[pl.ds(r0, S5_TILE), sl]
            for idx, sh in enumerate(S5_SHIFTS):
                pr = coef_ref[2 * idx, :, sl]
                pim = coef_ref[2 * idx + 1, :, sl]
                rs = (S5_TILE - sh) if rev else sh
                sr = pltpu.roll(xr, rs, 0)
                si = pltpu.roll(xi, rs, 0)
                xr, xi = xr + (pr * sr - pim * si), xi + (pr * si + pim * sr)
            pr = coef_ref[2 * len(S5_SHIFTS), :, sl]
            pim = coef_ref[2 * len(S5_SHIFTS) + 1, :, sl]
            hr = cr_s[:, sl]
            hi = ci_s[:, sl]
            xr, xi = xr + (pr * hr - pim * hi), xi + (pr * hi + pim * hr)
            xr_s[pl.ds(r0, S5_TILE), sl] = xr
            xi_s[pl.ds(r0, S5_TILE), sl] = xi
            cr_s[:, sl] = xr[carry_row:carry_row + 1, :]
            ci_s[:, sl] = xi[carry_row:carry_row + 1, :]
        return carry

    lax.fori_loop(0, nt, tile, 0)

    kc = S5_CG * S5_N
    nc = S5_CG * S5_CH
    for j in range(S5_G // S5_CG):
        sr = xr_s[:, kc * j:kc * (j + 1)].astype(BF16)
        si = xi_s[:, kc * j:kc * (j + 1)].astype(BF16)
        y_ref[0, :, nc * j:nc * (j + 1)] = (
            jnp.dot(sr, cdr_ref[j], preferred_element_type=F32)
            - jnp.dot(si, cdi_ref[j], preferred_element_type=F32))
    hfr_ref[0] = cr_s[...]
    hfi_ref[0] = ci_s[...]


def _s5_direction_params(lam_re, lam_im, log_dt, b_re, b_im, c_re, c_im, rev):
    dt = jnp.exp(log_dt)[:, None]
    mag = jnp.exp(lam_re * dt)
    ar, ai = mag * jnp.cos(lam_im * dt), mag * jnp.sin(lam_im * dt)
    den = lam_re * lam_re + lam_im * lam_im
    cr = ((ar - 1) * lam_re + ai * lam_im) / den
    ci = (ai * lam_re - (ar - 1) * lam_im) / den
    bb_re = cr[..., None] * b_re - ci[..., None] * b_im
    bb_im = cr[..., None] * b_im + ci[..., None] * b_re

    def bdiag_in(bb):
        t = bb.reshape(S5_G // S5_BG, S5_BG, S5_N, S5_CH)
        eye = jnp.eye(S5_BG, dtype=F32)
        return jnp.einsum('jgnc,gh->jgchn', t, eye).reshape(
            S5_G // S5_BG, S5_BG * S5_CH, S5_BG * S5_N).astype(BF16)

    def bdiag_out(cc):
        t = cc.reshape(S5_G // S5_CG, S5_CG, S5_CH, S5_N)
        eye = jnp.eye(S5_CG, dtype=F32)
        return jnp.einsum('jgcn,gh->jgnhc', t, eye).reshape(
            S5_G // S5_CG, S5_CG * S5_N, S5_CG * S5_CH).astype(BF16)

    a_r, a_i = ar.reshape(-1), ai.reshape(-1)
    pw = [(a_r, a_i)]
    for _ in range(S5_TILE - 1):
        pr, pim = pw[-1]
        pw.append((pr * a_r - pim * a_i, pr * a_i + pim * a_r))
    rows = jnp.arange(S5_TILE)[:, None]
    coef = []
    for sh in S5_SHIFTS:
        valid = (rows <= S5_TILE - 1 - sh) if rev else (rows >= sh)
        coef.append(jnp.where(valid, pw[sh - 1][0][None, :], 0.0))
        coef.append(jnp.where(valid, pw[sh - 1][1][None, :], 0.0))
    order = list(range(S5_TILE - 1, -1, -1)) if rev else list(range(S5_TILE))
    coef.append(jnp.stack([pw[k][0] for k in order]))
    coef.append(jnp.stack([pw[k][1] for k in order]))
    return (bdiag_in(bb_re), bdiag_in(bb_im), bdiag_out(c_re), bdiag_out(c_im),
            jnp.stack(coef).astype(F32))


def s5_scan(u, h0r, h0i, dparams, rev):
    bn, seq_len, _ = u.shape
    bdr, bdi, cdr, cdi, coef = dparams
    tt = min(256, seq_len)
    nblk = seq_len // tt
    tmap = (lambda b, i: (b, nblk - 1 - i, 0)) if rev else (lambda b, i: (b, i, 0))
    const3 = lambda b, i: (0, 0, 0)
    y, hfr, hfi = pl.pallas_call(
        functools.partial(_s5_scan_body, rev=rev, tt=tt),
        grid=(bn, nblk),
        in_specs=[pl.BlockSpec((1, tt, S5_W), tmap),
                  pl.BlockSpec((1, 1, S5_LANES), lambda b, i: (b, 0, 0)),
                  pl.BlockSpec((1, 1, S5_LANES), lambda b, i: (b, 0, 0)),
                  pl.BlockSpec(bdr.shape, const3),
                  pl.BlockSpec(bdi.shape, const3),
                  pl.BlockSpec(cdr.shape, const3),
                  pl.BlockSpec(cdi.shape, const3),
                  pl.BlockSpec(coef.shape, const3)],
        out_specs=[pl.BlockSpec((1, tt, S5_W), tmap),
                   pl.BlockSpec((1, 1, S5_LANES), lambda b, i: (b, 0, 0)),
                   pl.BlockSpec((1, 1, S5_LANES), lambda b, i: (b, 0, 0))],
        out_shape=[jax.ShapeDtypeStruct((bn, seq_len, S5_W), F32),
                   jax.ShapeDtypeStruct((bn, 1, S5_LANES), F32),
                   jax.ShapeDtypeStruct((bn, 1, S5_LANES), F32)],
        scratch_shapes=[pltpu.VMEM((tt, S5_LANES), F32), pltpu.VMEM((tt, S5_LANES), F32),
                        pltpu.VMEM((1, S5_LANES), F32), pltpu.VMEM((1, S5_LANES), F32)],
        compiler_params=pltpu.CompilerParams(
            dimension_semantics=("parallel", "arbitrary"),
            vmem_limit_bytes=V7X_VMEM_LIMIT),
        name="s5_scan_bwd" if rev else "s5_scan_fwd",
    )(u, h0r.reshape(bn, 1, S5_LANES), h0i.reshape(bn, 1, S5_LANES), bdr, bdi, cdr, cdi, coef)
    return y, hfr.reshape(bn, S5_LANES), hfi.reshape(bn, S5_LANES)


def _s5_glu_body(yf_ref, yb_ref, u_ref, d_ref, w_ref, b_ref, o_ref):
    y = yf_ref[...] + yb_ref[...] + d_ref[...] * u_ref[...]
    g = jax.nn.gelu(y)
    z = jnp.dot(g.astype(BF16), w_ref[...], preferred_element_type=F32) + b_ref[...]
    o_ref[...] = (g * jax.nn.sigmoid(z)).astype(o_ref.dtype)


def s5_glu(yf, yb, u, d, glu_w, glu_b):
    m = yf.shape[0]
    tm = _pick(m, (512, 256, 128, 64, 32, 16, 8))
    row = pl.BlockSpec((tm, S5_W), lambda i: (i, 0))
    vec = pl.BlockSpec((1, S5_W), lambda i: (0, 0))
    return pl.pallas_call(
        _s5_glu_body,
        grid=(m // tm,),
        in_specs=[row, row, row, vec, pl.BlockSpec((S5_W, S5_W), lambda i: (0, 0)), vec],
        out_specs=row,
        out_shape=jax.ShapeDtypeStruct((m, S5_W), F32),
        compiler_params=pltpu.CompilerParams(
            dimension_semantics=("parallel",), vmem_limit_bytes=V7X_VMEM_LIMIT),
        name="s5_glu",
    )(yf, yb, u, d.reshape(1, S5_W), glu_w.astype(BF16), glu_b.reshape(1, S5_W))


def s5_stream(u, state, lam_re, lam_im, log_dt, b_re, b_im, c_re, c_im, d, glu_w, glu_b):
    bn, seq_len, _ = u.shape
    pf = _s5_direction_params(lam_re[0], lam_im[0], log_dt[0], b_re[0], b_im[0], c_re[0], c_im[0], False)
    pb = _s5_direction_params(lam_re[1], lam_im[1], log_dt[1], b_re[1], b_im[1], c_re[1], c_im[1], True)
    yf, fr, fi = s5_scan(u, state[0], state[1], pf, False)
    yb, br, bi = s5_scan(u, state[2], state[3], pb, True)
    m = bn * seq_len
    out = s5_glu(yf.reshape(m, S5_W), yb.reshape(m, S5_W), u.reshape(m, S5_W), d, glu_w, glu_b)
    return out.reshape(bn, seq_len, S5_W), (fr, fi, br, bi)


SSD_TT = 256
SSD_CW = 1024
SSD_R = SSD_H // SSD_G


def _conv_silu_body(xp_ref, x_ref, xn_ref, cw_ref, cb_ref, o_ref, *, tt, nblk):
    blk = pl.program_id(1)
    x = x_ref[0]
    xp = xp_ref[0, tt - V7X_SUBLANES:, :] * (blk > 0).astype(F32)
    xn = xn_ref[0, :V7X_SUBLANES, :] * (blk < nblk - 1).astype(F32)
    xc = jnp.concatenate([xp, x, xn], axis=0)
    base = V7X_SUBLANES - CONV_PAD[0]
    y = cb_ref[...] + sum(cw_ref[j:j + 1, :] * xc[base + j:base + j + tt, :] for j in range(CONV_W))
    o_ref[0] = jax.nn.silu(y)


def conv_silu(x, conv_w, conv_b):
    bn, seq_len, ch = x.shape
    tt = min(SSD_TT, seq_len)
    nblk = seq_len // tt
    cur = lambda b, i, p: (b, i, p)
    prev = lambda b, i, p: (b, jnp.maximum(i - 1, 0), p)
    nxt = lambda b, i, p: (b, jnp.minimum(i + 1, nblk - 1), p)
    return pl.pallas_call(
        functools.partial(_conv_silu_body, tt=tt, nblk=nblk),
        grid=(bn, nblk, ch // SSD_CW),
        in_specs=[pl.BlockSpec((1, tt, SSD_CW), prev), pl.BlockSpec((1, tt, SSD_CW), cur),
                  pl.BlockSpec((1, tt, SSD_CW), nxt),
                  pl.BlockSpec((CONV_W, SSD_CW), lambda b, i, p: (0, p)),
                  pl.BlockSpec((1, SSD_CW), lambda b, i, p: (0, p))],
        out_specs=pl.BlockSpec((1, tt, SSD_CW), cur),
        out_shape=jax.ShapeDtypeStruct((bn, seq_len, ch), F32),
        compiler_params=pltpu.CompilerParams(
            dimension_semantics=("parallel", "parallel", "arbitrary"),
            vmem_limit_bytes=V7X_VMEM_LIMIT),
        name="conv_silu",
    )(x, x, x, conv_w, conv_b.reshape(1, ch))


def _ssd_scan_body(prm_ref, x_ref, b_ref, c_ref, dc_ref, dr_ref, s0_ref, y_ref, sf_ref, s_s, *, rev):
    c = pl.program_id(1)

    @pl.when(c == 0)
    def _():
        s_s[...] = s0_ref[0]

    qn = SSD_CHUNK
    ri = lax.broadcasted_iota(jnp.int32, (qn, qn), 0)
    ci = lax.broadcasted_iota(jnp.int32, (qn, qn), 1)
    incl = (ri <= ci) if rev else (ri >= ci)
    incl_t = (ri >= ci) if rev else (ri <= ci)
    end = 0 if rev else qn - 1
    nt = (((1,), (1,)), ((), ()))
    tn = (((0,), (0,)), ((), ()))
    for g in range(SSD_G):
        bm = b_ref[0, :, SSD_N * g:SSD_N * (g + 1)].astype(BF16)
        cm = c_ref[0, :, SSD_N * g:SSD_N * (g + 1)].astype(BF16)
        cb = lax.dot_general(cm, bm, nt, preferred_element_type=F32)
        heads = range(SSD_R * g, SSD_R * (g + 1))
        dts, cums, segs = [], [], []
        for h in heads:
            a_neg = prm_ref[0, h]
            dtb = prm_ref[1, h]
            dt_col = jax.nn.softplus(dc_ref[0, :, h:h + 1] + dtb)
            dt_row = jax.nn.softplus(dr_ref[0, 0, h:h + 1, :] + dtb)
            cum_col = jnp.sum(jnp.where(incl, dt_row * a_neg, 0.0), axis=1, keepdims=True)
            cum_row = jnp.sum(jnp.where(incl_t, dt_col * a_neg, 0.0), axis=0, keepdims=True)
            dts.append(dt_col)
            cums.append(cum_col)
            segs.append(jnp.where(incl, jnp.exp(cum_col - cum_row), 0.0) * cb)
        xdts = [x_ref[0, :, SSD_P * h:SSD_P * (h + 1)] * dts[i] for i, h in enumerate(heads)]
        ydiags = [jnp.dot(segs[i].astype(BF16), xdts[i].astype(BF16), preferred_element_type=F32)
                  for i in range(SSD_R)]
        ss = [s_s[h] for h in heads]
        yoffs = [lax.dot_general(cm, ss[i].astype(BF16), nt, preferred_element_type=F32) * jnp.exp(cums[i])
                 for i in range(SSD_R)]
        ys = [ydiags[i] + yoffs[i] for i in range(SSD_R)]
        for i in range(0, SSD_R, 2):
            lo = SSD_P * (SSD_R * g + i)
            y_ref[0, :, lo:lo + 2 * SSD_P] = jnp.concatenate([ys[i], ys[i + 1]], axis=1)
        for i, h in enumerate(heads):
            cend = cums[i][end:end + 1, :]
            xw = (xdts[i] * jnp.exp(cend - cums[i])).astype(BF16)
            s_new = ss[i] * jnp.exp(cend) + lax.dot_general(xw, bm, tn, preferred_element_type=F32)
            s_s[h] = s_new
            sf_ref[0, h] = s_new


def ssd_scan_pallas(xs, bm, cm, dt_col, dt_row, prm, s0, rev):
    bn, seq_len, _ = xs.shape
    qn = SSD_CHUNK
    nc = seq_len // qn
    cidx = (lambda c: nc - 1 - c) if rev else (lambda c: c)
    tok = lambda w: pl.BlockSpec((1, qn, w), lambda b, c: (b, cidx(c), 0))
    st = pl.BlockSpec((1, SSD_H, SSD_P, SSD_N), lambda b, c: (b, 0, 0, 0))
    return pl.pallas_call(
        functools.partial(_ssd_scan_body, rev=rev),
        grid=(bn, nc),
        in_specs=[pl.BlockSpec(memory_space=pltpu.SMEM), tok(SSD_DI), tok(SSD_G * SSD_N), tok(SSD_G * SSD_N),
                  tok(SSD_H), pl.BlockSpec((1, 1, SSD_H, qn), lambda b, c: (b, cidx(c), 0, 0)), st],
        out_specs=[tok(SSD_DI), st],
        out_shape=[jax.ShapeDtypeStruct((bn, seq_len, SSD_DI), F32),
                   jax.ShapeDtypeStruct((bn, SSD_H, SSD_P, SSD_N), F32)],
        scratch_shapes=[pltpu.VMEM((SSD_H, SSD_P, SSD_N), F32)],
        compiler_params=pltpu.CompilerParams(
            dimension_semantics=("parallel", "arbitrary"),
            vmem_limit_bytes=V7X_VMEM_LIMIT),
        name="ssd_scan_bwd" if rev else "ssd_scan_fwd",
    )(prm, xs, bm, cm, dt_col, dt_row, s0)


def _ssd_post_body(yf_ref, yb_ref, xs_ref, z_ref, d_ref, g_ref, o_ref):
    y = (yf_ref[...] + yb_ref[...] + d_ref[...] * xs_ref[...]) * jax.nn.silu(z_ref[...])
    o_ref[...] = y * lax.rsqrt(jnp.mean(y * y, axis=-1, keepdims=True) + EPS) * g_ref[...]


def ssd_post(yf, yb, xs, z, d_vec, norm_g):
    m = yf.shape[0]
    tm = _pick(m, (SSD_TT, 128, 64, 32, 16, 8))
    row = pl.BlockSpec((tm, SSD_DI), lambda i: (i, 0))
    vec = pl.BlockSpec((1, SSD_DI), lambda i: (0, 0))
    return pl.pallas_call(
        _ssd_post_body, grid=(m // tm,), in_specs=[row, row, row, row, vec, vec], out_specs=row,
        out_shape=jax.ShapeDtypeStruct((m, SSD_DI), F32),
        compiler_params=pltpu.CompilerParams(
            dimension_semantics=("parallel",), vmem_limit_bytes=V7X_VMEM_LIMIT),
        name="ssd_post",
    )(yf, yb, xs, z, d_vec, norm_g.reshape(1, SSD_DI))


def ssd_stream(z, xbc, dt_raw, state, conv_w, conv_b, dt_bias, a_log, d, norm_g):
    bn, seq_len, _ = z.shape
    xbc = conv_silu(xbc, conv_w, conv_b)
    xs, bm, cm = jnp.split(xbc, [SSD_DI, SSD_DI + SSD_G * SSD_N], axis=-1)
    nc = seq_len // SSD_CHUNK
    dtr = dt_raw.reshape(bn, seq_len, 2, SSD_H)
    outs = []
    for dr in range(2):
        dt_col = dtr[:, :, dr]
        dt_row = jnp.transpose(dt_col.reshape(bn, nc, SSD_CHUNK, SSD_H), (0, 1, 3, 2))
        prm = jnp.stack([-jnp.exp(a_log[dr]), dt_bias[dr]]).astype(F32)
        outs.append(ssd_scan_pallas(xs, bm, cm, dt_col, dt_row, prm, state[dr], dr == 1))
    (yf, hf), (yb, hb) = outs
    m = bn * seq_len
    d_vec = jnp.repeat(d, SSD_P).reshape(1, SSD_DI)
    out = ssd_post(yf.reshape(m, SSD_DI), yb.reshape(m, SSD_DI), xs.reshape(m, SSD_DI),
                   z.reshape(m, SSD_DI), d_vec, norm_g)
    return out.reshape(bn, seq_len, SSD_DI), (hf, hb)


def gated_delta_rule(q, k, v, beta, g, s0):
    bn, seq_len, nh, dk = q.shape
    dv = v.shape[-1]
    qn = DN_CHUNK
    nc = seq_len // qn

    def chunks(t):
        return jnp.swapaxes(t.reshape((bn, nc, qn) + t.shape[2:]), 2, 3)
    qc, kc, vc, bc, gc = (chunks(t) for t in (q, k, v, beta, g))
    gcum = jnp.cumsum(gc, axis=-1)
    idx = jnp.arange(qn)
    incl = idx[:, None] >= idx[None, :]
    strict = idx[:, None] > idx[None, :]
    dmat = jnp.exp(jnp.where(incl, gcum[..., :, None] - gcum[..., None, :], -jnp.inf))
    kb = kc * bc[..., None]
    m = jnp.where(strict, jnp.einsum('bnhid,bnhjd->bnhij', kb, kc) * dmat, 0.0)
    rhs = jnp.concatenate([vc * bc[..., None], kb * jnp.exp(gcum)[..., None]], axis=-1)
    nmat = -m
    tinv = jnp.eye(qn, dtype=m.dtype) + nmat
    npow = nmat
    for _ in range(DN_NEUMANN):
        npow = jnp.einsum('...ij,...jk->...ik', npow, npow, precision=lax.Precision.HIGHEST)
        tinv = tinv + jnp.einsum('...ij,...jk->...ik', tinv, npow, precision=lax.Precision.HIGHEST)
    sol = jnp.einsum('...ij,...jk->...ik', tinv, rhs)
    u, w = sol[..., :dv], sol[..., dv:]
    qk = jnp.einsum('bnhid,bnhjd->bnhij', qc, kc) * dmat
    q_dec = qc * jnp.exp(gcum)[..., None]
    k_tail = kc * jnp.exp(gcum[..., -1:] - gcum)[..., None]
    tot = jnp.exp(gcum[..., -1])

    def step(s, inp):
        u_c, w_c, qk_c, qd_c, kt_c, tot_c = inp
        v_new = u_c - jnp.einsum('bhqk,bhkv->bhqv', w_c, s)
        o = jnp.einsum('bhqk,bhkv->bhqv', qd_c, s) + jnp.einsum('bhij,bhjv->bhiv', qk_c, v_new)
        s = s * tot_c[..., None, None] + jnp.einsum('bhqk,bhqv->bhkv', kt_c, v_new)
        return s, o
    xs = tuple(jnp.moveaxis(t, 1, 0) for t in (u, w, qk, q_dec, k_tail, tot))
    s_last, o = lax.scan(step, s0, xs)
    o = jnp.swapaxes(jnp.moveaxis(o, 0, 1), 2, 3).reshape(bn, seq_len, nh, dv)
    return o, s_last


DN_NEUMANN = 5


def deltanet_stream(qkv, z, a_raw, b_raw, state, conv_w, conv_b, dt_bias, a_log, norm_g):
    bn, seq_len, _ = z.shape
    qkv = jax.nn.silu(dwconv(qkv, conv_w, conv_b))
    q, k, v = jnp.split(qkv, [DN_H * DN_DK, 2 * DN_H * DN_DK], axis=-1)
    q = l2norm(q.reshape(bn, seq_len, DN_H, DN_DK)) * (DN_DK ** -0.5)
    k = l2norm(k.reshape(bn, seq_len, DN_H, DN_DK))
    v = v.reshape(bn, seq_len, DN_H, DN_DV)
    beta = jax.nn.sigmoid(b_raw.reshape(bn, seq_len, 2, DN_H))
    g = -jnp.exp(a_log) * jax.nn.softplus(a_raw.reshape(bn, seq_len, 2, DN_H) + dt_bias)
    of, sf = gated_delta_rule(q, k, v, beta[:, :, 0], g[:, :, 0], state[0])
    ob, sb = gated_delta_rule(flip(q), flip(k), flip(v), flip(beta[:, :, 1]), flip(g[:, :, 1]), state[1])
    o = rmsnorm(of + flip(ob), norm_g)
    o = o.reshape(bn, seq_len, DN_H * DN_DV) * jax.nn.silu(z)
    return o, (sf, sb)


RG_TT = 256


def _rglru_body(xp_ref, x_ref, xn_ref, h0_ref, cw_ref, cb_ref, wa_ref, ba_ref, wx_ref, bx_ref, c_ref,
                h_ref, hf_ref, carry_s, *, rev, tt, nblk):
    i = pl.program_id(1)
    blk = (nblk - 1 - i) if rev else i

    @pl.when(i == 0)
    def _():
        carry_s[...] = h0_ref[0]

    x = x_ref[0]
    xp = xp_ref[0, tt - V7X_SUBLANES:, :] * (blk > 0).astype(F32)
    xn = xn_ref[0, :V7X_SUBLANES, :] * (blk < nblk - 1).astype(F32)
    xc = jnp.concatenate([xp, x, xn], axis=0)
    base = V7X_SUBLANES - CONV_PAD[0]
    xh = cb_ref[...] + sum(cw_ref[j:j + 1, :] * xc[base + j:base + j + tt, :] for j in range(CONV_W))

    xb = xh.astype(BF16)
    rs, is_ = [], []
    for h in range(RG_H):
        xs = xb[:, RG_BS * h:RG_BS * (h + 1)]
        rs.append(jnp.dot(xs, wa_ref[h], preferred_element_type=F32))
        is_.append(jnp.dot(xs, wx_ref[h], preferred_element_type=F32))
    r = jax.nn.sigmoid(jnp.concatenate(rs, axis=1) + ba_ref[...])
    ig = jax.nn.sigmoid(jnp.concatenate(is_, axis=1) + bx_ref[...])
    log_a = r * c_ref[...]
    a = jnp.exp(log_a)
    b = jnp.sqrt(1.0 - jnp.exp(2.0 * log_a)) * (ig * xh)

    row = lax.broadcasted_iota(jnp.int32, (tt, RG_W), 0)
    s = 1
    while s < tt:
        valid = (row < tt - s) if rev else (row >= s)
        shift = (tt - s) if rev else s
        a_s = jnp.where(valid, pltpu.roll(a, shift, 0), 1.0)
        b_s = jnp.where(valid, pltpu.roll(b, shift, 0), 0.0)
        b = a * b_s + b
        a = a * a_s
        s *= 2
    hcur = b + a * carry_s[...]
    h_ref[0] = hcur
    last = 0 if rev else tt - 1
    carry_s[...] = hcur[last:last + 1, :]
    hf_ref[0] = hcur[last:last + 1, :]


def rglru_scan(xr, h0, conv_w, conv_b, wa, ba, wx, bx, lam, rev):
    bn, seq_len, _ = xr.shape
    tt = min(RG_TT, seq_len)
    nblk = seq_len // tt
    blk = (lambda i: nblk - 1 - i) if rev else (lambda i: i)
    cur = lambda b, i: (b, blk(i), 0)
    prev = lambda b, i: (b, jnp.maximum(blk(i) - 1, 0), 0)
    nxt = lambda b, i: (b, jnp.minimum(blk(i) + 1, nblk - 1), 0)
    vec = pl.BlockSpec((1, RG_W), lambda b, i: (0, 0))
    wspec = pl.BlockSpec((RG_H, RG_BS, RG_BS), lambda b, i: (0, 0, 0))
    st = pl.BlockSpec((1, 1, RG_W), lambda b, i: (b, 0, 0))
    c = (-RG_C * jax.nn.softplus(-lam)).reshape(1, RG_W)
    h, hf = pl.pallas_call(
        functools.partial(_rglru_body, rev=rev, tt=tt, nblk=nblk),
        grid=(bn, nblk),
        in_specs=[pl.BlockSpec((1, tt, RG_W), prev), pl.BlockSpec((1, tt, RG_W), cur),
                  pl.BlockSpec((1, tt, RG_W), nxt), st,
                  pl.BlockSpec((CONV_W, RG_W), lambda b, i: (0, 0)), vec,
                  wspec, vec, wspec, vec, vec],
        out_specs=[pl.BlockSpec((1, tt, RG_W), cur), st],
        out_shape=[jax.ShapeDtypeStruct((bn, seq_len, RG_W), F32),
                   jax.ShapeDtypeStruct((bn, 1, RG_W), F32)],
        scratch_shapes=[pltpu.VMEM((1, RG_W), F32)],
        compiler_params=pltpu.CompilerParams(
            dimension_semantics=("parallel", "arbitrary"),
            vmem_limit_bytes=V7X_VMEM_LIMIT),
        name="rglru_bwd" if rev else "rglru_fwd",
    )(xr, xr, xr, h0.reshape(bn, 1, RG_W), conv_w, conv_b.reshape(1, RG_W),
      wa.astype(BF16), ba.reshape(1, RG_W), wx.astype(BF16), bx.reshape(1, RG_W), c)
    return h, hf.reshape(bn, RG_W)


def _rg_combine_body(hf_ref, hb_ref, g_ref, o_ref):
    o_ref[...] = (hf_ref[...] + hb_ref[...]) * jax.nn.gelu(g_ref[...])


def rg_combine(hf, hb, gate):
    m = hf.shape[0]
    tm = _pick(m, (1024, 512, 256, 128, 64, 32, 16, 8))
    row = pl.BlockSpec((tm, RG_W), lambda i: (i, 0))
    return pl.pallas_call(
        _rg_combine_body, grid=(m // tm,), in_specs=[row, row, row], out_specs=row,
        out_shape=jax.ShapeDtypeStruct((m, RG_W), F32),
        compiler_params=pltpu.CompilerParams(dimension_semantics=("parallel",)),
        name="rg_combine",
    )(hf, hb, gate)


def rglru_stream(xr, gate, state, conv_w, conv_b, wa, ba, wx, bx, lam):
    bn, seq_len, _ = xr.shape
    hf, sf = rglru_scan(xr, state[0], conv_w, conv_b, wa[0], ba[0], wx[0], bx[0], lam[0], False)
    hb, sb = rglru_scan(xr, state[1], conv_w, conv_b, wa[1], ba[1], wx[1], bx[1], lam[1], True)
    m = bn * seq_len
    y = rg_combine(hf.reshape(m, RG_W), hb.reshape(m, RG_W), gate.reshape(m, RG_W))
    return y.reshape(bn, seq_len, RG_W), (sf, sb)


def even_mixer(hx, hc, w_in, w_out, s5_params, ssd_params, need_ctx):
    bn = hx.shape[0]

    def stream(h, st_s5, st_ssd):
        p = mm3(h, w_in)
        u, z, xbc, dt_raw = jnp.split(p, EV_CUTS, axis=-1)
        ya, st_a = s5_stream(u, st_s5, *s5_params)
        yb, st_b = ssd_stream(z, xbc, dt_raw, st_ssd, *ssd_params)
        return jnp.concatenate([ya, yb], axis=-1), st_a, st_b
    zs5 = jnp.zeros((bn, S5_LANES), F32)
    zssd = jnp.zeros((bn, SSD_H, SSD_P, SSD_N), F32)
    yc, st_a, st_b = stream(hc, (zs5, zs5, zs5, zs5), (zssd, zssd))
    yx, _, _ = stream(hx, st_a, st_b)
    return mm3(yx, w_out), (mm3(yc, w_out) if need_ctx else None)


def odd_mixer(hx, hc, w_in, w_out, dn_params, rg_params, need_ctx):
    bn = hx.shape[0]

    def stream(h, st_dn, st_rg):
        p = mm3(h, w_in)
        qkv, z, a_raw, b_raw, xr, gate = jnp.split(p, OD_CUTS, axis=-1)
        yd, st_d = deltanet_stream(qkv, z, a_raw, b_raw, st_dn, *dn_params)
        yr, st_r = rglru_stream(xr, gate, st_rg, *rg_params)
        return jnp.concatenate([yd, yr], axis=-1), st_d, st_r
    zdn = jnp.zeros((bn, DN_H, DN_DK, DN_DV), F32)
    zrg = jnp.zeros((bn, RG_W), F32)
    yc, st_d, st_r = stream(hc, (zdn, zdn), (zrg, zrg))
    yx, _, _ = stream(hx, st_d, st_r)
    return mm3(yx, w_out), (mm3(yc, w_out) if need_ctx else None)


PEER_TM = 512
PEER_EB = 512
PEER_RT = 16
PEER_SEL = PEER_H * PEER_TOPK


def _peer_body(flags_ref, ht_ref, u_ref, vt_ref, s1_ref, s2_ref, e1_ref, e2_ref, tau_ref,
               pidx_ref, ridx_ref, gate_ref, o_ref, st_s, g_s, *, tm, eb):
    i = pl.program_id(0)
    j = pl.program_id(1)
    npk = eb // PEER_NK

    @pl.when(j == 0)
    def _():
        o_ref[...] = jnp.zeros_like(o_ref)

    @pl.when(flags_ref[i] == 0)
    def _():
        st_s[...] = jnp.dot(u_ref[...], ht_ref[...], preferred_element_type=F32)
        for lg in range(tm // V7X_LANES):
            ls = slice(V7X_LANES * lg, V7X_LANES * (lg + 1))
            t_rows = [tau_ref[h:h + 1, ls] for h in range(PEER_H)]
            a_tiles = [s1_ref[j * npk + pk, :, ls] for pk in range(npk)]
            z_tiles = [e1_ref[j * npk + pk, :, ls] for pk in range(npk)]
            for rt in range(PEER_NK // PEER_RT):
                rs = slice(rt * PEER_RT, (rt + 1) * PEER_RT)
                s2 = [s2_ref[h, rs, ls] for h in range(PEER_H)]
                e2 = [e2_ref[h, rs, ls] for h in range(PEER_H)]
                for pk in range(npk):
                    acc = jnp.zeros((PEER_RT, V7X_LANES), F32)
                    for h in range(PEER_H):
                        hit = a_tiles[pk][h:h + 1, :] + s2[h] >= t_rows[h]
                        acc = acc + jnp.where(hit, z_tiles[pk][h:h + 1, :] * e2[h], 0.0)
                    rows = slice(pk * PEER_NK + rt * PEER_RT, pk * PEER_NK + (rt + 1) * PEER_RT)
                    g_s[rows, ls] = (acc * jax.nn.gelu(st_s[rows, ls])).astype(BF16)
        o_ref[...] += jnp.dot(vt_ref[...], g_s[...], preferred_element_type=F32)

    @pl.when(flags_ref[i] != 0)
    def _():
        st_s[...] = jnp.dot(u_ref[...], ht_ref[...], preferred_element_type=F32)
        riota = lax.broadcasted_iota(jnp.int32, (PEER_NK, V7X_LANES), 0)
        for lg in range(tm // V7X_LANES):
            ls = slice(V7X_LANES * lg, V7X_LANES * (lg + 1))
            for pk in range(npk):
                p = j * npk + pk

                def kbody(k8, w, ls=ls, p=p):
                    k0 = pl.multiple_of(k8 * V7X_SUBLANES, V7X_SUBLANES)
                    ptile = pidx_ref[pl.ds(k0, V7X_SUBLANES), ls]
                    rtile = ridx_ref[pl.ds(k0, V7X_SUBLANES), ls]
                    gtile = gate_ref[pl.ds(k0, V7X_SUBLANES), ls]
                    ctile = jnp.where(ptile == p, gtile, 0.0)
                    for k in range(V7X_SUBLANES):
                        w = w + jnp.where(rtile[k:k + 1, :] == riota, ctile[k:k + 1, :], 0.0)
                    return w

                w = lax.fori_loop(0, PEER_SEL // V7X_SUBLANES, kbody,
                                  jnp.zeros((PEER_NK, V7X_LANES), F32))
                rows = slice(pk * PEER_NK, (pk + 1) * PEER_NK)
                g_s[rows, ls] = (w * jax.nn.gelu(st_s[rows, ls])).astype(BF16)
        o_ref[...] += jnp.dot(vt_ref[...], g_s[...], preferred_element_type=F32)


def peer_experts(flags, ht, u_bf, vt_bf, s1t, s2t, e1t, e2t, taut, pidx, ridx, gate, tm):
    dm, t = ht.shape
    eb = PEER_EB
    head3 = pl.BlockSpec((PEER_H, PEER_NK, tm), lambda i, j, f: (0, 0, i))
    key3 = pl.BlockSpec((PEER_NK, PEER_H, tm), lambda i, j, f: (0, 0, i))
    sel2 = pl.BlockSpec((PEER_SEL, tm), lambda i, j, f: (0, i))
    grid_spec = pltpu.PrefetchScalarGridSpec(
        num_scalar_prefetch=1,
        grid=(t // tm, PEER_E // eb),
        in_specs=[pl.BlockSpec((dm, tm), lambda i, j, f: (0, i)),
                  pl.BlockSpec((eb, dm), lambda i, j, f: (j, 0)),
                  pl.BlockSpec((dm, eb), lambda i, j, f: (0, j)),
                  key3, head3, key3, head3,
                  pl.BlockSpec((PEER_H, tm), lambda i, j, f: (0, i)),
                  sel2, sel2, sel2],
        out_specs=pl.BlockSpec((dm, tm), lambda i, j, f: (0, i)),
        scratch_shapes=[pltpu.VMEM((eb, tm), F32), pltpu.VMEM((eb, tm), BF16)])
    return pl.pallas_call(
        functools.partial(_peer_body, tm=tm, eb=eb),
        grid_spec=grid_spec,
        out_shape=jax.ShapeDtypeStruct((dm, t), F32),
        compiler_params=pltpu.CompilerParams(
            dimension_semantics=("parallel", "arbitrary"),
            vmem_limit_bytes=V7X_VMEM_LIMIT),
        name="peer_experts",
    )(flags, ht, u_bf, vt_bf, s1t, s2t, e1t, e2t, taut, pidx, ridx, gate)


def _topk_rows(x, k):
    n = x.shape[0]
    iota = lax.broadcasted_iota(jnp.int32, x.shape, 0)
    vals, idxs = [], []
    for r in range(k):
        m = jnp.max(x, axis=0, keepdims=True)
        idx = jnp.min(jnp.where(x == m, iota, n), axis=0, keepdims=True)
        vals.append(m)
        idxs.append(idx)
        if r < k - 1:
            x = jnp.where(iota == idx, -jnp.inf, x)
    return vals, idxs


def _peer_select_body(q_ref, k1_ref, k2_ref, s1_ref, s2_ref, e1_ref, e2_ref, tau_ref, tie_ref,
                      pidx_ref, ridx_ref, gate_ref):
    qb = q_ref[...].astype(BF16)
    dn = (((1,), (1,)), ((), ()))
    s1 = lax.dot_general(k1_ref[0], qb[:, :PEER_HALF], dn, preferred_element_type=F32)
    s2 = lax.dot_general(k2_ref[0], qb[:, PEER_HALF:], dn, preferred_element_type=F32)
    t1, i1 = _topk_rows(s1, PEER_TOPK + 1)
    t2, i2 = _topk_rows(s2, PEER_TOPK + 1)
    t1blk = jnp.concatenate(t1[:PEER_TOPK], axis=0)
    t2blk = jnp.concatenate(t2[:PEER_TOPK], axis=0)
    hs = V7X_SUBLANES
    cand = jnp.concatenate([t1[0] + t2blk] + [t1[a] + t2blk[:hs] for a in range(1, hs)]
                           + [t1blk[hs:] + t2[0]], axis=0)
    top, ci = _topk_rows(cand, PEER_TOPK)
    topb = jnp.concatenate(top, axis=0)
    rowb = jnp.concatenate(ci, axis=0)
    mid = rowb - PEER_TOPK
    ca = jnp.where(rowb < PEER_TOPK, 0, jnp.where(mid < hs * (hs - 1), 1 + (mid >> 3), mid - hs * (hs - 2)))
    cb = jnp.where(rowb < PEER_TOPK, rowb, jnp.where(mid < hs * (hs - 1), mid & (hs - 1), 0))
    pid = jnp.zeros_like(rowb)
    rid = jnp.zeros_like(rowb)
    for a in range(PEER_TOPK):
        pid = pid + jnp.where(ca == a, i1[a], 0)
        rid = rid + jnp.where(cb == a, i2[a], 0)
    ex = jnp.exp(topb - top[0])
    zsum = jnp.sum(ex, axis=0, keepdims=True)
    tau = top[PEER_TOPK - 1]
    hits = sum(((t1[a] + t2blk) >= tau).astype(jnp.int32) for a in range(PEER_TOPK))
    cnt = jnp.sum(hits, axis=0, keepdims=True)
    tie = ((cnt != PEER_TOPK) | (t1[PEER_TOPK] + t2[0] >= tau) | (t1[0] + t2[PEER_TOPK] >= tau))
    s1_ref[0] = s1
    s2_ref[0] = s2
    e1_ref[0] = jnp.exp(s1 - t1[0]) / zsum
    e2_ref[0] = jnp.exp(s2 - t2[0])
    tau_ref[0] = tau
    tie_ref[0] = tie.astype(jnp.int32)
    pidx_ref[...] = pid
    ridx_ref[...] = rid
    gate_ref[...] = ex / zsum


def peer_select(q, k1_bf, k2_bf):
    t = q.shape[0]
    tl = V7X_LANES
    head3 = pl.BlockSpec((1, PEER_NK, tl), lambda i, h: (h, 0, i))
    row3 = pl.BlockSpec((1, 1, tl), lambda i, h: (h, 0, i))
    sel2 = pl.BlockSpec((PEER_TOPK, tl), lambda i, h: (h, i))
    key_spec = pl.BlockSpec((1, PEER_NK, PEER_HALF), lambda i, h: (h, 0, 0))
    f3 = jax.ShapeDtypeStruct((PEER_H, PEER_NK, t), F32)
    return pl.pallas_call(
        _peer_select_body,
        grid=(t // tl, PEER_H),
        in_specs=[pl.BlockSpec((tl, PEER_DK), lambda i, h: (i, h)), key_spec, key_spec],
        out_specs=[head3, head3, head3, head3, row3, row3, sel2, sel2, sel2],
        out_shape=[f3, f3, f3, f3,
                   jax.ShapeDtypeStruct((PEER_H, 1, t), F32),
                   jax.ShapeDtypeStruct((PEER_H, 1, t), jnp.int32),
                   jax.ShapeDtypeStruct((PEER_SEL, t), jnp.int32),
                   jax.ShapeDtypeStruct((PEER_SEL, t), jnp.int32),
                   jax.ShapeDtypeStruct((PEER_SEL, t), F32)],
        compiler_params=pltpu.CompilerParams(
            dimension_semantics=("parallel", "arbitrary"),
            vmem_limit_bytes=V7X_VMEM_LIMIT),
        name="peer_select",
    )(q, k1_bf, k2_bf)


def peer(h, w_q, k1, k2, u_bf, vt_bf, tm=PEER_TM):
    bn, seq_len, dm = h.shape
    t = bn * seq_len
    q = mm(h.reshape(t, dm), w_q)
    s1t, s2t, e1t, e2t, taut, tiet, pidx, ridx, gate = peer_select(q, k1.astype(BF16), k2.astype(BF16))
    flags = jnp.any(tiet.reshape(PEER_H, t // tm, tm) != 0, axis=(0, 2)).astype(jnp.int32)
    outt = peer_experts(
        flags, h.reshape(t, dm).T.astype(BF16), u_bf, vt_bf,
        jnp.transpose(s1t, (1, 0, 2)), s2t, jnp.transpose(e1t, (1, 0, 2)), e2t, taut.reshape(PEER_H, t),
        pidx, ridx, gate, tm)
    return outt.T.reshape(bn, seq_len, dm)


def kernel(x, c, ctx, c_ctx, ada_w, ada_b, norm1_g, norm2_g, final_g, ev_w_in, ev_w_out, s5_lam_re, s5_lam_im, s5_log_dt, s5_b_re, s5_b_im, s5_c_re, s5_c_im, s5_d, s5_glu_w, s5_glu_b, ssd_conv_w, ssd_conv_b, ssd_dt_bias, ssd_a_log, ssd_d, ssd_norm_g, od_w_in, od_w_out, dn_conv_w, dn_conv_b, dn_dt_bias, dn_a_log, dn_norm_g, rg_conv_w, rg_conv_b, rg_wa, rg_ba, rg_wx, rg_bx, rg_lam, peer_wq, peer_k1, peer_k2, peer_u, peer_v):
    bn, seq_len, _ = x.shape
    rows = seq_len // GRID_W
    sc = jax.nn.silu(c)
    scc = jax.nn.silu(c_ctx)
    for layer in range(DEPTH):
        j = layer // 2
        need_ctx = layer < DEPTH - 1
        mx = (sc @ ada_w[layer] + ada_b[layer]).reshape(bn, 6, 1, D_MODEL)
        mc = (scc @ ada_w[layer] + ada_b[layer]).reshape(6, D_MODEL)
        hx = modulate(rmsnorm(x, norm1_g[layer]), mx[:, 0], mx[:, 1])
        hc = modulate(rmsnorm(ctx, norm1_g[layer]), mc[0], mc[1])
        if layer % 2 == 0:
            s5_params = (s5_lam_re[j], s5_lam_im[j], s5_log_dt[j], s5_b_re[j], s5_b_im[j],
                         s5_c_re[j], s5_c_im[j], s5_d[j], s5_glu_w[j], s5_glu_b[j])
            ssd_params = (ssd_conv_w[j], ssd_conv_b[j], ssd_dt_bias[j], ssd_a_log[j],
                          ssd_d[j], ssd_norm_g[j])
            ox, oc = even_mixer(hx, hc, ev_w_in[j], ev_w_out[j], s5_params, ssd_params, need_ctx)
        else:
            dn_params = (dn_conv_w[j], dn_conv_b[j], dn_dt_bias[j], dn_a_log[j], dn_norm_g[j])
            rg_params = (rg_conv_w[j], rg_conv_b[j], rg_wa[j], rg_ba[j], rg_wx[j], rg_bx[j], rg_lam[j])
            ox, oc = odd_mixer(to_col_major(hx, rows), hc, od_w_in[j], od_w_out[j],
                               dn_params, rg_params, need_ctx)
            ox = to_row_major(ox, rows)
        x = x + mx[:, 2] * ox
        hx = modulate(rmsnorm(x, norm2_g[layer]), mx[:, 3], mx[:, 4])
        u_bf = peer_u[layer].astype(BF16)
        vt_bf = peer_v[layer].T.astype(BF16)
        x = x + mx[:, 5] * peer(hx, peer_wq[layer], peer_k1[layer], peer_k2[layer], u_bf, vt_bf)
        if need_ctx:
            ctx = ctx + mc[2] * oc
            hc = modulate(rmsnorm(ctx, norm2_g[layer]), mc[3], mc[4])
            ctx = ctx + mc[5] * peer(hc, peer_wq[layer], peer_k1[layer], peer_k2[layer], u_bf, vt_bf)
    return rmsnorm(x, final_g)
```
